```python
import math
import jax, jax.numpy as jnp
from jax import lax
import numpy as np

D_MODEL = 1024
BATCH = 16
SEQ = 4096
DEPTH = 1

HEAD_DIM = 64
RWKV_HEADS = 8
RWKV_DIM = RWKV_HEADS * HEAD_DIM
DECAY_LORA = 64
AAA_LORA = 64
GATE_LORA = 128
GN_EPS = 64e-5
ATTN_HEADS = 8
ATTN_DIM = ATTN_HEADS * HEAD_DIM
IDX_HEADS = 8
IDX_DIM = 64
MAX_TOPK = 256
Q_BLOCK = 128
N_BUCKETS = 32
MAX_EXACT = N_BUCKETS // 2
MAX_DISTANCE = 128
N_GROUPS = 4
EXPERTS_PER_GROUP = 8
N_EXPERTS = N_GROUPS * EXPERTS_PER_GROUP
EXPERT_TOP_K = 2
D_EXPERT = 512
LN_EPS = 1e-5
ALPHA = (2 * DEPTH) ** 0.25
BETA = (8 * DEPTH) ** -0.25

RWKV_SIZES = [RWKV_DIM, DECAY_LORA, RWKV_DIM, RWKV_DIM, AAA_LORA, GATE_LORA]
RWKV_COLS = sum(RWKV_SIZES)
IN_SIZES = [RWKV_COLS, ATTN_DIM, ATTN_DIM, ATTN_DIM,
            IDX_HEADS * IDX_DIM, IDX_DIM, IDX_HEADS, D_MODEL, D_MODEL]
N_IN = sum(IN_SIZES)

kernel_name = "hybrid_rwkv7_dsa_hmoe_deepnorm"


def _split(t, sizes):
    return jnp.split(t, [int(i) for i in np.cumsum(sizes)[:-1]], axis=-1)


def _layer_norm(x, g, b, eps=LN_EPS):
    xf = x.astype(jnp.float32)
    mu = jnp.mean(xf, -1, keepdims=True)
    var = jnp.mean(jnp.square(xf - mu), -1, keepdims=True)
    return ((xf - mu) * lax.rsqrt(var + eps)) * g + b


def _t5_bucket(n):
    n = jnp.maximum(n, 0)
    nf = jnp.maximum(n, 1).astype(jnp.float32)
    large = MAX_EXACT + (jnp.log(nf / MAX_EXACT) / math.log(MAX_DISTANCE / MAX_EXACT)
                         * (N_BUCKETS - MAX_EXACT)).astype(jnp.int32)
    large = jnp.minimum(large, N_BUCKETS - 1)
    return jnp.where(n < MAX_EXACT, n, large)


def _rwkv7_step(state, inp):
    r, w, k, v, a, b = inp
    sa = jnp.einsum('bhij,bhj->bhi', state, a)
    state = state * w[:, :, None, :] + sa[..., None] * b[:, :, None, :] + v[..., None] * k[:, :, None, :]
    y = jnp.einsum('bhij,bhj->bhi', state, r)
    return state, y


def _rwkv7_mix(p, mu_shift, w0, w_lora_up, a0, a_lora_up, g_lora_up, k_k, k_a, r_k, ln_x_g, ln_x_b):
    B, S, _ = p.shape
    prev = jnp.pad(p[:, :-1], ((0, 0), (1, 0), (0, 0)))
    p = p + (prev - p) * mu_shift
    r, wd, k, v, ad, gd = _split(p, RWKV_SIZES)
    w = -jax.nn.softplus(-(w0 + jnp.tanh(wd) @ w_lora_up)) - 0.5
    decay = jnp.exp(-jnp.exp(w.astype(jnp.float32)))
    a = jax.nn.sigmoid(a0 + ad @ a_lora_up)
    g = jax.nn.sigmoid(gd) @ g_lora_up
    heads = lambda t: t.reshape(B, S, RWKV_HEADS, HEAD_DIM).astype(jnp.float32)
    kk = heads(k * k_k)
    kk = kk / jnp.maximum(jnp.sqrt(jnp.sum(kk * kk, -1, keepdims=True)), 1e-12)
    k = k * (1.0 + (a - 1.0) * k_a)
    r_h, k_h, v_h, a_h, w_h = heads(r), heads(k), heads(v), heads(a), heads(decay)
    xs = tuple(jnp.moveaxis(t, 1, 0) for t in (r_h, w_h, k_h, v_h, -kk, kk * a_h))
    state0 = jnp.zeros((B, RWKV_HEADS, HEAD_DIM, HEAD_DIM), jnp.float32)
    _, y = lax.scan(_rwkv7_step, state0, xs)
    y = jnp.moveaxis(y, 0, 1)
    mu = jnp.mean(y, -1, keepdims=True)
    var = jnp.mean(jnp.square(y - mu), -1, keepdims=True)
    y = ((y - mu) * lax.rsqrt(var + GN_EPS)).reshape(B, S, RWKV_DIM) * ln_x_g + ln_x_b
    bonus = jnp.sum(r_h * k_h * r_k, -1, keepdims=True) * v_h
    return (y + bonus.reshape(B, S, RWKV_DIM)) * g


def _dsa_attention(q, k, v, q_idx, k_idx, w_idx, rel_bias, top_k):
    B, S = q.shape[:2]
    nb = S // Q_BLOCK
    blocks = lambda t: t.reshape(B * nb, Q_BLOCK, *t.shape[2:])
    b_ids = jnp.repeat(jnp.arange(B, dtype=jnp.int32), nb)
    t0s = jnp.tile(jnp.arange(nb, dtype=jnp.int32) * Q_BLOCK, B)
    s_pos = jnp.arange(S, dtype=jnp.int32)

    def one_block(args):
        b, t0, qb, qib, wib = args
        kb, vb, kib = k[b], v[b], k_idx[b]
        t = t0 + jnp.arange(Q_BLOCK, dtype=jnp.int32)
        sc = jax.nn.relu(jnp.einsum('qhd,sd->qhs', qib, kib).astype(jnp.float32) * IDX_DIM ** -0.5)
        index = jnp.einsum('qhs,qh->qs', sc, wib.astype(jnp.float32))
        index = jnp.where(s_pos[None, :] <= t[:, None], index, -jnp.inf)
        _, sel = lax.top_k(index, top_k)
        ks, vs = kb[sel], vb[sel]
        dist = t[:, None] - sel
        bias = rel_bias[_t5_bucket(dist)].astype(jnp.float32)
        logits = jnp.einsum('qhd,qkhd->qhk', qb, ks).astype(jnp.float32) * HEAD_DIM ** -0.5
        logits = logits + jnp.transpose(bias, (0, 2, 1))
        logits = jnp.where((dist >= 0)[:, None, :], logits, -jnp.inf)
        prob = jax.nn.softmax(logits, axis=-1)
        return jnp.einsum('qhk,qkhd->qhd', prob.astype(vs.dtype), vs)

    out = lax.map(one_block, (b_ids, t0s, blocks(q), blocks(q_idx), blocks(w_idx)))
    return out.reshape(B, S, ATTN_DIM)


def _hier_moe(h, w_router_grp, b_router_grp, w_router_exp, b_router_exp, w_gate, w_up, w_down):
    B, S, D = h.shape
    T = B * S
    hf = h.reshape(T, D)
    p_grp = jax.nn.softmax((hf @ w_router_grp + b_router_grp).astype(jnp.float32), axis=-1)
    p_g, g_sel = lax.top_k(p_grp, 1)
    e_logits = (hf @ w_router_exp + b_router_exp).astype(jnp.float32).reshape(T, N_GROUPS, EXPERTS_PER_GROUP)
    e_logits = jnp.take_along_axis(e_logits, g_sel[:, :, None], axis=1)[:, 0]
    p_e = jax.nn.softmax(e_logits, axis=-1)
    top_v, top_i = lax.top_k(p_e, EXPERT_TOP_K)
    gate = p_g * top_v / jnp.sum(top_v, -1, keepdims=True)
    eid = (g_sel * EXPERTS_PER_GROUP + top_i).reshape(-1)
    order = jnp.argsort(eid)
    tok = order // EXPERT_TOP_K
    xs = hf[tok]
    sizes = jnp.bincount(eid, length=N_EXPERTS).astype(jnp.int32)
    hg = lax.ragged_dot(xs, w_gate, sizes)
    hu = lax.ragged_dot(xs, w_up, sizes)
    ys = lax.ragged_dot(jax.nn.silu(hg) * hu, w_down, sizes)
    y = jnp.zeros_like(ys).at[order].set(ys).reshape(T, EXPERT_TOP_K, D)
    return jnp.einsum('tkd,tk->td', y, gate.astype(y.dtype)).reshape(B, S, D)


def setup_inputs(seed: int = 0) -> dict:
    key = jax.random.key(seed)
    ks = jax.random.split(key, 32)
    L = DEPTH
    nrm = lambda k, shape, scale: jax.random.normal(k, shape, jnp.float32) * scale
    v_rwkv0 = RWKV_DIM + DECAY_LORA + RWKV_DIM
    v_attn0 = RWKV_COLS + 2 * ATTN_DIM
    col_scale = (jnp.ones((N_IN,), jnp.float32)
                 .at[v_rwkv0:v_rwkv0 + RWKV_DIM].set(BETA)
                 .at[v_attn0:v_attn0 + ATTN_DIM].set(BETA))
    return {
        "x": nrm(ks[0], (BATCH, SEQ, D_MODEL), 1.0),
        "w_in": nrm(ks[1], (L, D_MODEL, N_IN), D_MODEL ** -0.5) * col_scale,
        "mu_shift": jax.random.uniform(ks[2], (L, RWKV_COLS), jnp.float32),
        "w0": jax.random.uniform(ks[3], (L, RWKV_DIM), jnp.float32, -6.0, -1.0),
        "w_lora_up": nrm(ks[4], (L, DECAY_LORA, RWKV_DIM), 0.5 * DECAY_LORA ** -0.5),
        "a0": nrm(ks[5], (L, RWKV_DIM), 0.1),
        "a_lora_up": nrm(ks[6], (L, AAA_LORA, RWKV_DIM), AAA_LORA ** -0.5),
        "g_lora_up": nrm(ks[7], (L, GATE_LORA, RWKV_DIM), GATE_LORA ** -0.5),
        "k_k": 0.85 + nrm(ks[8], (L, RWKV_DIM), 0.02),
        "k_a": 1.0 + nrm(ks[9], (L, RWKV_DIM), 0.02),
        "r_k": nrm(ks[10], (L, RWKV_HEADS, HEAD_DIM), 0.1),
        "ln_x_g": 1.0 + nrm(ks[11], (L, RWKV_DIM), 0.02),
        "ln_x_b": nrm(ks[12], (L, RWKV_DIM), 0.02),
        "w_branch_a": nrm(ks[13], (L, RWKV_DIM, D_MODEL), BETA * RWKV_DIM ** -0.5),
        "idx_k_ln_g": 1.0 + nrm(ks[14], (L, IDX_DIM), 0.02),
        "idx_k_ln_b": nrm(ks[15], (L, IDX_DIM), 0.02),
        "rel_bias": nrm(ks[16], (N_BUCKETS, ATTN_HEADS), 0.5),
        "w_branch_b": nrm(ks[17], (L, ATTN_DIM, D_MODEL), BETA * ATTN_DIM ** -0.5),
        "w_out": nrm(ks[18], (L, D_MODEL, D_MODEL), BETA * D_MODEL ** -0.5),
        "ln1_g": 1.0 + nrm(ks[19], (L, D_MODEL), 0.02),
        "ln1_b": nrm(ks[20], (L, D_MODEL), 0.02),
        "w_router_grp": nrm(ks[21], (L, D_MODEL, N_GROUPS), D_MODEL ** -0.5),
        "b_router_grp": nrm(ks[22], (L, N_GROUPS), 0.01),
        "w_router_exp": nrm(ks[23], (L, D_MODEL, N_EXPERTS), D_MODEL ** -0.5),
        "b_router_exp": nrm(ks[24], (L, N_EXPERTS), 0.01),
        "w_expert_gate": nrm(ks[25], (L, N_EXPERTS, D_MODEL, D_EXPERT), D_MODEL ** -0.5),
        "w_expert_up": nrm(ks[26], (L, N_EXPERTS, D_MODEL, D_EXPERT), D_MODEL ** -0.5),
        "w_expert_down": nrm(ks[27], (L, N_EXPERTS, D_EXPERT, D_MODEL), BETA * D_EXPERT ** -0.5),
        "ln2_g": 1.0 + nrm(ks[28], (L, D_MODEL), 0.02),
        "ln2_b": nrm(ks[29], (L, D_MODEL), 0.02),
    }


def reference(x, w_in, mu_shift, w0, w_lora_up, a0, a_lora_up, g_lora_up, k_k, k_a, r_k,
              ln_x_g, ln_x_b, w_branch_a, idx_k_ln_g, idx_k_ln_b, rel_bias, w_branch_b, w_out,
              ln1_g, ln1_b, w_router_grp, b_router_grp, w_router_exp, b_router_exp,
              w_expert_gate, w_expert_up, w_expert_down, ln2_g, ln2_b):
    B, S, _ = x.shape
    top_k = min(MAX_TOPK, S // 4)
    h = x
    for l in range(DEPTH):
        p = h @ w_in[l]
        p_rwkv, q, k, v, q_idx, k_idx, w_idx, g_a, g_b = _split(p, IN_SIZES)
        y_a = _rwkv7_mix(p_rwkv, mu_shift[l], w0[l], w_lora_up[l], a0[l], a_lora_up[l], g_lora_up[l],
                         k_k[l], k_a[l], r_k[l], ln_x_g[l], ln_x_b[l]) @ w_branch_a[l]
        hd = lambda t: t.reshape(B, S, ATTN_HEADS, HEAD_DIM)
        k_idx_n = _layer_norm(k_idx, idx_k_ln_g[l], idx_k_ln_b[l]).astype(k_idx.dtype)
        y_b = _dsa_attention(hd(q), hd(k), hd(v), q_idx.reshape(B, S, IDX_HEADS, IDX_DIM), k_idx_n,
                             w_idx * IDX_HEADS ** -0.5, rel_bias, top_k) @ w_branch_b[l]
        mix = (jax.nn.sigmoid(g_a) * y_a + jax.nn.sigmoid(g_b) * y_b) @ w_out[l]
        h = _layer_norm(ALPHA * h + mix, ln1_g[l], ln1_b[l]).astype(x.dtype)
        moe = _hier_moe(h, w_router_grp[l], b_router_grp[l], w_router_exp[l], b_router_exp[l],
                        w_expert_gate[l], w_expert_up[l], w_expert_down[l])
        h = _layer_norm(ALPHA * h + moe, ln2_g[l], ln2_b[l]).astype(x.dtype)
    return h
```

```python
import functools
import math

import jax
import jax.numpy as jnp
from jax import lax
from jax.experimental import pallas as pl
from jax.experimental.pallas import tpu as pltpu

F32 = jnp.float32
BF16 = jnp.bfloat16
I32 = jnp.int32

D_MODEL = 1024
HEAD_DIM = 64
N_HEADS = 8
HDIM = N_HEADS * HEAD_DIM
DECAY_LORA = 64
AAA_LORA = 64
GATE_LORA = 128
RWKV_COLS = 3 * HDIM + DECAY_LORA + AAA_LORA + GATE_LORA
IDX_DIM = 64
MAX_TOPK = 256
N_BUCKETS = 32
MAX_EXACT = 16
MAX_DISTANCE = 128
N_GROUPS = 4
EXPERTS_PER_GROUP = 8
N_EXPERTS = 32
D_EXPERT = 512
GN_EPS = 64e-5
LN_EPS = 1e-5
ALPHA = 2.0 ** 0.25
LANES = 128
CHUNK = 64
QB = 128
KCH = 512
INT_MIN = -(2 ** 31)
NEG = -1e30
VMEM_LIMIT = 56 * 1024 * 1024

NN = (((1,), (0,)), ((), ()))
NT = (((1,), (1,)), ((), ()))


def _dot(a, b, dims=NN):
    return lax.dot_general(a, b, dims, preferred_element_type=F32)


def _split2(x):
    hi = x.astype(BF16)
    lo = (x - hi.astype(F32)).astype(BF16)
    return hi, lo


def _split3(x):
    hi = x.astype(BF16)
    r1 = x - hi.astype(F32)
    mid = r1.astype(BF16)
    lo = (r1 - mid.astype(F32)).astype(BF16)
    return hi, mid, lo


def _mm1(a, b, dims=NN):
    return _dot(a.astype(BF16), b.astype(BF16), dims)


def _mm3(a, b, dims=NN):
    ah, al = _split2(a)
    bh, bl = _split2(b)
    return _dot(ah, bh, dims) + (_dot(ah, bl, dims) + _dot(al, bh, dims))


def _mm_exact_lhs(a_bf, b):
    b0, b1, b2 = _split3(b)
    return _dot(a_bf, b0) + (_dot(a_bf, b1) + _dot(a_bf, b2))


def _mm_exact_rhs(a, b_bf):
    a0, a1, a2 = _split3(a)
    return _dot(a0, b_bf) + (_dot(a1, b_bf) + _dot(a2, b_bf))


def _params(sem, vmem=VMEM_LIMIT):
    return pltpu.CompilerParams(dimension_semantics=sem, vmem_limit_bytes=vmem)


def _full(shape):
    nd = len(shape)
    return pl.BlockSpec(shape, lambda *_: (0,) * nd)


def _inproj_kernel(x_ref, wr_ref, wq_ref, wc_ref, lng_ref, lnb_ref,
                   pr_ref, q_ref, k_ref, v_ref, qi_ref, kz_ref, kw_ref):
    xb = x_ref[...].astype(BF16)
    pr_ref[...] = _dot(xb, wr_ref[...])
    qkv = _dot(xb, wq_ref[...])
    q_ref[...] = (qkv[:, 0:HDIM] * (HEAD_DIM ** -0.5)).astype(BF16)
    k_ref[...] = qkv[:, HDIM:2 * HDIM].astype(BF16)
    v_ref[...] = qkv[:, 2 * HDIM:3 * HDIM].astype(BF16)
    qi_ref[...] = qkv[:, 3 * HDIM:4 * HDIM].astype(BF16)
    c = _dot(xb, wc_ref[...])
    kw_ref[...] = c
    lane = lax.broadcasted_iota(I32, c.shape, 1)
    isk = lane < IDX_DIM
    mu = jnp.sum(jnp.where(isk, c, 0.0), axis=1, keepdims=True) * (1.0 / IDX_DIM)
    d = jnp.where(isk, c - mu, 0.0)
    var = jnp.sum(d * d, axis=1, keepdims=True) * (1.0 / IDX_DIM)
    kn = d * lax.rsqrt(var + LN_EPS) * lng_ref[...] + lnb_ref[...]
    kz_ref[:, 0:LANES] = kn.astype(BF16)
    kz_ref[:, LANES:2 * LANES] = pltpu.roll(kn, IDX_DIM, 1).astype(BF16)


def _inproj(x2, wr, wq, wc, lng, lnb, tm):
    T = x2.shape[0]
    row = lambda n: pl.BlockSpec((tm, n), lambda i: (i, 0))
    return pl.pallas_call(
        _inproj_kernel,
        grid=(T // tm,),
        in_specs=[row(D_MODEL), _full(wr.shape), _full(wq.shape), _full(wc.shape),
                  _full(lng.shape), _full(lnb.shape)],
        out_specs=[row(RWKV_COLS), row(HDIM), row(HDIM), row(HDIM), row(HDIM),
                   row(2 * LANES), row(LANES)],
        out_shape=[jax.ShapeDtypeStruct((T, RWKV_COLS), F32)]
        + [jax.ShapeDtypeStruct((T, HDIM), BF16)] * 4
        + [jax.ShapeDtypeStruct((T, 2 * LANES), BF16), jax.ShapeDtypeStruct((T, LANES), F32)],
        compiler_params=_params(("parallel",)),
        name="inproj",
    )(x2, wr, wq, wc, lng, lnb)


def _softplus(x):
    return jnp.maximum(x, 0.0) + jnp.log(1.0 + jnp.exp(-jnp.abs(x)))


def _sigmoid(x):
    return 1.0 / (1.0 + jnp.exp(-x))


def _prep_kernel(tiles_per_seq, p_ref, pp_ref, mu_ref, w0_ref, a0_ref, kk_ref, ka_ref, rk_ref,
                 wup_ref, aup_ref, gup_ref, bd_ref,
                 r_ref, lw_ref, k_ref, v_ref, a_ref, b_ref, g_ref, bon_ref):
    i = pl.program_id(0)
    p = p_ref[...]
    tm = p.shape[0]
    first = (i % tiles_per_seq) == 0
    prow = jnp.where(first, 0.0, pp_ref[7:8, :])
    rowid = lax.broadcasted_iota(I32, p.shape, 0)
    prev = jnp.where(rowid == 0, prow, pltpu.roll(p, 1, 0))
    ps = p + (prev - p) * mu_ref[...]
    r = ps[:, 0:HDIM]
    k = ps[:, HDIM:2 * HDIM]
    v = ps[:, 2 * HDIM:3 * HDIM]
    da = ps[:, 3 * HDIM:3 * HDIM + LANES]
    gd = ps[:, 3 * HDIM + LANES:3 * HDIM + 2 * LANES]
    w = -_softplus(-(w0_ref[...] + _mm3(jnp.tanh(da), wup_ref[...]))) - 0.5
    lw_ref[...] = -jnp.exp(w)
    a = _sigmoid(a0_ref[...] + _mm3(da, aup_ref[...]))
    g_ref[...] = _mm3(_sigmoid(gd), gup_ref[...])
    bd = bd_ref[...]
    kk = k * kk_ref[...]
    ss = _mm_exact_rhs(kk * kk, bd)
    kk = kk / jnp.maximum(jnp.sqrt(ss), 1e-12)
    k2 = k * (1.0 + (a - 1.0) * ka_ref[...])
    r_ref[...] = r
    k_ref[...] = k2
    v_ref[...] = v
    a_ref[...] = -kk
    b_ref[...] = kk * a
    bon_ref[...] = _mm_exact_rhs(r * k2 * rk_ref[...], bd) * v


def _rwkv_prep(pr, S, prm, tm):
    T = pr.shape[0]
    row = lambda n: pl.BlockSpec((tm, n), lambda i: (i, 0))
    prev = pl.BlockSpec((8, RWKV_COLS), lambda i: (jnp.maximum(i * (tm // 8) - 1, 0), 0))
    names = ["mu", "w0", "a0", "k_k", "k_a", "r_k", "wup", "aup", "gup", "bd"]
    return pl.pallas_call(
        functools.partial(_prep_kernel, S // tm),
        grid=(T // tm,),
        in_specs=[row(RWKV_COLS), prev] + [_full(prm[n].shape) for n in names],
        out_specs=[row(HDIM)] * 8,
        out_shape=[jax.ShapeDtypeStruct((T, HDIM), F32)] * 8,
        compiler_params=_params(("parallel",)),
        name="rwkv_prep",
    )(pr, pr, *[prm[n] for n in names])


def _scan_kernel(r_ref, lw_ref, k_ref, v_ref, a_ref, b_ref, g_ref, bon_ref, lng_ref, lnb_ref, bd_ref,
                 o_ref, st_ref, y_ref):
    C, N = CHUNK, HEAD_DIM

    @pl.when(pl.program_id(1) == 0)
    def _():
        st_ref[...] = jnp.zeros(st_ref.shape, F32)

    ri = lax.broadcasted_iota(I32, (C, C), 0)
    ci = lax.broadcasted_iota(I32, (C, C), 1)
    incl = ri >= ci
    strict = ri > ci
    eye = ri == ci
    lw = lw_ref[...]
    cum = _mm_exact_lhs(jnp.where(incl, 1.0, 0.0).astype(BF16), lw)
    last = cum[C - 1:C, :]
    e_c = jnp.exp(cum)
    e_i = jnp.exp(-cum)
    e_end = jnp.exp(last - cum)
    gam = jnp.exp(last)
    r_t = r_ref[...] * e_c
    a_t = a_ref[...] * jnp.exp(cum - lw)
    v_all = v_ref[...]
    bT = (b_ref[...] * e_i).T
    kT = (k_ref[...] * e_i).T
    bhT = (b_ref[...] * e_end).T
    khT = (k_ref[...] * e_end).T
    eye_f = jnp.where(eye, 1.0, 0.0)

    for h in range(N_HEADS):
        sl = slice(h * N, (h + 1) * N)
        ah, rh, vh = a_t[:, sl], r_t[:, sl], v_all[:, sl]
        ar = jnp.concatenate([ah, rh], axis=0)
        sb = _mm3(ar, bT[sl, :])
        sk = _mm3(ar, kT[sl, :])
        a_ab = jnp.where(strict, sb[:C], 0.0)
        a_rb = jnp.where(incl, sb[C:], 0.0)
        a_ak = jnp.where(strict, sk[:C], 0.0)
        a_rk = jnp.where(incl, sk[C:], 0.0)
        tinv = eye_f + a_ab
        xp = a_ab
        for _ in range(5):
            xp = _mm3(xp, xp)
            tinv = tinv + _mm3(tinv, xp)
        u = _mm3(a_ak, vh)
        pm = _mm3(tinv, ah)
        qm = _mm3(tinv, u)
        st = st_ref[h]
        mmat = jnp.where(eye, gam[:, sl], 0.0) + _mm3(bhT[sl, :], pm)
        gmat = _mm3(bhT[sl, :], qm) + _mm3(khT[sl, :], vh)
        r2 = rh + _mm3(a_rb, pm)
        y0 = _mm3(a_rb, qm) + _mm3(a_rk, vh)
        y_ref[:, sl] = _mm3(r2, st) + y0
        st_ref[h] = _mm3(mmat, st) + gmat

    y = y_ref[...]
    bd = bd_ref[...]
    mu = _mm_exact_rhs(y, bd) * (1.0 / N)
    d = y - mu
    var = _mm_exact_rhs(d * d, bd) * (1.0 / N)
    yn = d * lax.rsqrt(var + GN_EPS) * lng_ref[...] + lnb_ref[...]
    o_ref[...] = ((yn + bon_ref[...]) * g_ref[...]).astype(BF16)


def _rwkv_scan(arrs, lng, lnb, bd, B, S):
    nc = S // CHUNK
    row = pl.BlockSpec((CHUNK, HDIM), lambda b, c: (b * nc + c, 0))
    return pl.pallas_call(
        _scan_kernel,
        grid=(B, nc),
        in_specs=[row] * 8 + [_full(lng.shape), _full(lnb.shape), _full(bd.shape)],
        out_specs=row,
        out_shape=jax.ShapeDtypeStruct((B * S, HDIM), BF16),
        scratch_shapes=[pltpu.VMEM((N_HEADS, HEAD_DIM, HEAD_DIM), F32), pltpu.VMEM((CHUNK, HDIM), F32)],
        compiler_params=_params(("parallel", "arbitrary")),
        name="rwkv_scan",
    )(*arrs, lng, lnb, bd)


def _sort_key(x):
    bits = pltpu.bitcast(x, I32)
    return bits ^ ((bits >> 31) & 0x7FFFFFFF)


def _attn_kernel(top_k, rb_ref, q_ref, k_ref, v_ref, qi_ref, kz_ref, kw_ref, o_ref,
                 ikey, wb, btab, m_scr, l_scr, acc):
    i = pl.program_id(1)
    t0 = i * QB
    lane = lax.broadcasted_iota(I32, (QB, LANES), 1)
    rowi = lax.broadcasted_iota(I32, (QB, LANES), 0)

    @pl.when(i == 0)
    def _build_bias():
        for m in range(2):
            n = jnp.maximum(m * QB + rowi - lane, 0)
            nf = jnp.maximum(n, 1).astype(F32)
            large = MAX_EXACT + (jnp.log(nf / MAX_EXACT) / math.log(MAX_DISTANCE / MAX_EXACT)
                                 * (N_BUCKETS - MAX_EXACT)).astype(I32)
            bucket = jnp.where(n < MAX_EXACT, n, jnp.minimum(large, N_BUCKETS - 1))
            for h in range(N_HEADS):
                t = jnp.zeros((QB, LANES), F32)
                for bk in range(N_BUCKETS):
                    t = jnp.where(bucket == bk, rb_ref[bk, h], t)
                btab[h, m] = t

    wscale = (N_HEADS ** -0.5) * (IDX_DIM ** -0.5)
    kw = kw_ref[...]
    for h in range(N_HEADS):
        wb[h] = jnp.broadcast_to(kw[:, IDX_DIM + h:IDX_DIM + h + 1] * (N_HEADS ** -0.5), (QB, LANES)) \
            * (IDX_DIM ** -0.5)
    del wscale
    n_ch = i // (KCH // QB) + 1

    def score_chunk(c, carry):
        c0 = pl.multiple_of(c * KCH, KCH)
        tot = [jnp.zeros((QB, LANES), F32) for _ in range(KCH // LANES)]
        for p in range(N_HEADS // 2):
            qp = qi_ref[:, p * LANES:(p + 1) * LANES]
            for e in range(2):
                z = _dot(qp, kz_ref[pl.ds(c0, KCH), e * LANES:(e + 1) * LANES], NT)
                wbh = wb[2 * p + e]
                for s in range(KCH // LANES):
                    tot[s] = tot[s] + jnp.maximum(z[:, s * LANES:(s + 1) * LANES], 0.0) * wbh
        for s in range(KCH // LANES):
            spos = c0 + s * LANES + lane
            key = jnp.where(spos <= t0 + rowi, _sort_key(tot[s]), INT_MIN)
            ikey[:, pl.ds(pl.multiple_of(c0 + s * LANES, LANES), LANES)] = key
        return carry

    lax.fori_loop(0, n_ch, score_chunk, 0)

    def count_ge(cand):
        def body(c, cnt):
            c0 = pl.multiple_of(c * KCH, KCH)
            for s in range(KCH // LANES):
                kc = ikey[:, pl.ds(pl.multiple_of(c0 + s * LANES, LANES), LANES)]
                cnt = cnt + jnp.where(kc >= cand, 1, 0)
            return cnt
        cnt = lax.fori_loop(0, n_ch, body, jnp.zeros((QB, LANES), I32))
        return jnp.sum(cnt, axis=1, keepdims=True)

    def search(bit, carry):
        u, cu = carry
        cand = u | (jnp.int32(1) << (31 - bit))
        cnt = count_ge(cand ^ INT_MIN)
        ok = cnt >= top_k
        return jnp.where(ok, cand, u), jnp.where(ok, cnt, cu)

    u, cu = lax.fori_loop(0, 32, search, (jnp.zeros((QB, 1), I32), jnp.zeros((QB, 1), I32)))
    thr = jnp.maximum(u ^ INT_MIN, INT_MIN + 1)

    @pl.when(jnp.max(cu) > top_k)
    def _fix_ties():
        budget = (top_k - count_ge(thr + 1)).astype(F32)
        ut = jnp.where(rowi <= lane, 1.0, 0.0).astype(BF16)

        def body(j, before):
            sl = pl.ds(pl.multiple_of(j * LANES, LANES), LANES)
            kc = ikey[:, sl]
            eq = kc == thr
            eqf = jnp.where(eq, 1.0, 0.0)
            rank = before + _dot(eqf.astype(BF16), ut) - eqf
            ikey[:, sl] = jnp.where(eq & (rank >= budget), INT_MIN, kc)
            return before + jnp.sum(eqf, axis=1, keepdims=True)
        lax.fori_loop(0, i + 1, body, jnp.zeros((QB, 1), F32))

    m_scr[...] = jnp.full(m_scr.shape, NEG, F32)
    l_scr[...] = jnp.zeros(l_scr.shape, F32)
    acc[...] = jnp.zeros(acc.shape, F32)
    q = q_ref[...]
    lane_q = lax.broadcasted_iota(I32, (QB, HDIM), 1)
    q_par = [jnp.where((lane_q % LANES < HEAD_DIM) == (e == 0), q, jnp.zeros_like(q)) for e in range(2)]

    def key_tile(j, near):
        rows = pl.ds(pl.multiple_of(j * QB, QB), QB)
        sel = ikey[:, rows] >= thr
        for p in range(N_HEADS // 2):
            cols = slice(p * LANES, (p + 1) * LANES)
            kp = k_ref[rows, cols]
            vp = v_ref[rows, cols]
            for e in range(2):
                h = 2 * p + e
                s = _dot(q_par[e][:, cols], kp, NT)
                s = s + (rb_ref[N_BUCKETS - 1, h] if near is None else btab[h, near])
                s = jnp.where(sel, s, NEG)
                m_old = m_scr[h]
                m_new = jnp.maximum(m_old, jnp.max(s, axis=1, keepdims=True))
                pr = jnp.exp(s - m_new)
                alpha = jnp.exp(m_old - m_new)
                l_scr[h] = alpha * l_scr[h] + jnp.sum(pr, axis=1, keepdims=True)
                acc[h] = alpha * acc[h] + _dot(pr.astype(BF16), vp)
                m_scr[h] = m_new

    def far_body(j, carry):
        key_tile(j, None)
        return carry

    lax.fori_loop(0, jnp.maximum(i - 1, 0), far_body, 0)

    @pl.when(i >= 1)
    def _():
        key_tile(i - 1, 1)

    key_tile(i, 0)

    for p in range(N_HEADS // 2):
        oe = acc[2 * p] / l_scr[2 * p]
        oo = acc[2 * p + 1] / l_scr[2 * p + 1]
        o_ref[:, p * LANES:(p + 1) * LANES] = jnp.where(lane < HEAD_DIM, oe, oo).astype(BF16)


def _attention(q, k, v, qi, kz, kw, rel_bias, B, S):
    nq = S // QB
    top_k = min(MAX_TOPK, S // 4)
    qrow = lambda n: pl.BlockSpec((QB, n), lambda b, i: (b * nq + i, 0))
    seq = lambda n: pl.BlockSpec((S, n), lambda b, i: (b, 0))
    return pl.pallas_call(
        functools.partial(_attn_kernel, top_k),
        grid=(B, nq),
        in_specs=[pl.BlockSpec(memory_space=pltpu.SMEM), qrow(HDIM), seq(HDIM), seq(HDIM), qrow(HDIM),
                  seq(2 * LANES), qrow(LANES)],
        out_specs=qrow(HDIM),
        out_shape=jax.ShapeDtypeStruct((B * S, HDIM), BF16),
        scratch_shapes=[pltpu.VMEM((QB, S), I32),
                        pltpu.VMEM((N_HEADS, QB, LANES), F32),
                        pltpu.VMEM((N_HEADS, 2, QB, LANES), F32),
                        pltpu.VMEM((N_HEADS, QB, 1), F32),
                        pltpu.VMEM((N_HEADS, QB, 1), F32),
                        pltpu.VMEM((N_HEADS, QB, LANES), F32)],
        compiler_params=_params(("parallel", "arbitrary")),
        name="dsa_attention",
    )(rel_bias, q, k, v, qi, kz, kw)


def _layer_norm(x, g, b):
    mu = jnp.mean(x, axis=1, keepdims=True)
    d = x - mu
    var = jnp.mean(d * d, axis=1, keepdims=True)
    return d * lax.rsqrt(var + LN_EPS) * g + b


def _merge_kernel(x_ref, ya_ref, at_ref, wg_ref, wa_ref, wb_ref, wo_ref, g1_ref, b1_ref, wr_ref, br_ref,
                  h_ref, ri_ref):
    x = x_ref[...]
    gates = _sigmoid(_dot(x.astype(BF16), wg_ref[...]))
    ya = _dot(ya_ref[...], wa_ref[...])
    yb = _dot(at_ref[...], wb_ref[...])
    mixin = gates[:, :D_MODEL] * ya + gates[:, D_MODEL:] * yb
    mix = _dot(mixin.astype(BF16), wo_ref[...])
    h = _layer_norm(ALPHA * x + mix, g1_ref[...], b1_ref[...])
    h_ref[...] = h
    lg = _mm3(h, wr_ref[...]) + br_ref[...]
    lane = lax.broadcasted_iota(I32, lg.shape, 1)
    gl = jnp.where(lane < N_GROUPS, lg, NEG)
    gmax = jnp.max(gl, axis=1, keepdims=True)
    p_g = 1.0 / jnp.sum(jnp.exp(gl - gmax), axis=1, keepdims=True)
    gsel = jnp.min(jnp.where(gl == gmax, lane, LANES), axis=1, keepdims=True)
    lo = N_GROUPS + EXPERTS_PER_GROUP * gsel
    el = jnp.where((lane >= lo) & (lane < lo + EXPERTS_PER_GROUP), lg, NEG)
    e1 = jnp.max(el, axis=1, keepdims=True)
    i1 = jnp.min(jnp.where(el == e1, lane, LANES), axis=1, keepdims=True)
    el2 = jnp.where(lane == i1, NEG, el)
    e2 = jnp.max(el2, axis=1, keepdims=True)
    i2 = jnp.min(jnp.where(el2 == e2, lane, LANES), axis=1, keepdims=True)
    w2 = jnp.exp(e2 - e1)
    gate1 = p_g / (1.0 + w2)
    gate2 = p_g * w2 / (1.0 + w2)
    ri = jnp.where(lane == 0, (i1 - N_GROUPS).astype(F32),
                   jnp.where(lane == 1, (i2 - N_GROUPS).astype(F32),
                             jnp.where(lane == 2, gate1, jnp.where(lane == 3, gate2, 0.0))))
    ri_ref[...] = ri


def _merge(x2, ya, at, wg, wa, wb, wo, g1, b1, wr, br, tm):
    T = x2.shape[0]
    row = lambda n: pl.BlockSpec((tm, n), lambda i: (i, 0))
    ws = [wg, wa, wb, wo, g1, b1, wr, br]
    return pl.pallas_call(
        _merge_kernel,
        grid=(T // tm,),
        in_specs=[row(D_MODEL), row(HDIM), row(HDIM)] + [_full(w.shape) for w in ws],
        out_specs=[row(D_MODEL), row(LANES)],
        out_shape=[jax.ShapeDtypeStruct((T, D_MODEL), F32), jax.ShapeDtypeStruct((T, LANES), F32)],
        compiler_params=_params(("parallel",)),
        name="merge_router",
    )(x2, ya, at, *ws)


def _row_gather(idx_ref, n, src_hbm, dst, sem):
    def issue(r, carry):
        pltpu.make_async_copy(src_hbm.at[pl.ds(idx_ref[0, 0, r], 1)], dst.at[pl.ds(r, 1)], sem).start()
        return carry
    lax.fori_loop(0, n, issue, 0)

    def drain(r, carry):
        pltpu.make_async_copy(src_hbm.at[pl.ds(0, 1)], dst.at[pl.ds(r, 1)], sem).wait()
        return carry
    lax.fori_loop(0, n, drain, 0)


def _moe_kernel(te_ref, nu_ref, src_ref, h_hbm, wg_ref, wu_ref, wd_ref, o_ref, xs, sem):
    t = pl.program_id(0)
    tm = xs.shape[0]

    @pl.when(t < nu_ref[0])
    def _():
        _row_gather(src_ref, tm, h_hbm, xs, sem)
        xb = xs[...].astype(BF16)
        hg = _dot(xb, wg_ref[0])
        hu = _dot(xb, wu_ref[0])
        act = (hg * _sigmoid(hg)) * hu
        o_ref[...] = _dot(act.astype(BF16), wd_ref[0])

    @pl.when(t >= nu_ref[0])
    def _():
        o_ref[...] = jnp.zeros(o_ref.shape, F32)


def _moe(tile_e, n_used, src3, h, wg, wu, wd, tm):
    n_tiles = src3.shape[0]
    grid_spec = pltpu.PrefetchScalarGridSpec(
        num_scalar_prefetch=2,
        grid=(n_tiles,),
        in_specs=[pl.BlockSpec((1, 1, tm), lambda t, te, nu: (t, 0, 0), memory_space=pltpu.SMEM),
                  pl.BlockSpec(memory_space=pl.ANY),
                  pl.BlockSpec((1, D_MODEL, D_EXPERT), lambda t, te, nu: (te[t], 0, 0)),
                  pl.BlockSpec((1, D_MODEL, D_EXPERT), lambda t, te, nu: (te[t], 0, 0)),
                  pl.BlockSpec((1, D_EXPERT, D_MODEL), lambda t, te, nu: (te[t], 0, 0))],
        out_specs=pl.BlockSpec((tm, D_MODEL), lambda t, te, nu: (t, 0)),
        scratch_shapes=[pltpu.VMEM((tm, D_MODEL), F32), pltpu.SemaphoreType.DMA(())],
    )
    return pl.pallas_call(
        _moe_kernel,
        grid_spec=grid_spec,
        out_shape=jax.ShapeDtypeStruct((n_tiles * tm, D_MODEL), F32),
        compiler_params=_params(("arbitrary",)),
        name="moe_experts",
    )(tile_e, n_used, src3, h, wg, wu, wd)


def _final_kernel(p0_ref, p1_ref, h_ref, ri_ref, ys_hbm, g2_ref, b2_ref, o_ref, y0, y1, sem0, sem1):
    tm = y0.shape[0]
    _row_gather(p0_ref, tm, ys_hbm, y0, sem0)
    _row_gather(p1_ref, tm, ys_hbm, y1, sem1)
    ri = ri_ref[...]
    moe = y0[...] * ri[:, 2:3] + y1[...] * ri[:, 3:4]
    o_ref[...] = _layer_norm(ALPHA * h_ref[...] + moe, g2_ref[...], b2_ref[...])


def _final(pos0, pos1, h, ri, ys, g2, b2, tm):
    T = h.shape[0]
    idx = pl.BlockSpec((1, 1, tm), lambda i: (i, 0, 0), memory_space=pltpu.SMEM)
    row = lambda n: pl.BlockSpec((tm, n), lambda i: (i, 0))
    return pl.pallas_call(
        _final_kernel,
        grid=(T // tm,),
        in_specs=[idx, idx, row(D_MODEL), row(LANES), pl.BlockSpec(memory_space=pl.ANY),
                  _full(g2.shape), _full(b2.shape)],
        out_specs=row(D_MODEL),
        out_shape=jax.ShapeDtypeStruct((T, D_MODEL), F32),
        scratch_shapes=[pltpu.VMEM((tm, D_MODEL), F32), pltpu.VMEM((tm, D_MODEL), F32),
                        pltpu.SemaphoreType.DMA(()), pltpu.SemaphoreType.DMA(())],
        compiler_params=_params(("arbitrary",)),
        name="combine_ln",
    )(pos0, pos1, h, ri, ys, g2, b2)


def _routing_tables(eid, tm):
    T = eid.shape[0]
    flat = eid.reshape(-1)
    n_tiles = (2 * T) // tm + N_EXPERTS
    order = jnp.argsort(flat, stable=True).astype(I32)
    sizes = jnp.zeros((N_EXPERTS,), I32).at[flat].add(1)
    padded = ((sizes + tm - 1) // tm) * tm
    pad_end = jnp.cumsum(padded)
    pad_off = pad_end - padded
    off = jnp.cumsum(sizes) - sizes
    e_sorted = flat[order]
    dest = pad_off[e_sorted] + (jnp.arange(2 * T, dtype=I32) - off[e_sorted])
    src_tok = jnp.zeros((n_tiles * tm,), I32).at[dest].set(order // 2)
    pos = jnp.zeros((2 * T,), I32).at[order].set(dest).reshape(T, 2)
    tile_e = jnp.minimum(jnp.searchsorted(pad_end, jnp.arange(n_tiles, dtype=I32) * tm, side="right"),
                         N_EXPERTS - 1).astype(I32)
    n_used = (pad_end[-1] // tm).astype(I32).reshape(1)
    return tile_e, n_used, src_tok.reshape(n_tiles, 1, tm), pos


def _block_diag_ones():
    hid = jnp.arange(HDIM, dtype=I32) // HEAD_DIM
    return (hid[:, None] == hid[None, :]).astype(BF16)


def _layer(x, w_in, mu_shift, w0, w_lora_up, a0, a_lora_up, g_lora_up, k_k, k_a, r_k, ln_x_g, ln_x_b,
           w_branch_a, idx_k_ln_g, idx_k_ln_b, rel_bias, w_branch_b, w_out, ln1_g, ln1_b,
           w_router_grp, b_router_grp, w_router_exp, b_router_exp, w_gate, w_up, w_down, ln2_g, ln2_b):
    B, S, _ = x.shape
    T = B * S
    x2 = x.reshape(T, D_MODEL)
    row = lambda t: t.reshape(1, -1)

    c_r, c_wd, c_k, c_v, c_ad, c_gd = 0, 512, 576, 1088, 1600, 1664
    perm = jnp.concatenate([jnp.arange(c_r, c_r + 512), jnp.arange(c_k, c_k + 512), jnp.arange(c_v, c_v + 512),
                            jnp.arange(c_wd, c_wd + 64), jnp.arange(c_ad, c_ad + 64),
                            jnp.arange(c_gd, c_gd + 128)])
    o_q = RWKV_COLS
    o_c = o_q + 4 * HDIM
    o_g = o_c + IDX_DIM + N_HEADS
    wr = w_in[:, :RWKV_COLS][:, perm].astype(BF16)
    wq = w_in[:, o_q:o_c].astype(BF16)
    wc = jnp.pad(w_in[:, o_c:o_g], ((0, 0), (0, LANES - IDX_DIM - N_HEADS))).astype(BF16)
    wgates = w_in[:, o_g:].astype(BF16)
    pad_idx = lambda t: jnp.pad(t, (0, LANES - IDX_DIM)).reshape(1, LANES)

    pr, q, k, v, qi, kz, kw = _inproj(x2, wr, wq, wc, pad_idx(idx_k_ln_g), pad_idx(idx_k_ln_b), tm=512)

    bd = _block_diag_ones()
    prm = {
        "mu": row(mu_shift[perm]), "w0": row(w0), "a0": row(a0), "k_k": row(k_k), "k_a": row(k_a),
        "r_k": row(r_k),
        "wup": jnp.pad(w_lora_up, ((0, AAA_LORA), (0, 0))),
        "aup": jnp.pad(a_lora_up, ((DECAY_LORA, 0), (0, 0))),
        "gup": g_lora_up, "bd": bd,
    }
    arrs = _rwkv_prep(pr, S, prm, tm=256)
    ya = _rwkv_scan(arrs, row(ln_x_g), row(ln_x_b), bd, B, S)

    at = _attention(q, k, v, qi, kz, kw, rel_bias, B, S)

    w_router = jnp.pad(jnp.concatenate([w_router_grp, w_router_exp], axis=1),
                       ((0, 0), (0, LANES - N_GROUPS - N_EXPERTS)))
    b_router = jnp.pad(jnp.concatenate([b_router_grp, b_router_exp]), (0, LANES - N_GROUPS - N_EXPERTS))
    h1, ri = _merge(x2, ya, at, wgates, w_branch_a.astype(BF16), w_branch_b.astype(BF16), w_out.astype(BF16),
                    row(ln1_g), row(ln1_b), w_router, row(b_router), tm=256)

    tm_e = 512
    eid = ri[:, 0:2].astype(I32)
    tile_e, n_used, src3, pos = _routing_tables(eid, tm_e)
    ys = _moe(tile_e, n_used, src3, h1, w_gate.astype(BF16), w_up.astype(BF16), w_down.astype(BF16), tm_e)

    tm_f = 256
    pos0 = pos[:, 0].reshape(T // tm_f, 1, tm_f)
    pos1 = pos[:, 1].reshape(T // tm_f, 1, tm_f)
    out = _final(pos0, pos1, h1, ri, ys, row(ln2_g), row(ln2_b), tm_f)
    return out.reshape(B, S, D_MODEL)


def kernel(x, w_in, mu_shift, w0, w_lora_up, a0, a_lora_up, g_lora_up, k_k, k_a, r_k, ln_x_g, ln_x_b, w_branch_a, idx_k_ln_g, idx_k_ln_b, rel_bias, w_branch_b, w_out, ln1_g, ln1_b, w_router_grp, b_router_grp, w_router_exp, b_router_exp, w_expert_gate, w_expert_up, w_expert_down, ln2_g, ln2_b):
    assert w_in.shape[0] == 1, "single-layer (DEPTH = 1) block"
    l = 0
    return _layer(x, w_in[l], mu_shift[l], w0[l], w_lora_up[l], a0[l], a_lora_up[l], g_lora_up[l], k_k[l],
                  k_a[l], r_k[l], ln_x_g[l], ln_x_b[l], w_branch_a[l], idx_k_ln_g[l], idx_k_ln_b[l], rel_bias,
                  w_branch_b[l], w_out[l], ln1_g[l], ln1_b[l], w_router_grp[l], b_router_grp[l],
                  w_router_exp[l], b_router_exp[l], w_expert_gate[l], w_expert_up[l], w_expert_down[l],
                  ln2_g[l], ln2_b[l])
```

```python
import functools
import math

import jax
import jax.numpy as jnp
from jax import lax
from jax.experimental import pallas as pl
from jax.experimental.pallas import tpu as pltpu

F32 = jnp.float32
BF16 = jnp.bfloat16
I32 = jnp.int32

D_MODEL = 1024
HEAD_DIM = 64
N_HEADS = 8
HDIM = N_HEADS * HEAD_DIM
DECAY_LORA = 64
AAA_LORA = 64
GATE_LORA = 128
RWKV_COLS = 3 * HDIM + DECAY_LORA + AAA_LORA + GATE_LORA
IDX_DIM = 64
MAX_TOPK = 256
N_BUCKETS = 32
MAX_EXACT = 16
MAX_DISTANCE = 128
N_GROUPS = 4
EXPERTS_PER_GROUP = 8
N_EXPERTS = 32
D_EXPERT = 512
GN_EPS = 64e-5
LN_EPS = 1e-5
ALPHA = 2.0 ** 0.25
LANES = 128
CHUNK = 64
SCAN_SUB = 2
QB = 128
KCH = 512
INT_MIN = -(2 ** 31)
NEG = -1e30
LOG2E = 1.4426950408889634
VMEM_LIMIT = 56 * 1024 * 1024

NN = (((1,), (0,)), ((), ()))
NT = (((1,), (1,)), ((), ()))


def _dot(a, b, dims=NN):
    return lax.dot_general(a, b, dims, preferred_element_type=F32)


def _split2(x):
    hi = x.astype(BF16)
    lo = (x - hi.astype(F32)).astype(BF16)
    return hi, lo


def _split3(x):
    hi = x.astype(BF16)
    r1 = x - hi.astype(F32)
    mid = r1.astype(BF16)
    lo = (r1 - mid.astype(F32)).astype(BF16)
    return hi, mid, lo


def _mm1(a, b, dims=NN):
    return _dot(a.astype(BF16), b.astype(BF16), dims)


def _mm3(a, b, dims=NN):
    ah, al = _split2(a)
    bh, bl = _split2(b)
    return _dot(ah, bh, dims) + (_dot(ah, bl, dims) + _dot(al, bh, dims))


_mm_misc = _mm1
_mm_inv = _mm1
_mm_state = _mm3


def _mm_exact_lhs(a_bf, b):
    b0, b1, b2 = _split3(b)
    return _dot(a_bf, b0) + (_dot(a_bf, b1) + _dot(a_bf, b2))


def _mm_exact_rhs(a, b_bf):
    a0, a1, a2 = _split3(a)
    return _dot(a0, b_bf) + (_dot(a1, b_bf) + _dot(a2, b_bf))


def _params(sem, vmem=VMEM_LIMIT):
    return pltpu.CompilerParams(dimension_semantics=sem, vmem_limit_bytes=vmem)


def _full(shape):
    nd = len(shape)
    return pl.BlockSpec(shape, lambda *_: (0,) * nd)


def _inproj_kernel(x_ref, wr_ref, wq_ref, wc_ref, lng_ref, lnb_ref,
                   pr_ref, q_ref, k_ref, ve_ref, vo_ref, qi_ref, kz_ref, kw_ref):
    xb = x_ref[...].astype(BF16)
    pr_ref[...] = _dot(xb, wr_ref[...])
    qkv = _dot(xb, wq_ref[...])
    q_ref[...] = (qkv[:, 0:HDIM] * (HEAD_DIM ** -0.5 * LOG2E)).astype(BF16)
    k_ref[...] = qkv[:, HDIM:2 * HDIM].astype(BF16)
    v = qkv[:, 2 * HDIM:3 * HDIM]
    even = (lax.broadcasted_iota(I32, v.shape, 1) % LANES) < HEAD_DIM
    ve_ref[...] = jnp.where(even, v, 1.0).astype(BF16)
    vo_ref[...] = jnp.where(even, 1.0, v).astype(BF16)
    qi_ref[...] = qkv[:, 3 * HDIM:4 * HDIM].astype(BF16)
    c = _dot(xb, wc_ref[...])
    kw_ref[...] = c
    lane = lax.broadcasted_iota(I32, c.shape, 1)
    isk = lane < IDX_DIM
    mu = jnp.sum(jnp.where(isk, c, 0.0), axis=1, keepdims=True) * (1.0 / IDX_DIM)
    d = jnp.where(isk, c - mu, 0.0)
    var = jnp.sum(d * d, axis=1, keepdims=True) * (1.0 / IDX_DIM)
    kn = d * lax.rsqrt(var + LN_EPS) * lng_ref[...] + lnb_ref[...]
    kz_ref[:, 0:LANES] = kn.astype(BF16)
    kz_ref[:, LANES:2 * LANES] = pltpu.roll(kn, IDX_DIM, 1).astype(BF16)


def _inproj(x2, wr, wq, wc, lng, lnb, tm):
    T = x2.shape[0]
    row = lambda n: pl.BlockSpec((tm, n), lambda i: (i, 0))
    return pl.pallas_call(
        _inproj_kernel,
        grid=(T // tm,),
        in_specs=[row(D_MODEL), _full(wr.shape), _full(wq.shape), _full(wc.shape),
                  _full(lng.shape), _full(lnb.shape)],
        out_specs=[row(RWKV_COLS), row(HDIM), row(HDIM), row(HDIM), row(HDIM), row(HDIM),
                   row(2 * LANES), row(LANES)],
        out_shape=[jax.ShapeDtypeStruct((T, RWKV_COLS), F32)]
        + [jax.ShapeDtypeStruct((T, HDIM), BF16)] * 5
        + [jax.ShapeDtypeStruct((T, 2 * LANES), BF16), jax.ShapeDtypeStruct((T, LANES), F32)],
        compiler_params=_params(("parallel",)),
        name="inproj",
    )(x2, wr, wq, wc, lng, lnb)


def _softplus(x):
    return jnp.maximum(x, 0.0) + jnp.log(1.0 + jnp.exp(-jnp.abs(x)))


def _sigmoid(x):
    return 1.0 / (1.0 + jnp.exp(-x))


def _prep_kernel(tiles_per_seq, p_ref, pp_ref, mu_ref, w0_ref, a0_ref, kk_ref, ka_ref, rk_ref,
                 wup_ref, aup_ref, gup_ref, bd_ref,
                 r_ref, lw_ref, k_ref, v_ref, a_ref, b_ref, g_ref, bon_ref):
    i = pl.program_id(0)
    p = p_ref[...]
    tm = p.shape[0]
    first = (i % tiles_per_seq) == 0
    prow = jnp.where(first, 0.0, pp_ref[7:8, :])
    rowid = lax.broadcasted_iota(I32, p.shape, 0)
    prev = jnp.where(rowid == 0, prow, pltpu.roll(p, 1, 0))
    ps = p + (prev - p) * mu_ref[...]
    r = ps[:, 0:HDIM]
    k = ps[:, HDIM:2 * HDIM]
    v = ps[:, 2 * HDIM:3 * HDIM]
    da = ps[:, 3 * HDIM:3 * HDIM + LANES]
    gd = ps[:, 3 * HDIM + LANES:3 * HDIM + 2 * LANES]
    w = -_softplus(-(w0_ref[...] + _mm3(jnp.tanh(da), wup_ref[...]))) - 0.5
    lw_ref[...] = -jnp.exp(w)
    a = _sigmoid(a0_ref[...] + _mm3(da, aup_ref[...]))
    g_ref[...] = _mm3(_sigmoid(gd), gup_ref[...])
    bd = bd_ref[...]
    kk = k * kk_ref[...]
    ss = _mm_exact_rhs(kk * kk, bd)
    kk = kk / jnp.maximum(jnp.sqrt(ss), 1e-12)
    k2 = k * (1.0 + (a - 1.0) * ka_ref[...])
    r_ref[...] = r
    k_ref[...] = k2
    v_ref[...] = v
    a_ref[...] = -kk
    b_ref[...] = kk * a
    bon_ref[...] = _mm_exact_rhs(r * k2 * rk_ref[...], bd) * v


def _rwkv_prep(pr, S, prm, tm):
    T = pr.shape[0]
    row = lambda n: pl.BlockSpec((tm, n), lambda i: (i, 0))
    prev = pl.BlockSpec((8, RWKV_COLS), lambda i: (jnp.maximum(i * (tm // 8) - 1, 0), 0))
    names = ["mu", "w0", "a0", "k_k", "k_a", "r_k", "wup", "aup", "gup", "bd"]
    return pl.pallas_call(
        functools.partial(_prep_kernel, S // tm),
        grid=(T // tm,),
        in_specs=[row(RWKV_COLS), prev] + [_full(prm[n].shape) for n in names],
        out_specs=[row(HDIM)] * 8,
        out_shape=[jax.ShapeDtypeStruct((T, HDIM), F32)] * 8,
        compiler_params=_params(("parallel",)),
        name="rwkv_prep",
    )(pr, pr, *[prm[n] for n in names])


def _scan_kernel(r_ref, lw_ref, k_ref, v_ref, a_ref, b_ref, g_ref, bon_ref, lng_ref, lnb_ref, bd_ref,
                 o_ref, st_ref, y_ref):
    C, N, H = CHUNK, HEAD_DIM, N_HEADS

    @pl.when(pl.program_id(1) == 0)
    def _():
        st_ref[...] = jnp.zeros(st_ref.shape, F32)

    ri = lax.broadcasted_iota(I32, (C, C), 0)
    ci = lax.broadcasted_iota(I32, (C, C), 1)
    incl = ri >= ci
    strict = ri > ci
    eye = ri == ci
    eye_f = jnp.where(eye, 1.0, 0.0)
    lmat = jnp.where(incl, 1.0, 0.0).astype(BF16)
    sls = [slice(h * N, (h + 1) * N) for h in range(H)]
    units = [(s, h) for s in range(SCAN_SUB) for h in range(H)]
    ah, rh, vh, bT, kT, bhT, khT, gam = {}, {}, {}, {}, {}, {}, {}, {}
    for s in range(SCAN_SUB):
        rows = slice(s * C, (s + 1) * C)
        lw = lw_ref[rows, :]
        cum = _mm_exact_lhs(lmat, lw)
        last = cum[C - 1:C, :]
        e_i = jnp.exp(-cum)
        e_end = jnp.exp(last - cum)
        g_s = jnp.exp(last)
        r_t = r_ref[rows, :] * jnp.exp(cum)
        a_t = a_ref[rows, :] * jnp.exp(cum - lw)
        v_s = v_ref[rows, :]
        b_s, k_s = b_ref[rows, :], k_ref[rows, :]
        bT_s, kT_s = (b_s * e_i).T, (k_s * e_i).T
        bhT_s, khT_s = (b_s * e_end).T, (k_s * e_end).T
        for h in range(H):
            un = (s, h)
            ah[un], rh[un], vh[un] = a_t[:, sls[h]], r_t[:, sls[h]], v_s[:, sls[h]]
            bT[un], kT[un] = bT_s[sls[h], :], kT_s[sls[h], :]
            bhT[un], khT[un] = bhT_s[sls[h], :], khT_s[sls[h], :]
            gam[un] = g_s[:, sls[h]]

    ar = {un: jnp.concatenate([ah[un], rh[un]], axis=0) for un in units}
    sb = {un: _mm_misc(ar[un], bT[un]) for un in units}
    sk = {un: _mm_misc(ar[un], kT[un]) for un in units}
    a_ab = {un: jnp.where(strict, sb[un][:C], 0.0) for un in units}
    a_rb = {un: jnp.where(incl, sb[un][C:], 0.0) for un in units}
    a_ak = {un: jnp.where(strict, sk[un][:C], 0.0) for un in units}
    a_rk = {un: jnp.where(incl, sk[un][C:], 0.0) for un in units}
    u = {un: _mm_misc(a_ak[un], vh[un]) for un in units}
    tinv = {un: eye_f + a_ab[un] for un in units}
    xp = a_ab
    for _ in range(5):
        xp = {un: _mm_inv(xp[un], xp[un]) for un in units}
        tinv = {un: tinv[un] + _mm_inv(tinv[un], xp[un]) for un in units}
    pm = {un: _mm_misc(tinv[un], ah[un]) for un in units}
    qm = {un: _mm_misc(tinv[un], u[un]) for un in units}
    r2 = {un: rh[un] + _mm_misc(a_rb[un], pm[un]) for un in units}
    mmat = {un: jnp.where(eye, gam[un], 0.0) + _mm_misc(bhT[un], pm[un]) for un in units}
    y0 = {un: _mm_misc(a_rb[un], qm[un]) + _mm_misc(a_rk[un], vh[un]) for un in units}
    gmat = {un: _mm_misc(bhT[un], qm[un]) + _mm_misc(khT[un], vh[un]) for un in units}
    st = [st_ref[h] for h in range(H)]
    for s in range(SCAN_SUB):
        for h in range(H):
            y_ref[s * C:(s + 1) * C, sls[h]] = _mm_state(r2[(s, h)], st[h]) + y0[(s, h)]
        st = [_mm_state(mmat[(s, h)], st[h]) + gmat[(s, h)] for h in range(H)]
    for h in range(H):
        st_ref[h] = st[h]

    y = y_ref[...]
    bd = bd_ref[...]
    mu = _mm_exact_rhs(y, bd) * (1.0 / N)
    d = y - mu
    var = _mm_exact_rhs(d * d, bd) * (1.0 / N)
    yn = d * lax.rsqrt(var + GN_EPS) * lng_ref[...] + lnb_ref[...]
    o_ref[...] = ((yn + bon_ref[...]) * g_ref[...]).astype(BF16)


def _rwkv_scan(arrs, lng, lnb, bd, B, S):
    rows = SCAN_SUB * CHUNK
    nc = S // rows
    row = pl.BlockSpec((rows, HDIM), lambda b, c: (b * nc + c, 0))
    return pl.pallas_call(
        _scan_kernel,
        grid=(B, nc),
        in_specs=[row] * 8 + [_full(lng.shape), _full(lnb.shape), _full(bd.shape)],
        out_specs=row,
        out_shape=jax.ShapeDtypeStruct((B * S, HDIM), BF16),
        scratch_shapes=[pltpu.VMEM((N_HEADS, HEAD_DIM, HEAD_DIM), F32), pltpu.VMEM((rows, HDIM), F32)],
        compiler_params=_params(("parallel", "arbitrary")),
        name="rwkv_scan",
    )(*arrs, lng, lnb, bd)


def _sort_key(x):
    bits = pltpu.bitcast(x, I32)
    return bits ^ ((bits >> 31) & 0x7FFFFFFF)


def _attn_kernel(top_k, rb_ref, q_ref, k_ref, ve_ref, vo_ref, qi_ref, kz_ref, kw_ref, o_ref,
                 ikey, mbias, wb, btab, qm, s_scr, p_scr, *state):
    m_scr, acc, a_scr = (state[n * N_HEADS:(n + 1) * N_HEADS] for n in range(3))
    i = pl.program_id(1)
    t0 = i * QB
    lane = lax.broadcasted_iota(I32, (QB, LANES), 1)
    rowi = lax.broadcasted_iota(I32, (QB, LANES), 0)

    @pl.when(i == 0)
    def _build_bias():
        for h in range(N_HEADS):
            btab[h, 2] = jnp.zeros((QB, LANES), F32)
        for m in range(2):
            n = jnp.maximum(m * QB + rowi - lane, 0)
            nf = jnp.maximum(n, 1).astype(F32)
            large = MAX_EXACT + (jnp.log(nf / MAX_EXACT) / math.log(MAX_DISTANCE / MAX_EXACT)
                                 * (N_BUCKETS - MAX_EXACT)).astype(I32)
            bucket = jnp.where(n < MAX_EXACT, n, jnp.minimum(large, N_BUCKETS - 1))
            for h in range(N_HEADS):
                t = jnp.zeros((QB, LANES), F32)
                for bk in range(N_BUCKETS):
                    t = jnp.where(bucket == bk, rb_ref[bk, h], t)
                btab[h, m] = (t - rb_ref[N_BUCKETS - 1, h]) * LOG2E

    kw = kw_ref[...]
    for h in range(N_HEADS):
        wb[h] = jnp.broadcast_to(kw[:, IDX_DIM + h:IDX_DIM + h + 1] * (N_HEADS ** -0.5), (QB, LANES)) \
            * (IDX_DIM ** -0.5)
    n_ch = i // (KCH // QB) + 1

    def score_chunk(c, carry):
        c0 = pl.multiple_of(c * KCH, KCH)
        tot = [jnp.zeros((QB, LANES), F32) for _ in range(KCH // LANES)]
        for p in range(N_HEADS // 2):
            qp = qi_ref[:, p * LANES:(p + 1) * LANES]
            for e in range(2):
                z = _dot(qp, kz_ref[pl.ds(c0, KCH), e * LANES:(e + 1) * LANES], NT)
                wbh = wb[2 * p + e]
                for s in range(KCH // LANES):
                    tot[s] = tot[s] + jnp.maximum(z[:, s * LANES:(s + 1) * LANES], 0.0) * wbh
        for s in range(KCH // LANES):
            spos = c0 + s * LANES + lane
            key = jnp.where(spos <= t0 + rowi, _sort_key(tot[s]), INT_MIN)
            ikey[:, pl.ds(pl.multiple_of(c0 + s * LANES, LANES), LANES)] = key
        return carry

    lax.fori_loop(0, n_ch, score_chunk, 0)

    def count_ge(cand):
        def body(c, cnt):
            c0 = pl.multiple_of(c * KCH, KCH)
            for s in range(KCH // LANES):
                kc = ikey[:, pl.ds(pl.multiple_of(c0 + s * LANES, LANES), LANES)]
                cnt = cnt + jnp.where(kc >= cand, 1, 0)
            return cnt
        cnt = lax.fori_loop(0, n_ch, body, jnp.zeros((QB, LANES), I32))
        return jnp.sum(cnt, axis=1, keepdims=True)

    def search(bit, carry):
        u, cu = carry
        cand = u | (jnp.int32(1) << (31 - bit))
        cnt = count_ge(cand ^ INT_MIN)
        ok = cnt >= top_k
        return jnp.where(ok, cand, u), jnp.where(ok, cnt, cu)

    few = (t0 + lax.broadcasted_iota(I32, (QB, 1), 0)) < top_k

    def unsettled(carry):
        g, _, cu = carry
        return (g < 8) & (jnp.min(jnp.where(few | (cu == top_k), 1, 0)) == 0)

    def four_bits(carry):
        g, u, cu = carry
        u, cu = lax.fori_loop(4 * g, 4 * g + 4, search, (u, cu))
        return g + 1, u, cu

    _, u, cu = lax.while_loop(unsettled, four_bits,
                              (jnp.int32(0), jnp.zeros((QB, 1), I32), jnp.zeros((QB, 1), I32)))
    thr = jnp.maximum(u ^ INT_MIN, INT_MIN + 1)

    @pl.when(jnp.max(cu) > top_k)
    def _fix_ties():
        budget = (top_k - count_ge(thr + 1)).astype(F32)
        ut = jnp.where(rowi <= lane, 1.0, 0.0).astype(BF16)

        def body(j, before):
            sl = pl.ds(pl.multiple_of(j * LANES, LANES), LANES)
            kc = ikey[:, sl]
            eq = kc == thr
            eqf = jnp.where(eq, 1.0, 0.0)
            rank = before + _dot(eqf.astype(BF16), ut) - eqf
            ikey[:, sl] = jnp.where(eq & (rank >= budget), INT_MIN, kc)
            return before + jnp.sum(eqf, axis=1, keepdims=True)
        lax.fori_loop(0, i + 1, body, jnp.zeros((QB, 1), F32))

    def mask_chunk(c, carry):
        c0 = pl.multiple_of(c * KCH, KCH)
        for s in range(KCH // LANES):
            sl = pl.ds(pl.multiple_of(c0 + s * LANES, LANES), LANES)
            mbias[:, sl] = jnp.where(ikey[:, sl] >= thr, 0.0, NEG)
        return carry
    lax.fori_loop(0, n_ch, mask_chunk, 0)

    q = q_ref[...]
    even = lane < HEAD_DIM
    for p in range(N_HEADS // 2):
        qp = q[:, p * LANES:(p + 1) * LANES]
        qm[2 * p] = jnp.where(even, qp, jnp.zeros_like(qp))
        qm[2 * p + 1] = jnp.where(even, jnp.zeros_like(qp), qp)
    for h in range(N_HEADS):
        m_scr[h][...] = jnp.full((QB, LANES), NEG, F32)
        acc[h][...] = jnp.zeros((QB, LANES), F32)

    def key_block(start, width, bias_idx):
        rows = pl.ds(start, width)
        nsub = width // LANES
        for h in range(N_HEADS):
            cols = slice((h // 2) * LANES, (h // 2 + 1) * LANES)
            s_scr[h, :, 0:width] = _dot(qm[h], k_ref[rows, cols], NT)
        for h in range(N_HEADS):
            sub = []
            for c in range(nsub):
                sc = s_scr[h, :, c * LANES:(c + 1) * LANES] + mbias[:, pl.ds(start + c * LANES, LANES)]
                if bias_idx is not None:
                    sc = sc + btab[h, bias_idx]
                sub.append(sc)
            mx = sub[0]
            for c in range(1, nsub):
                mx = jnp.maximum(mx, sub[c])
            m_old = m_scr[h][...]
            m_new = jnp.maximum(m_old, jnp.max(mx, axis=1, keepdims=True))
            for c in range(nsub):
                p_scr[h, :, c * LANES:(c + 1) * LANES] = jnp.exp2(sub[c] - m_new).astype(BF16)
            a_scr[h][...] = jnp.exp2(m_old - m_new)
            m_scr[h][...] = m_new
        for h in range(N_HEADS):
            cols = slice((h // 2) * LANES, (h // 2 + 1) * LANES)
            v_ref = vo_ref if h % 2 else ve_ref
            acc[h][...] = a_scr[h][...] * acc[h][...] + _dot(p_scr[h, :, 0:width], v_ref[rows, cols])

    n_far = jnp.maximum(i - 1, 0) // (KCH // QB)

    def far_body(c, carry):
        key_block(pl.multiple_of(c * KCH, KCH), KCH, None)
        return carry
    lax.fori_loop(0, n_far, far_body, 0)

    def near_body(j, carry):
        key_block(pl.multiple_of(j * QB, QB), QB, jnp.minimum(i - j, 2))
        return carry
    lax.fori_loop(n_far * (KCH // QB), i + 1, near_body, 0)

    for p in range(N_HEADS // 2):
        ae, ao = acc[2 * p][...], acc[2 * p + 1][...]
        oe = ae / pltpu.roll(ae, HEAD_DIM, 1)
        oo = ao / pltpu.roll(ao, HEAD_DIM, 1)
        o_ref[:, p * LANES:(p + 1) * LANES] = jnp.where(even, oe, oo).astype(BF16)


def _attention(q, k, ve, vo, qi, kz, kw, rel_bias, B, S):
    nq = S // QB
    top_k = min(MAX_TOPK, S // 4)
    qrow = lambda n: pl.BlockSpec((QB, n), lambda b, i: (b * nq + i, 0))
    seq = lambda n: pl.BlockSpec((S, n), lambda b, i: (b, 0))
    return pl.pallas_call(
        functools.partial(_attn_kernel, top_k),
        grid=(B, nq),
        in_specs=[pl.BlockSpec(memory_space=pltpu.SMEM), qrow(HDIM), seq(HDIM), seq(HDIM), seq(HDIM),
                  qrow(HDIM), seq(2 * LANES), qrow(LANES)],
        out_specs=qrow(HDIM),
        out_shape=jax.ShapeDtypeStruct((B * S, HDIM), BF16),
        scratch_shapes=[pltpu.VMEM((QB, S), I32),
                        pltpu.VMEM((QB, S), F32),
                        pltpu.VMEM((N_HEADS, QB, LANES), F32),
                        pltpu.VMEM((N_HEADS, 3, QB, LANES), F32),
                        pltpu.VMEM((N_HEADS, QB, LANES), BF16),
                        pltpu.VMEM((N_HEADS, QB, KCH), F32),
                        pltpu.VMEM((N_HEADS, QB, KCH), BF16)]
        + [pltpu.VMEM((QB, LANES), F32)] * (3 * N_HEADS),
        compiler_params=_params(("parallel", "arbitrary")),
        name="dsa_attention",
    )(rel_bias, q, k, ve, vo, qi, kz, kw)


def _layer_norm(x, g, b):
    mu = jnp.mean(x, axis=1, keepdims=True)
    d = x - mu
    var = jnp.mean(d * d, axis=1, keepdims=True)
    return d * lax.rsqrt(var + LN_EPS) * g + b


def _merge_kernel(x_ref, ya_ref, at_ref, wg_ref, wa_ref, wb_ref, wo_ref, g1_ref, b1_ref, wr_ref, br_ref,
                  h_ref, ri_ref):
    x = x_ref[...]
    gates = _sigmoid(_dot(x.astype(BF16), wg_ref[...]))
    ya = _dot(ya_ref[...], wa_ref[...])
    yb = _dot(at_ref[...], wb_ref[...])
    mixin = gates[:, :D_MODEL] * ya + gates[:, D_MODEL:] * yb
    mix = _dot(mixin.astype(BF16), wo_ref[...])
    h = _layer_norm(ALPHA * x + mix, g1_ref[...], b1_ref[...])
    h_ref[...] = h
    lg = _mm3(h, wr_ref[...]) + br_ref[...]
    lane = lax.broadcasted_iota(I32, lg.shape, 1)
    gl = jnp.where(lane < N_GROUPS, lg, NEG)
    gmax = jnp.max(gl, axis=1, keepdims=True)
    p_g = 1.0 / jnp.sum(jnp.exp(gl - gmax), axis=1, keepdims=True)
    gsel = jnp.min(jnp.where(gl == gmax, lane, LANES), axis=1, keepdims=True)
    lo = N_GROUPS + EXPERTS_PER_GROUP * gsel
    el = jnp.where((lane >= lo) & (lane < lo + EXPERTS_PER_GROUP), lg, NEG)
    e1 = jnp.max(el, axis=1, keepdims=True)
    i1 = jnp.min(jnp.where(el == e1, lane, LANES), axis=1, keepdims=True)
    el2 = jnp.where(lane == i1, NEG, el)
    e2 = jnp.max(el2, axis=1, keepdims=True)
    i2 = jnp.min(jnp.where(el2 == e2, lane, LANES), axis=1, keepdims=True)
    w2 = jnp.exp(e2 - e1)
    gate1 = p_g / (1.0 + w2)
    gate2 = p_g * w2 / (1.0 + w2)
    ri = jnp.where(lane == 0, (i1 - N_GROUPS).astype(F32),
                   jnp.where(lane == 1, (i2 - N_GROUPS).astype(F32),
                             jnp.where(lane == 2, gate1, jnp.where(lane == 3, gate2, 0.0))))
    ri_ref[...] = ri


def _merge(x2, ya, at, wg, wa, wb, wo, g1, b1, wr, br, tm):
    T = x2.shape[0]
    row = lambda n: pl.BlockSpec((tm, n), lambda i: (i, 0))
    ws = [wg, wa, wb, wo, g1, b1, wr, br]
    return pl.pallas_call(
        _merge_kernel,
        grid=(T // tm,),
        in_specs=[row(D_MODEL), row(HDIM), row(HDIM)] + [_full(w.shape) for w in ws],
        out_specs=[row(D_MODEL), row(LANES)],
        out_shape=[jax.ShapeDtypeStruct((T, D_MODEL), F32), jax.ShapeDtypeStruct((T, LANES), F32)],
        compiler_params=_params(("parallel",)),
        name="merge_router",
    )(x2, ya, at, *ws)


def _row_gather(idx_ref, n, src_hbm, dst, sem):
    def issue(r, carry):
        pltpu.make_async_copy(src_hbm.at[pl.ds(idx_ref[0, 0, r], 1)], dst.at[pl.ds(r, 1)], sem).start()
        return carry
    lax.fori_loop(0, n, issue, 0, unroll=8)
    pltpu.make_async_copy(src_hbm.at[pl.ds(0, n)], dst, sem).wait()


def _moe_kernel(te_ref, nu_ref, src_ref, h_hbm, wg_ref, wu_ref, wd_ref, o_ref, xs, sem):
    t = pl.program_id(0)
    tm = xs.shape[0]

    @pl.when(t < nu_ref[0])
    def _():
        _row_gather(src_ref, tm, h_hbm, xs, sem)
        xb = xs[...].astype(BF16)
        hg = _dot(xb, wg_ref[0])
        hu = _dot(xb, wu_ref[0])
        act = (hg * _sigmoid(hg)) * hu
        o_ref[...] = _dot(act.astype(BF16), wd_ref[0])

    @pl.when(t >= nu_ref[0])
    def _():
        o_ref[...] = jnp.zeros(o_ref.shape, F32)


def _moe(tile_e, n_used, src3, h, wg, wu, wd, tm):
    n_tiles = src3.shape[0]
    grid_spec = pltpu.PrefetchScalarGridSpec(
        num_scalar_prefetch=2,
        grid=(n_tiles,),
        in_specs=[pl.BlockSpec((1, 1, tm), lambda t, te, nu: (t, 0, 0), memory_space=pltpu.SMEM),
                  pl.BlockSpec(memory_space=pl.ANY),
                  pl.BlockSpec((1, D_MODEL, D_EXPERT), lambda t, te, nu: (te[t], 0, 0)),
                  pl.BlockSpec((1, D_MODEL, D_EXPERT), lambda t, te, nu: (te[t], 0, 0)),
                  pl.BlockSpec((1, D_EXPERT, D_MODEL), lambda t, te, nu: (te[t], 0, 0))],
        out_specs=pl.BlockSpec((tm, D_MODEL), lambda t, te, nu: (t, 0)),
        scratch_shapes=[pltpu.VMEM((tm, D_MODEL), F32), pltpu.SemaphoreType.DMA(())],
    )
    return pl.pallas_call(
        _moe_kernel,
        grid_spec=grid_spec,
        out_shape=jax.ShapeDtypeStruct((n_tiles * tm, D_MODEL), F32),
        compiler_params=_params(("arbitrary",)),
        name="moe_experts",
    )(tile_e, n_used, src3, h, wg, wu, wd)


def _final_kernel(p0_ref, p1_ref, h_ref, ri_ref, ys_hbm, g2_ref, b2_ref, o_ref, y0, y1, sem0, sem1):
    tm = y0.shape[0]
    _row_gather(p0_ref, tm, ys_hbm, y0, sem0)
    _row_gather(p1_ref, tm, ys_hbm, y1, sem1)
    ri = ri_ref[...]
    moe = y0[...] * ri[:, 2:3] + y1[...] * ri[:, 3:4]
    o_ref[...] = _layer_norm(ALPHA * h_ref[...] + moe, g2_ref[...], b2_ref[...])


def _final(pos0, pos1, h, ri, ys, g2, b2, tm):
    T = h.shape[0]
    idx = pl.BlockSpec((1, 1, tm), lambda i: (i, 0, 0), memory_space=pltpu.SMEM)
    row = lambda n: pl.BlockSpec((tm, n), lambda i: (i, 0))
    return pl.pallas_call(
        _final_kernel,
        grid=(T // tm,),
        in_specs=[idx, idx, row(D_MODEL), row(LANES), pl.BlockSpec(memory_space=pl.ANY),
                  _full(g2.shape), _full(b2.shape)],
        out_specs=row(D_MODEL),
        out_shape=jax.ShapeDtypeStruct((T, D_MODEL), F32),
        scratch_shapes=[pltpu.VMEM((tm, D_MODEL), F32), pltpu.VMEM((tm, D_MODEL), F32),
                        pltpu.SemaphoreType.DMA(()), pltpu.SemaphoreType.DMA(())],
        compiler_params=_params(("arbitrary",)),
        name="combine_ln",
    )(pos0, pos1, h, ri, ys, g2, b2)


def _routing_tables(eid, tm):
    T = eid.shape[0]
    flat = eid.reshape(-1)
    n_tiles = (2 * T) // tm + N_EXPERTS
    order = jnp.argsort(flat, stable=True).astype(I32)
    sizes = jnp.zeros((N_EXPERTS,), I32).at[flat].add(1)
    padded = ((sizes + tm - 1) // tm) * tm
    pad_end = jnp.cumsum(padded)
    pad_off = pad_end - padded
    off = jnp.cumsum(sizes) - sizes
    e_sorted = flat[order]
    dest = pad_off[e_sorted] + (jnp.arange(2 * T, dtype=I32) - off[e_sorted])
    src_tok = jnp.zeros((n_tiles * tm,), I32).at[dest].set(order // 2)
    pos = jnp.zeros((2 * T,), I32).at[order].set(dest).reshape(T, 2)
    tile_e = jnp.minimum(jnp.searchsorted(pad_end, jnp.arange(n_tiles, dtype=I32) * tm, side="right"),
                         N_EXPERTS - 1).astype(I32)
    n_used = (pad_end[-1] // tm).astype(I32).reshape(1)
    return tile_e, n_used, src_tok.reshape(n_tiles, 1, tm), pos


def _block_diag_ones():
    hid = jnp.arange(HDIM, dtype=I32) // HEAD_DIM
    return (hid[:, None] == hid[None, :]).astype(BF16)


def _layer(x, w_in, mu_shift, w0, w_lora_up, a0, a_lora_up, g_lora_up, k_k, k_a, r_k, ln_x_g, ln_x_b,
           w_branch_a, idx_k_ln_g, idx_k_ln_b, rel_bias, w_branch_b, w_out, ln1_g, ln1_b,
           w_router_grp, b_router_grp, w_router_exp, b_router_exp, w_gate, w_up, w_down, ln2_g, ln2_b):
    B, S, _ = x.shape
    T = B * S
    x2 = x.reshape(T, D_MODEL)
    row = lambda t: t.reshape(1, -1)

    c_r, c_wd, c_k, c_v, c_ad, c_gd = 0, 512, 576, 1088, 1600, 1664
    perm = jnp.concatenate([jnp.arange(c_r, c_r + 512), jnp.arange(c_k, c_k + 512), jnp.arange(c_v, c_v + 512),
                            jnp.arange(c_wd, c_wd + 64), jnp.arange(c_ad, c_ad + 64),
                            jnp.arange(c_gd, c_gd + 128)])
    o_q = RWKV_COLS
    o_c = o_q + 4 * HDIM
    o_g = o_c + IDX_DIM + N_HEADS
    wr = w_in[:, :RWKV_COLS][:, perm].astype(BF16)
    wq = w_in[:, o_q:o_c].astype(BF16)
    wc = jnp.pad(w_in[:, o_c:o_g], ((0, 0), (0, LANES - IDX_DIM - N_HEADS))).astype(BF16)
    wgates = w_in[:, o_g:].astype(BF16)
    pad_idx = lambda t: jnp.pad(t, (0, LANES - IDX_DIM)).reshape(1, LANES)

    pr, q, k, ve, vo, qi, kz, kw = _inproj(x2, wr, wq, wc, pad_idx(idx_k_ln_g), pad_idx(idx_k_ln_b), tm=512)

    bd = _block_diag_ones()
    prm = {
        "mu": row(mu_shift[perm]), "w0": row(w0), "a0": row(a0), "k_k": row(k_k), "k_a": row(k_a),
        "r_k": row(r_k),
        "wup": jnp.pad(w_lora_up, ((0, AAA_LORA), (0, 0))),
        "aup": jnp.pad(a_lora_up, ((DECAY_LORA, 0), (0, 0))),
        "gup": g_lora_up, "bd": bd,
    }
    arrs = _rwkv_prep(pr, S, prm, tm=256)
    ya = _rwkv_scan(arrs, row(ln_x_g), row(ln_x_b), bd, B, S)

    at = _attention(q, k, ve, vo, qi, kz, kw, rel_bias, B, S)

    w_router = jnp.pad(jnp.concatenate([w_router_grp, w_router_exp], axis=1),
                       ((0, 0), (0, LANES - N_GROUPS - N_EXPERTS)))
    b_router = jnp.pad(jnp.concatenate([b_router_grp, b_router_exp]), (0, LANES - N_GROUPS - N_EXPERTS))
    h1, ri = _merge(x2, ya, at, wgates, w_branch_a.astype(BF16), w_branch_b.astype(BF16), w_out.astype(BF16),
                    row(ln1_g), row(ln1_b), w_router, row(b_router), tm=256)

    tm_e = 512
    eid = ri[:, 0:2].astype(I32)
    tile_e, n_used, src3, pos = _routing_tables(eid, tm_e)
    ys = _moe(tile_e, n_used, src3, h1, w_gate.astype(BF16), w_up.astype(BF16), w_down.astype(BF16), tm_e)

    tm_f = 256
    pos0 = pos[:, 0].reshape(T // tm_f, 1, tm_f)
    pos1 = pos[:, 1].reshape(T // tm_f, 1, tm_f)
    out = _final(pos0, pos1, h1, ri, ys, row(ln2_g), row(ln2_b), tm_f)
    return out.reshape(B, S, D_MODEL)


def kernel(x, w_in, mu_shift, w0, w_lora_up, a0, a_lora_up, g_lora_up, k_k, k_a, r_k, ln_x_g, ln_x_b, w_branch_a, idx_k_ln_g, idx_k_ln_b, rel_bias, w_branch_b, w_out, ln1_g, ln1_b, w_router_grp, b_router_grp, w_router_exp, b_router_exp, w_expert_gate, w_expert_up, w_expert_down, ln2_g, ln2_b):
    assert w_in.shape[0] == 1, "single-layer (DEPTH = 1) block"
    l = 0
    return _layer(x, w_in[l], mu_shift[l], w0[l], w_lora_up[l], a0[l], a_lora_up[l], g_lora_up[l], k_k[l],
                  k_a[l], r_k[l], ln_x_g[l], ln_x_b[l], w_branch_a[l], idx_k_ln_g[l], idx_k_ln_b[l], rel_bias,
                  w_branch_b[l], w_out[l], ln1_g[l], ln1_b[l], w_router_grp[l], b_router_grp[l],
                  w_router_exp[l], b_router_exp[l], w_expert_gate[l], w_expert_up[l], w_expert_down[l],
                  ln2_g[l], ln2_b[l])
```

```python
import functools
import math

import jax
import jax.numpy as jnp
from jax import lax
from jax.experimental import pallas as pl
from jax.experimental.pallas import tpu as pltpu

F32 = jnp.float32
BF16 = jnp.bfloat16
I32 = jnp.int32

D_MODEL = 1024
HEAD_DIM = 64
N_HEADS = 8
HDIM = N_HEADS * HEAD_DIM
DECAY_LORA = 64
AAA_LORA = 64
GATE_LORA = 128
RWKV_COLS = 3 * HDIM + DECAY_LORA + AAA_LORA + GATE_LORA
IDX_DIM = 64
MAX_TOPK = 256
N_BUCKETS = 32
MAX_EXACT = 16
MAX_DISTANCE = 128
N_GROUPS = 4
EXPERTS_PER_GROUP = 8
N_EXPERTS = 32
D_EXPERT = 512
GN_EPS = 64e-5
LN_EPS = 1e-5
ALPHA = 2.0 ** 0.25
LANES = 128
CHUNK = 64
SCAN_SUB = 2
QB = 128
KCH = 512
INT_MIN = -(2 ** 31)
NEG = -1e30
LOG2E = 1.4426950408889634
VMEM_LIMIT = 56 * 1024 * 1024

NN = (((1,), (0,)), ((), ()))
NT = (((1,), (1,)), ((), ()))


def _dot(a, b, dims=NN):
    return lax.dot_general(a, b, dims, preferred_element_type=F32)


def _split2(x):
    hi = x.astype(BF16)
    lo = (x - hi.astype(F32)).astype(BF16)
    return hi, lo


def _split3(x):
    hi = x.astype(BF16)
    r1 = x - hi.astype(F32)
    mid = r1.astype(BF16)
    lo = (r1 - mid.astype(F32)).astype(BF16)
    return hi, mid, lo


def _mm1(a, b, dims=NN):
    return _dot(a.astype(BF16), b.astype(BF16), dims)


def _mm3(a, b, dims=NN):
    ah, al = _split2(a)
    bh, bl = _split2(b)
    return _dot(ah, bh, dims) + (_dot(ah, bl, dims) + _dot(al, bh, dims))


_mm_misc = _mm1
_mm_inv = _mm1
_mm_state = _mm3


def _mm_exact_lhs(a_bf, b):
    b0, b1, b2 = _split3(b)
    return _dot(a_bf, b0) + (_dot(a_bf, b1) + _dot(a_bf, b2))


def _mm_exact_rhs(a, b_bf):
    a0, a1, a2 = _split3(a)
    return _dot(a0, b_bf) + (_dot(a1, b_bf) + _dot(a2, b_bf))


def _params(sem, vmem=VMEM_LIMIT):
    return pltpu.CompilerParams(dimension_semantics=sem, vmem_limit_bytes=vmem)


def _full(shape):
    nd = len(shape)
    return pl.BlockSpec(shape, lambda *_: (0,) * nd)


def _inproj_kernel(x_ref, wr_ref, wq_ref, wc_ref, lng_ref, lnb_ref,
                   pr_ref, q_ref, k_ref, ve_ref, vo_ref, qi_ref, kz_ref, kw_ref):
    xb = x_ref[...].astype(BF16)
    pr_ref[...] = _dot(xb, wr_ref[...])
    qkv = _dot(xb, wq_ref[...])
    q_ref[...] = (qkv[:, 0:HDIM] * (HEAD_DIM ** -0.5 * LOG2E)).astype(BF16)
    k_ref[...] = qkv[:, HDIM:2 * HDIM].astype(BF16)
    v = qkv[:, 2 * HDIM:3 * HDIM]
    even = (lax.broadcasted_iota(I32, v.shape, 1) % LANES) < HEAD_DIM
    ve_ref[...] = jnp.where(even, v, 1.0).astype(BF16)
    vo_ref[...] = jnp.where(even, 1.0, v).astype(BF16)
    qi_ref[...] = qkv[:, 3 * HDIM:4 * HDIM].astype(BF16)
    c = _dot(xb, wc_ref[...])
    kw_ref[...] = c
    lane = lax.broadcasted_iota(I32, c.shape, 1)
    isk = lane < IDX_DIM
    mu = jnp.sum(jnp.where(isk, c, 0.0), axis=1, keepdims=True) * (1.0 / IDX_DIM)
    d = jnp.where(isk, c - mu, 0.0)
    var = jnp.sum(d * d, axis=1, keepdims=True) * (1.0 / IDX_DIM)
    kn = d * lax.rsqrt(var + LN_EPS) * lng_ref[...] + lnb_ref[...]
    kz_ref[:, 0:LANES] = kn.astype(BF16)
    kz_ref[:, LANES:2 * LANES] = pltpu.roll(kn, IDX_DIM, 1).astype(BF16)


def _inproj(x2, wr, wq, wc, lng, lnb, tm):
    T = x2.shape[0]
    row = lambda n: pl.BlockSpec((tm, n), lambda i: (i, 0))
    return pl.pallas_call(
        _inproj_kernel,
        grid=(T // tm,),
        in_specs=[row(D_MODEL), _full(wr.shape), _full(wq.shape), _full(wc.shape),
                  _full(lng.shape), _full(lnb.shape)],
        out_specs=[row(RWKV_COLS), row(HDIM), row(HDIM), row(HDIM), row(HDIM), row(HDIM),
                   row(2 * LANES), row(LANES)],
        out_shape=[jax.ShapeDtypeStruct((T, RWKV_COLS), F32)]
        + [jax.ShapeDtypeStruct((T, HDIM), BF16)] * 5
        + [jax.ShapeDtypeStruct((T, 2 * LANES), BF16), jax.ShapeDtypeStruct((T, LANES), F32)],
        compiler_params=_params(("parallel",)),
        name="inproj",
    )(x2, wr, wq, wc, lng, lnb)


def _softplus(x):
    return jnp.maximum(x, 0.0) + jnp.log(1.0 + jnp.exp(-jnp.abs(x)))


def _sigmoid(x):
    return 1.0 / (1.0 + jnp.exp(-x))


def _prep_kernel(tiles_per_seq, p_ref, pp_ref, mu_ref, w0_ref, a0_ref, kk_ref, ka_ref, rk_ref,
                 wup_ref, aup_ref, gup_ref, bd_ref,
                 r_ref, lw_ref, k_ref, v_ref, a_ref, b_ref, g_ref, bon_ref):
    i = pl.program_id(0)
    p = p_ref[...]
    tm = p.shape[0]
    first = (i % tiles_per_seq) == 0
    prow = jnp.where(first, 0.0, pp_ref[7:8, :])
    rowid = lax.broadcasted_iota(I32, p.shape, 0)
    prev = jnp.where(rowid == 0, prow, pltpu.roll(p, 1, 0))
    ps = p + (prev - p) * mu_ref[...]
    r = ps[:, 0:HDIM]
    k = ps[:, HDIM:2 * HDIM]
    v = ps[:, 2 * HDIM:3 * HDIM]
    da = ps[:, 3 * HDIM:3 * HDIM + LANES]
    gd = ps[:, 3 * HDIM + LANES:3 * HDIM + 2 * LANES]
    w = -_softplus(-(w0_ref[...] + _mm3(jnp.tanh(da), wup_ref[...]))) - 0.5
    lw_ref[...] = -jnp.exp(w)
    a = _sigmoid(a0_ref[...] + _mm3(da, aup_ref[...]))
    g_ref[...] = _mm3(_sigmoid(gd), gup_ref[...])
    bd = bd_ref[...]
    kk = k * kk_ref[...]
    ss = _mm_exact_rhs(kk * kk, bd)
    kk = kk / jnp.maximum(jnp.sqrt(ss), 1e-12)
    k2 = k * (1.0 + (a - 1.0) * ka_ref[...])
    r_ref[...] = r
    k_ref[...] = k2
    v_ref[...] = v
    a_ref[...] = -kk
    b_ref[...] = kk * a
    bon_ref[...] = _mm_exact_rhs(r * k2 * rk_ref[...], bd) * v


def _rwkv_prep(pr, S, prm, tm):
    T = pr.shape[0]
    row = lambda n: pl.BlockSpec((tm, n), lambda i: (i, 0))
    prev = pl.BlockSpec((8, RWKV_COLS), lambda i: (jnp.maximum(i * (tm // 8) - 1, 0), 0))
    names = ["mu", "w0", "a0", "k_k", "k_a", "r_k", "wup", "aup", "gup", "bd"]
    return pl.pallas_call(
        functools.partial(_prep_kernel, S // tm),
        grid=(T // tm,),
        in_specs=[row(RWKV_COLS), prev] + [_full(prm[n].shape) for n in names],
        out_specs=[row(HDIM)] * 8,
        out_shape=[jax.ShapeDtypeStruct((T, HDIM), F32)] * 8,
        compiler_params=_params(("parallel",)),
        name="rwkv_prep",
    )(pr, pr, *[prm[n] for n in names])


def _scan_kernel(r_ref, lw_ref, k_ref, v_ref, a_ref, b_ref, g_ref, bon_ref, lng_ref, lnb_ref, bd_ref,
                 o_ref, st_ref, y_ref):
    C, N, H = CHUNK, HEAD_DIM, N_HEADS

    @pl.when(pl.program_id(1) == 0)
    def _():
        st_ref[...] = jnp.zeros(st_ref.shape, F32)

    ri = lax.broadcasted_iota(I32, (C, C), 0)
    ci = lax.broadcasted_iota(I32, (C, C), 1)
    incl = ri >= ci
    strict = ri > ci
    eye = ri == ci
    eye_f = jnp.where(eye, 1.0, 0.0)
    lmat = jnp.where(incl, 1.0, 0.0).astype(BF16)
    sls = [slice(h * N, (h + 1) * N) for h in range(H)]
    units = [(s, h) for s in range(SCAN_SUB) for h in range(H)]
    ah, rh, vh, bT, kT, bhT, khT, gam = {}, {}, {}, {}, {}, {}, {}, {}
    for s in range(SCAN_SUB):
        rows = slice(s * C, (s + 1) * C)
        lw = lw_ref[rows, :]
        cum = _mm_exact_lhs(lmat, lw)
        last = cum[C - 1:C, :]
        e_i = jnp.exp(-cum)
        e_end = jnp.exp(last - cum)
        g_s = jnp.exp(last)
        r_t = r_ref[rows, :] * jnp.exp(cum)
        a_t = a_ref[rows, :] * jnp.exp(cum - lw)
        v_s = v_ref[rows, :]
        b_s, k_s = b_ref[rows, :], k_ref[rows, :]
        bT_s, kT_s = (b_s * e_i).T, (k_s * e_i).T
        bhT_s, khT_s = (b_s * e_end).T, (k_s * e_end).T
        for h in range(H):
            un = (s, h)
            ah[un], rh[un], vh[un] = a_t[:, sls[h]], r_t[:, sls[h]], v_s[:, sls[h]]
            bT[un], kT[un] = bT_s[sls[h], :], kT_s[sls[h], :]
            bhT[un], khT[un] = bhT_s[sls[h], :], khT_s[sls[h], :]
            gam[un] = g_s[:, sls[h]]

    ar = {un: jnp.concatenate([ah[un], rh[un]], axis=0) for un in units}
    sb = {un: _mm_misc(ar[un], bT[un]) for un in units}
    sk = {un: _mm_misc(ar[un], kT[un]) for un in units}
    a_ab = {un: jnp.where(strict, sb[un][:C], 0.0) for un in units}
    a_rb = {un: jnp.where(incl, sb[un][C:], 0.0) for un in units}
    a_ak = {un: jnp.where(strict, sk[un][:C], 0.0) for un in units}
    a_rk = {un: jnp.where(incl, sk[un][C:], 0.0) for un in units}
    u = {un: _mm_misc(a_ak[un], vh[un]) for un in units}
    tinv = {un: eye_f + a_ab[un] for un in units}
    xp = a_ab
    for _ in range(5):
        xp = {un: _mm_inv(xp[un], xp[un]) for un in units}
        tinv = {un: tinv[un] + _mm_inv(tinv[un], xp[un]) for un in units}
    pm = {un: _mm_misc(tinv[un], ah[un]) for un in units}
    qm = {un: _mm_misc(tinv[un], u[un]) for un in units}
    r2 = {un: rh[un] + _mm_misc(a_rb[un], pm[un]) for un in units}
    mmat = {un: jnp.where(eye, gam[un], 0.0) + _mm_misc(bhT[un], pm[un]) for un in units}
    y0 = {un: _mm_misc(a_rb[un], qm[un]) + _mm_misc(a_rk[un], vh[un]) for un in units}
    gmat = {un: _mm_misc(bhT[un], qm[un]) + _mm_misc(khT[un], vh[un]) for un in units}
    st = [st_ref[h] for h in range(H)]
    for s in range(SCAN_SUB):
        for h in range(H):
            y_ref[s * C:(s + 1) * C, sls[h]] = _mm_state(r2[(s, h)], st[h]) + y0[(s, h)]
        st = [_mm_state(mmat[(s, h)], st[h]) + gmat[(s, h)] for h in range(H)]
    for h in range(H):
        st_ref[h] = st[h]

    y = y_ref[...]
    bd = bd_ref[...]
    mu = _mm_exact_rhs(y, bd) * (1.0 / N)
    d = y - mu
    var = _mm_exact_rhs(d * d, bd) * (1.0 / N)
    yn = d * lax.rsqrt(var + GN_EPS) * lng_ref[...] + lnb_ref[...]
    o_ref[...] = ((yn + bon_ref[...]) * g_ref[...]).astype(BF16)


def _rwkv_scan(arrs, lng, lnb, bd, B, S):
    rows = SCAN_SUB * CHUNK
    nc = S // rows
    row = pl.BlockSpec((rows, HDIM), lambda b, c: (b * nc + c, 0))
    return pl.pallas_call(
        _scan_kernel,
        grid=(B, nc),
        in_specs=[row] * 8 + [_full(lng.shape), _full(lnb.shape), _full(bd.shape)],
        out_specs=row,
        out_shape=jax.ShapeDtypeStruct((B * S, HDIM), BF16),
        scratch_shapes=[pltpu.VMEM((N_HEADS, HEAD_DIM, HEAD_DIM), F32), pltpu.VMEM((rows, HDIM), F32)],
        compiler_params=_params(("parallel", "arbitrary")),
        name="rwkv_scan",
    )(*arrs, lng, lnb, bd)


def _sort_key(x):
    bits = pltpu.bitcast(x, I32)
    return bits ^ ((bits >> 31) & 0x7FFFFFFF)


def _attn_kernel(top_k, rb_ref, q_ref, k_ref, ve_ref, vo_ref, qi_ref, kz_ref, kw_ref, o_ref,
                 ikey, mbias, wb, btab, qm, s_scr, p_scr, *state):
    m_scr, acc, a_scr = (state[n * N_HEADS:(n + 1) * N_HEADS] for n in range(3))
    i = pl.program_id(1)
    t0 = i * QB
    lane = lax.broadcasted_iota(I32, (QB, LANES), 1)
    rowi = lax.broadcasted_iota(I32, (QB, LANES), 0)

    @pl.when(i == 0)
    def _build_bias():
        for h in range(N_HEADS):
            btab[h, 2] = jnp.zeros((QB, LANES), F32)
        for m in range(2):
            n = jnp.maximum(m * QB + rowi - lane, 0)
            nf = jnp.maximum(n, 1).astype(F32)
            large = MAX_EXACT + (jnp.log(nf / MAX_EXACT) / math.log(MAX_DISTANCE / MAX_EXACT)
                                 * (N_BUCKETS - MAX_EXACT)).astype(I32)
            bucket = jnp.where(n < MAX_EXACT, n, jnp.minimum(large, N_BUCKETS - 1))
            for h in range(N_HEADS):
                t = jnp.zeros((QB, LANES), F32)
                for bk in range(N_BUCKETS):
                    t = jnp.where(bucket == bk, rb_ref[bk, h], t)
                btab[h, m] = (t - rb_ref[N_BUCKETS - 1, h]) * LOG2E

    kw = kw_ref[...]
    for h in range(N_HEADS):
        wb[h] = jnp.broadcast_to(kw[:, IDX_DIM + h:IDX_DIM + h + 1] * (N_HEADS ** -0.5), (QB, LANES)) \
            * (IDX_DIM ** -0.5)
    n_ch = i // (KCH // QB) + 1

    def score_chunk(c, carry):
        c0 = pl.multiple_of(c * KCH, KCH)
        tot = [jnp.zeros((QB, LANES), F32) for _ in range(KCH // LANES)]
        for p in range(N_HEADS // 2):
            qp = qi_ref[:, p * LANES:(p + 1) * LANES]
            for e in range(2):
                z = _dot(qp, kz_ref[pl.ds(c0, KCH), e * LANES:(e + 1) * LANES], NT)
                wbh = wb[2 * p + e]
                for s in range(KCH // LANES):
                    tot[s] = tot[s] + jnp.maximum(z[:, s * LANES:(s + 1) * LANES], 0.0) * wbh
        for s in range(KCH // LANES):
            spos = c0 + s * LANES + lane
            key = jnp.where(spos <= t0 + rowi, _sort_key(tot[s]), INT_MIN)
            ikey[:, pl.ds(pl.multiple_of(c0 + s * LANES, LANES), LANES)] = key
        return carry

    lax.fori_loop(0, n_ch, score_chunk, 0)

    def count_ge(cand):
        def body(c, cnt):
            c0 = pl.multiple_of(c * KCH, KCH)
            for s in range(KCH // LANES):
                kc = ikey[:, pl.ds(pl.multiple_of(c0 + s * LANES, LANES), LANES)]
                cnt = cnt + jnp.where(kc >= cand, 1, 0)
            return cnt
        cnt = lax.fori_loop(0, n_ch, body, jnp.zeros((QB, LANES), I32))
        return jnp.sum(cnt, axis=1, keepdims=True)

    def search(bit, carry):
        u, cu = carry
        cand = u | (jnp.int32(1) << (31 - bit))
        cnt = count_ge(cand ^ INT_MIN)
        ok = cnt >= top_k
        return jnp.where(ok, cand, u), jnp.where(ok, cnt, cu)

    few = (t0 + lax.broadcasted_iota(I32, (QB, 1), 0)) < top_k

    def unsettled(carry):
        g, _, cu = carry
        return (g < 8) & (jnp.min(jnp.where(few | (cu == top_k), 1, 0)) == 0)

    def four_bits(carry):
        g, u, cu = carry
        u, cu = lax.fori_loop(4 * g, 4 * g + 4, search, (u, cu))
        return g + 1, u, cu

    _, u, cu = lax.while_loop(unsettled, four_bits,
                              (jnp.int32(0), jnp.zeros((QB, 1), I32), jnp.zeros((QB, 1), I32)))
    thr = jnp.maximum(u ^ INT_MIN, INT_MIN + 1)

    @pl.when(jnp.max(cu) > top_k)
    def _fix_ties():
        budget = (top_k - count_ge(thr + 1)).astype(F32)
        ut = jnp.where(rowi <= lane, 1.0, 0.0).astype(BF16)

        def body(j, before):
            sl = pl.ds(pl.multiple_of(j * LANES, LANES), LANES)
            kc = ikey[:, sl]
            eq = kc == thr
            eqf = jnp.where(eq, 1.0, 0.0)
            rank = before + _dot(eqf.astype(BF16), ut) - eqf
            ikey[:, sl] = jnp.where(eq & (rank >= budget), INT_MIN, kc)
            return before + jnp.sum(eqf, axis=1, keepdims=True)
        lax.fori_loop(0, i + 1, body, jnp.zeros((QB, 1), F32))

    def mask_chunk(c, carry):
        c0 = pl.multiple_of(c * KCH, KCH)
        for s in range(KCH // LANES):
            sl = pl.ds(pl.multiple_of(c0 + s * LANES, LANES), LANES)
            mbias[:, sl] = jnp.where(ikey[:, sl] >= thr, 0.0, NEG)
        return carry
    lax.fori_loop(0, n_ch, mask_chunk, 0)

    q = q_ref[...]
    even = lane < HEAD_DIM
    for p in range(N_HEADS // 2):
        qp = q[:, p * LANES:(p + 1) * LANES]
        qm[2 * p] = jnp.where(even, qp, jnp.zeros_like(qp))
        qm[2 * p + 1] = jnp.where(even, jnp.zeros_like(qp), qp)
    for h in range(N_HEADS):
        m_scr[h][...] = jnp.full((QB, LANES), NEG, F32)
        acc[h][...] = jnp.zeros((QB, LANES), F32)

    def key_block(c, near):
        start = pl.multiple_of(c * KCH, KCH)
        rows = pl.ds(start, KCH)
        nsub = KCH // LANES
        for h in range(N_HEADS):
            cols = slice((h // 2) * LANES, (h // 2 + 1) * LANES)
            s_scr[h] = _dot(qm[h], k_ref[rows, cols], NT)
        for h in range(N_HEADS):
            sub = []
            for n in range(nsub):
                sc = s_scr[h, :, n * LANES:(n + 1) * LANES] + mbias[:, pl.ds(start + n * LANES, LANES)]
                if near:
                    sc = sc + btab[h, jnp.clip(i - (c * nsub + n), 0, 2)]
                sub.append(sc)
            mx = jnp.maximum(jnp.maximum(sub[0], sub[1]), jnp.maximum(sub[2], sub[3]))
            m_old = m_scr[h][...]
            m_new = jnp.maximum(m_old, jnp.max(mx, axis=1, keepdims=True))
            for n in range(nsub):
                p_scr[h, :, n * LANES:(n + 1) * LANES] = jnp.exp2(sub[n] - m_new).astype(BF16)
            a_scr[h][...] = jnp.exp2(m_old - m_new)
            m_scr[h][...] = m_new
        for h in range(N_HEADS):
            cols = slice((h // 2) * LANES, (h // 2 + 1) * LANES)
            v_ref = vo_ref if h % 2 else ve_ref
            acc[h][...] = a_scr[h][...] * acc[h][...] + _dot(p_scr[h], v_ref[rows, cols])

    n_far = jnp.maximum(i - 1, 0) // (KCH // QB)

    def far_body(c, carry):
        key_block(c, False)
        return carry
    lax.fori_loop(0, n_far, far_body, 0)

    def near_body(c, carry):
        key_block(c, True)
        return carry
    lax.fori_loop(n_far, n_ch, near_body, 0)

    for p in range(N_HEADS // 2):
        ae, ao = acc[2 * p][...], acc[2 * p + 1][...]
        oe = ae / pltpu.roll(ae, HEAD_DIM, 1)
        oo = ao / pltpu.roll(ao, HEAD_DIM, 1)
        o_ref[:, p * LANES:(p + 1) * LANES] = jnp.where(even, oe, oo).astype(BF16)


def _attention(q, k, ve, vo, qi, kz, kw, rel_bias, B, S):
    nq = S // QB
    top_k = min(MAX_TOPK, S // 4)
    qrow = lambda n: pl.BlockSpec((QB, n), lambda b, i: (b * nq + i, 0))
    seq = lambda n: pl.BlockSpec((S, n), lambda b, i: (b, 0))
    return pl.pallas_call(
        functools.partial(_attn_kernel, top_k),
        grid=(B, nq),
        in_specs=[pl.BlockSpec(memory_space=pltpu.SMEM), qrow(HDIM), seq(HDIM), seq(HDIM), seq(HDIM),
                  qrow(HDIM), seq(2 * LANES), qrow(LANES)],
        out_specs=qrow(HDIM),
        out_shape=jax.ShapeDtypeStruct((B * S, HDIM), BF16),
        scratch_shapes=[pltpu.VMEM((QB, S), I32),
                        pltpu.VMEM((QB, S), F32),
                        pltpu.VMEM((N_HEADS, QB, LANES), F32),
                        pltpu.VMEM((N_HEADS, 3, QB, LANES), F32),
                        pltpu.VMEM((N_HEADS, QB, LANES), BF16),
                        pltpu.VMEM((N_HEADS, QB, KCH), F32),
                        pltpu.VMEM((N_HEADS, QB, KCH), BF16)]
        + [pltpu.VMEM((QB, LANES), F32)] * (3 * N_HEADS),
        compiler_params=_params(("parallel", "arbitrary")),
        name="dsa_attention",
    )(rel_bias, q, k, ve, vo, qi, kz, kw)


def _layer_norm(x, g, b):
    mu = jnp.mean(x, axis=1, keepdims=True)
    d = x - mu
    var = jnp.mean(d * d, axis=1, keepdims=True)
    return d * lax.rsqrt(var + LN_EPS) * g + b


def _merge_kernel(x_ref, ya_ref, at_ref, wg_ref, wa_ref, wb_ref, wo_ref, g1_ref, b1_ref, wr_ref, br_ref,
                  h_ref, ri_ref, cnt_ref):
    @pl.when(pl.program_id(0) == 0)
    def _():
        cnt_ref[...] = jnp.zeros(cnt_ref.shape, F32)
    half = x_ref.shape[0] // 2
    halves = [slice(sb * half, (sb + 1) * half) for sb in range(2)]
    pre = []
    for rows in halves:
        g = _dot(x_ref[rows, :].astype(BF16), wg_ref[...])
        pre.append((g, _dot(ya_ref[rows, :], wa_ref[...]), _dot(at_ref[rows, :], wb_ref[...])))
    mixes = []
    for g, ya, yb in pre:
        mixin = _sigmoid(g[:, :D_MODEL]) * ya + _sigmoid(g[:, D_MODEL:]) * yb
        mixes.append(_dot(mixin.astype(BF16), wo_ref[...]))
    for rows, mix in zip(halves, mixes):
        h_ref[rows, :] = _layer_norm(ALPHA * x_ref[rows, :] + mix, g1_ref[...], b1_ref[...])
    for rows in halves:
        _route_rows(h_ref.at[rows], wr_ref, br_ref, ri_ref.at[rows], cnt_ref)


def _route_rows(h_ref, wr_ref, br_ref, ri_ref, cnt_ref):
    lg = _mm3(h_ref[...], wr_ref[...]) + br_ref[...]
    lane = lax.broadcasted_iota(I32, lg.shape, 1)
    gl = jnp.where(lane < N_GROUPS, lg, NEG)
    gmax = jnp.max(gl, axis=1, keepdims=True)
    p_g = 1.0 / jnp.sum(jnp.exp(gl - gmax), axis=1, keepdims=True)
    gsel = jnp.min(jnp.where(gl == gmax, lane, LANES), axis=1, keepdims=True)
    lo = N_GROUPS + EXPERTS_PER_GROUP * gsel
    el = jnp.where((lane >= lo) & (lane < lo + EXPERTS_PER_GROUP), lg, NEG)
    e1 = jnp.max(el, axis=1, keepdims=True)
    i1 = jnp.min(jnp.where(el == e1, lane, LANES), axis=1, keepdims=True)
    el2 = jnp.where(lane == i1, NEG, el)
    e2 = jnp.max(el2, axis=1, keepdims=True)
    i2 = jnp.min(jnp.where(el2 == e2, lane, LANES), axis=1, keepdims=True)
    w2 = jnp.exp(e2 - e1)
    gate1 = p_g / (1.0 + w2)
    gate2 = p_g * w2 / (1.0 + w2)
    tm = lg.shape[0]
    oh = jnp.concatenate([jnp.where(lane == i1 - N_GROUPS, 1.0, 0.0),
                          jnp.where(lane == i2 - N_GROUPS, 1.0, 0.0)], axis=0)
    rr = lax.broadcasted_iota(I32, (2 * tm, 2 * tm), 0)
    cc = lax.broadcasted_iota(I32, (2 * tm, 2 * tm), 1)
    before = _dot(jnp.where(rr > cc, 1.0, 0.0).astype(BF16), oh.astype(BF16))
    rank = jnp.sum((before + cnt_ref[0:1, :]) * oh, axis=1, keepdims=True)
    cnt_ref[...] = cnt_ref[...] + jnp.sum(oh, axis=0, keepdims=True)
    cols = [(i1 - N_GROUPS).astype(F32), (i2 - N_GROUPS).astype(F32), gate1, gate2, rank[:tm], rank[tm:]]
    ri = jnp.zeros(lg.shape, F32)
    for n, col in enumerate(cols):
        ri = jnp.where(lane == n, col, ri)
    ri_ref[...] = ri


def _merge(x2, ya, at, wg, wa, wb, wo, g1, b1, wr, br, tm):
    T = x2.shape[0]
    row = lambda n: pl.BlockSpec((tm, n), lambda i: (i, 0))
    ws = [wg, wa, wb, wo, g1, b1, wr, br]
    return pl.pallas_call(
        _merge_kernel,
        grid=(T // tm,),
        in_specs=[row(D_MODEL), row(HDIM), row(HDIM)] + [_full(w.shape) for w in ws],
        out_specs=[row(D_MODEL), row(LANES), _full((8, LANES))],
        out_shape=[jax.ShapeDtypeStruct((T, D_MODEL), F32), jax.ShapeDtypeStruct((T, LANES), F32),
                   jax.ShapeDtypeStruct((8, LANES), F32)],
        compiler_params=_params(("arbitrary",)),
        name="merge_router",
    )(x2, ya, at, *ws)


def _row_gather(idx_ref, n, src_hbm, dst, sem):
    def issue(r, carry):
        pltpu.make_async_copy(src_hbm.at[pl.ds(idx_ref[0, 0, r], 1)], dst.at[pl.ds(r, 1)], sem).start()
        return carry
    lax.fori_loop(0, n, issue, 0, unroll=8)
    pltpu.make_async_copy(src_hbm.at[pl.ds(0, n)], dst, sem).wait()


def _dispatch_kernel(p0_ref, p1_ref, h_ref, xs_in, xs_hbm, sem):
    del xs_in
    tm = h_ref.shape[0]

    def issue(r, carry):
        src = h_ref.at[pl.ds(r, 1)]
        pltpu.make_async_copy(src, xs_hbm.at[pl.ds(p0_ref[0, 0, r], 1)], sem).start()
        pltpu.make_async_copy(src, xs_hbm.at[pl.ds(p1_ref[0, 0, r], 1)], sem).start()
        return carry
    lax.fori_loop(0, tm, issue, 0, unroll=8)
    for _ in range(2):
        pltpu.make_async_copy(h_ref, xs_hbm.at[pl.ds(0, tm)], sem).wait()


def _dispatch(pos0, pos1, h, n_rows, tm):
    T = h.shape[0]
    idx = pl.BlockSpec((1, 1, tm), lambda i: (i, 0, 0), memory_space=pltpu.SMEM)
    return pl.pallas_call(
        _dispatch_kernel,
        grid=(T // tm,),
        in_specs=[idx, idx, pl.BlockSpec((tm, D_MODEL), lambda i: (i, 0)), pl.BlockSpec(memory_space=pl.ANY)],
        out_specs=pl.BlockSpec(memory_space=pl.ANY),
        out_shape=jax.ShapeDtypeStruct((n_rows, D_MODEL), F32),
        scratch_shapes=[pltpu.SemaphoreType.DMA(())],
        input_output_aliases={3: 0},
        compiler_params=_params(("arbitrary",)),
        name="moe_dispatch",
    )(pos0, pos1, h, jnp.zeros((n_rows, D_MODEL), F32))


def _moe_kernel(te_ref, nu_ref, xs_ref, wg_ref, wu_ref, wd_ref, o_ref):
    t = pl.program_id(0)

    @pl.when(t < nu_ref[0])
    def _():
        xb = xs_ref[...].astype(BF16)
        hg = _dot(xb, wg_ref[0])
        hu = _dot(xb, wu_ref[0])
        act = (hg * _sigmoid(hg)) * hu
        o_ref[...] = _dot(act.astype(BF16), wd_ref[0])

    @pl.when(t >= nu_ref[0])
    def _():
        o_ref[...] = jnp.zeros(o_ref.shape, F32)


def _moe(tile_e, n_used, xs, wg, wu, wd, tm):
    n_tiles = xs.shape[0] // tm
    grid_spec = pltpu.PrefetchScalarGridSpec(
        num_scalar_prefetch=2,
        grid=(n_tiles,),
        in_specs=[pl.BlockSpec((tm, D_MODEL), lambda t, te, nu: (t, 0)),
                  pl.BlockSpec((1, D_MODEL, D_EXPERT), lambda t, te, nu: (te[t], 0, 0)),
                  pl.BlockSpec((1, D_MODEL, D_EXPERT), lambda t, te, nu: (te[t], 0, 0)),
                  pl.BlockSpec((1, D_EXPERT, D_MODEL), lambda t, te, nu: (te[t], 0, 0))],
        out_specs=pl.BlockSpec((tm, D_MODEL), lambda t, te, nu: (t, 0)),
    )
    return pl.pallas_call(
        _moe_kernel,
        grid_spec=grid_spec,
        out_shape=jax.ShapeDtypeStruct((n_tiles * tm, D_MODEL), F32),
        compiler_params=_params(("arbitrary",)),
        name="moe_experts",
    )(tile_e, n_used, xs, wg, wu, wd)


def _final_kernel(p0_ref, p1_ref, h_ref, ri_ref, ys_hbm, g2_ref, b2_ref, o_ref, y0, y1, sem0, sem1):
    tm = y0.shape[0]
    _row_gather(p0_ref, tm, ys_hbm, y0, sem0)
    _row_gather(p1_ref, tm, ys_hbm, y1, sem1)
    ri = ri_ref[...]
    moe = y0[...] * ri[:, 2:3] + y1[...] * ri[:, 3:4]
    o_ref[...] = _layer_norm(ALPHA * h_ref[...] + moe, g2_ref[...], b2_ref[...])


def _final(pos0, pos1, h, ri, ys, g2, b2, tm):
    T = h.shape[0]
    idx = pl.BlockSpec((1, 1, tm), lambda i: (i, 0, 0), memory_space=pltpu.SMEM)
    row = lambda n: pl.BlockSpec((tm, n), lambda i: (i, 0))
    return pl.pallas_call(
        _final_kernel,
        grid=(T // tm,),
        in_specs=[idx, idx, row(D_MODEL), row(LANES), pl.BlockSpec(memory_space=pl.ANY),
                  _full(g2.shape), _full(b2.shape)],
        out_specs=row(D_MODEL),
        out_shape=jax.ShapeDtypeStruct((T, D_MODEL), F32),
        scratch_shapes=[pltpu.VMEM((tm, D_MODEL), F32), pltpu.VMEM((tm, D_MODEL), F32),
                        pltpu.SemaphoreType.DMA(()), pltpu.SemaphoreType.DMA(())],
        compiler_params=_params(("arbitrary",)),
        name="combine_ln",
    )(pos0, pos1, h, ri, ys, g2, b2)


def _routing_tables(eid, rank, sizes, tm):
    T = eid.shape[0]
    n_tiles = (2 * T) // tm + N_EXPERTS
    padded = ((sizes + tm - 1) // tm) * tm
    pad_end = jnp.cumsum(padded)
    pad_off = pad_end - padded
    pos = jnp.take(pad_off, eid) + rank
    tile_start = jnp.arange(n_tiles, dtype=I32) * tm
    tile_e = jnp.minimum(jnp.sum(tile_start[:, None] >= pad_end[None, :], axis=1), N_EXPERTS - 1).astype(I32)
    n_used = (pad_end[-1] // tm).astype(I32).reshape(1)
    return tile_e, n_used, pos, n_tiles * tm


def _block_diag_ones():
    hid = jnp.arange(HDIM, dtype=I32) // HEAD_DIM
    return (hid[:, None] == hid[None, :]).astype(BF16)


def _layer(x, w_in, mu_shift, w0, w_lora_up, a0, a_lora_up, g_lora_up, k_k, k_a, r_k, ln_x_g, ln_x_b,
           w_branch_a, idx_k_ln_g, idx_k_ln_b, rel_bias, w_branch_b, w_out, ln1_g, ln1_b,
           w_router_grp, b_router_grp, w_router_exp, b_router_exp, w_gate, w_up, w_down, ln2_g, ln2_b):
    B, S, _ = x.shape
    T = B * S
    x2 = x.reshape(T, D_MODEL)
    row = lambda t: t.reshape(1, -1)

    c_r, c_wd, c_k, c_v, c_ad, c_gd = 0, 512, 576, 1088, 1600, 1664
    perm = jnp.concatenate([jnp.arange(c_r, c_r + 512), jnp.arange(c_k, c_k + 512), jnp.arange(c_v, c_v + 512),
                            jnp.arange(c_wd, c_wd + 64), jnp.arange(c_ad, c_ad + 64),
                            jnp.arange(c_gd, c_gd + 128)])
    o_q = RWKV_COLS
    o_c = o_q + 4 * HDIM
    o_g = o_c + IDX_DIM + N_HEADS
    wr = w_in[:, :RWKV_COLS][:, perm].astype(BF16)
    wq = w_in[:, o_q:o_c].astype(BF16)
    wc = jnp.pad(w_in[:, o_c:o_g], ((0, 0), (0, LANES - IDX_DIM - N_HEADS))).astype(BF16)
    wgates = w_in[:, o_g:].astype(BF16)
    pad_idx = lambda t: jnp.pad(t, (0, LANES - IDX_DIM)).reshape(1, LANES)

    pr, q, k, ve, vo, qi, kz, kw = _inproj(x2, wr, wq, wc, pad_idx(idx_k_ln_g), pad_idx(idx_k_ln_b), tm=512)

    bd = _block_diag_ones()
    prm = {
        "mu": row(mu_shift[perm]), "w0": row(w0), "a0": row(a0), "k_k": row(k_k), "k_a": row(k_a),
        "r_k": row(r_k),
        "wup": jnp.pad(w_lora_up, ((0, AAA_LORA), (0, 0))),
        "aup": jnp.pad(a_lora_up, ((DECAY_LORA, 0), (0, 0))),
        "gup": g_lora_up, "bd": bd,
    }
    arrs = _rwkv_prep(pr, S, prm, tm=256)
    ya = _rwkv_scan(arrs, row(ln_x_g), row(ln_x_b), bd, B, S)

    at = _attention(q, k, ve, vo, qi, kz, kw, rel_bias, B, S)

    w_router = jnp.pad(jnp.concatenate([w_router_grp, w_router_exp], axis=1),
                       ((0, 0), (0, LANES - N_GROUPS - N_EXPERTS)))
    b_router = jnp.pad(jnp.concatenate([b_router_grp, b_router_exp]), (0, LANES - N_GROUPS - N_EXPERTS))
    h1, ri, cnt = _merge(x2, ya, at, wgates, w_branch_a.astype(BF16), w_branch_b.astype(BF16),
                         w_out.astype(BF16), row(ln1_g), row(ln1_b), w_router, row(b_router), tm=512)

    tm_e = 512
    tile_e, n_used, pos, n_rows = _routing_tables(ri[:, 0:2].astype(I32), ri[:, 4:6].astype(I32),
                                                  cnt[0, :N_EXPERTS].astype(I32), tm_e)
    tm_f = 256
    pos0 = pos[:, 0].reshape(T // tm_f, 1, tm_f)
    pos1 = pos[:, 1].reshape(T // tm_f, 1, tm_f)
    xs = _dispatch(pos0, pos1, h1, n_rows, tm_f)
    ys = _moe(tile_e, n_used, xs, w_gate.astype(BF16), w_up.astype(BF16), w_down.astype(BF16), tm_e)
    out = _final(pos0, pos1, h1, ri, ys, row(ln2_g), row(ln2_b), tm_f)
    return out.reshape(B, S, D_MODEL)


def kernel(x, w_in, mu_shift, w0, w_lora_up, a0, a_lora_up, g_lora_up, k_k, k_a, r_k, ln_x_g, ln_x_b, w_branch_a, idx_k_ln_g, idx_k_ln_b, rel_bias, w_branch_b, w_out, ln1_g, ln1_b, w_router_grp, b_router_grp, w_router_exp, b_router_exp, w_expert_gate, w_expert_up, w_expert_down, ln2_g, ln2_b):
    assert w_in.shape[0] == 1, "single-layer (DEPTH = 1) block"
    l = 0
    return _layer(x, w_in[l], mu_shift[l], w0[l], w_lora_up[l], a0[l], a_lora_up[l], g_lora_up[l], k_k[l],
                  k_a[l], r_k[l], ln_x_g[l], ln_x_b[l], w_branch_a[l], idx_k_ln_g[l], idx_k_ln_b[l], rel_bias,
                  w_branch_b[l], w_out[l], ln1_g[l], ln1_b[l], w_router_grp[l], b_router_grp[l],
                  w_router_exp[l], b_router_exp[l], w_expert_gate[l], w_expert_up[l], w_expert_down[l],
                  ln2_g[l], ln2_b[l])
```

```python
import functools
import math

import jax
import jax.numpy as jnp
from jax import lax
from jax.experimental import pallas as pl
from jax.experimental.pallas import tpu as pltpu

F32 = jnp.float32
BF16 = jnp.bfloat16
I32 = jnp.int32
I16 = jnp.int16

D_MODEL = 1024
HEAD_DIM = 64
N_HEADS = 8
HDIM = N_HEADS * HEAD_DIM
DECAY_LORA = 64
AAA_LORA = 64
GATE_LORA = 128
RWKV_COLS = 3 * HDIM + DECAY_LORA + AAA_LORA + GATE_LORA
IDX_DIM = 64
MAX_TOPK = 256
N_BUCKETS = 32
MAX_EXACT = 16
MAX_DISTANCE = 128
N_GROUPS = 4
EXPERTS_PER_GROUP = 8
N_EXPERTS = 32
D_EXPERT = 512
GN_EPS = 64e-5
LN_EPS = 1e-5
ALPHA = 2.0 ** 0.25
LANES = 128
CHUNK = 64
SCAN_SUB = 2
QB = 128
KCH = 512
IQB = 512
INT_MIN = -(2 ** 31)
MIN16 = -(2 ** 15)
NEG = -1e30
LOG2E = 1.4426950408889634
VMEM_LIMIT = 56 * 1024 * 1024

NN = (((1,), (0,)), ((), ()))
NT = (((1,), (1,)), ((), ()))


def _dot(a, b, dims=NN):
    return lax.dot_general(a, b, dims, preferred_element_type=F32)


def _split2(x):
    hi = x.astype(BF16)
    lo = (x - hi.astype(F32)).astype(BF16)
    return hi, lo


def _split3(x):
    hi = x.astype(BF16)
    r1 = x - hi.astype(F32)
    mid = r1.astype(BF16)
    lo = (r1 - mid.astype(F32)).astype(BF16)
    return hi, mid, lo


def _mm1(a, b, dims=NN):
    return _dot(a.astype(BF16), b.astype(BF16), dims)


def _mm3(a, b, dims=NN):
    ah, al = _split2(a)
    bh, bl = _split2(b)
    return _dot(ah, bh, dims) + (_dot(ah, bl, dims) + _dot(al, bh, dims))


_mm_misc = _mm1
_mm_inv = _mm1
_mm_state = _mm3


def _mm_exact_lhs(a_bf, b):
    b0, b1, b2 = _split3(b)
    return _dot(a_bf, b0) + (_dot(a_bf, b1) + _dot(a_bf, b2))


def _mm_exact_rhs(a, b_bf):
    a0, a1, a2 = _split3(a)
    return _dot(a0, b_bf) + (_dot(a1, b_bf) + _dot(a2, b_bf))


def _params(sem, vmem=VMEM_LIMIT):
    return pltpu.CompilerParams(dimension_semantics=sem, vmem_limit_bytes=vmem)


def _full(shape):
    nd = len(shape)
    return pl.BlockSpec(shape, lambda *_: (0,) * nd)


def _inproj_kernel(x_ref, wr_ref, wq_ref, wc_ref, lng_ref, lnb_ref,
                   pr_ref, q_ref, k_ref, ve_ref, vo_ref, qi_ref, kz_ref, kw_ref):
    xb = x_ref[...].astype(BF16)
    pr_ref[...] = _dot(xb, wr_ref[...])
    qkv = _dot(xb, wq_ref[...])
    q_ref[...] = (qkv[:, 0:HDIM] * (HEAD_DIM ** -0.5 * LOG2E)).astype(BF16)
    k_ref[...] = qkv[:, HDIM:2 * HDIM].astype(BF16)
    v = qkv[:, 2 * HDIM:3 * HDIM]
    even = (lax.broadcasted_iota(I32, v.shape, 1) % LANES) < HEAD_DIM
    ve_ref[...] = jnp.where(even, v, 1.0).astype(BF16)
    vo_ref[...] = jnp.where(even, 1.0, v).astype(BF16)
    qi_ref[...] = qkv[:, 3 * HDIM:4 * HDIM].astype(BF16)
    c = _dot(xb, wc_ref[...])
    kw_ref[...] = c
    lane = lax.broadcasted_iota(I32, c.shape, 1)
    isk = lane < IDX_DIM
    mu = jnp.sum(jnp.where(isk, c, 0.0), axis=1, keepdims=True) * (1.0 / IDX_DIM)
    d = jnp.where(isk, c - mu, 0.0)
    var = jnp.sum(d * d, axis=1, keepdims=True) * (1.0 / IDX_DIM)
    kn = d * lax.rsqrt(var + LN_EPS) * lng_ref[...] + lnb_ref[...]
    kz_ref[:, 0:LANES] = kn.astype(BF16)
    kz_ref[:, LANES:2 * LANES] = pltpu.roll(kn, IDX_DIM, 1).astype(BF16)


def _inproj(x2, wr, wq, wc, lng, lnb, tm):
    T = x2.shape[0]
    row = lambda n: pl.BlockSpec((tm, n), lambda i: (i, 0))
    return pl.pallas_call(
        _inproj_kernel,
        grid=(T // tm,),
        in_specs=[row(D_MODEL), _full(wr.shape), _full(wq.shape), _full(wc.shape),
                  _full(lng.shape), _full(lnb.shape)],
        out_specs=[row(RWKV_COLS), row(HDIM), row(HDIM), row(HDIM), row(HDIM), row(HDIM),
                   row(2 * LANES), row(LANES)],
        out_shape=[jax.ShapeDtypeStruct((T, RWKV_COLS), F32)]
        + [jax.ShapeDtypeStruct((T, HDIM), BF16)] * 5
        + [jax.ShapeDtypeStruct((T, 2 * LANES), BF16), jax.ShapeDtypeStruct((T, LANES), F32)],
        compiler_params=_params(("parallel",)),
        name="inproj",
    )(x2, wr, wq, wc, lng, lnb)


def _softplus(x):
    return jnp.maximum(x, 0.0) + jnp.log(1.0 + jnp.exp(-jnp.abs(x)))


def _sigmoid(x):
    return 1.0 / (1.0 + jnp.exp(-x))


def _prep_kernel(tiles_per_seq, p_ref, pp_ref, mu_ref, w0_ref, a0_ref, kk_ref, ka_ref, rk_ref,
                 wup_ref, aup_ref, gup_ref, bd_ref,
                 r_ref, lw_ref, k_ref, v_ref, a_ref, b_ref, g_ref, bon_ref):
    i = pl.program_id(0)
    p = p_ref[...]
    tm = p.shape[0]
    first = (i % tiles_per_seq) == 0
    prow = jnp.where(first, 0.0, pp_ref[7:8, :])
    rowid = lax.broadcasted_iota(I32, p.shape, 0)
    prev = jnp.where(rowid == 0, prow, pltpu.roll(p, 1, 0))
    ps = p + (prev - p) * mu_ref[...]
    r = ps[:, 0:HDIM]
    k = ps[:, HDIM:2 * HDIM]
    v = ps[:, 2 * HDIM:3 * HDIM]
    da = ps[:, 3 * HDIM:3 * HDIM + LANES]
    gd = ps[:, 3 * HDIM + LANES:3 * HDIM + 2 * LANES]
    w = -_softplus(-(w0_ref[...] + _mm3(jnp.tanh(da), wup_ref[...]))) - 0.5
    lw_ref[...] = -jnp.exp(w)
    a = _sigmoid(a0_ref[...] + _mm3(da, aup_ref[...]))
    g_ref[...] = _mm3(_sigmoid(gd), gup_ref[...])
    bd = bd_ref[...]
    kk = k * kk_ref[...]
    ss = _mm_exact_rhs(kk * kk, bd)
    kk = kk / jnp.maximum(jnp.sqrt(ss), 1e-12)
    k2 = k * (1.0 + (a - 1.0) * ka_ref[...])
    r_ref[...] = r
    k_ref[...] = k2
    v_ref[...] = v
    a_ref[...] = -kk
    b_ref[...] = kk * a
    bon_ref[...] = _mm_exact_rhs(r * k2 * rk_ref[...], bd) * v


def _rwkv_prep(pr, S, prm, tm):
    T = pr.shape[0]
    row = lambda n: pl.BlockSpec((tm, n), lambda i: (i, 0))
    prev = pl.BlockSpec((8, RWKV_COLS), lambda i: (jnp.maximum(i * (tm // 8) - 1, 0), 0))
    names = ["mu", "w0", "a0", "k_k", "k_a", "r_k", "wup", "aup", "gup", "bd"]
    return pl.pallas_call(
        functools.partial(_prep_kernel, S // tm),
        grid=(T // tm,),
        in_specs=[row(RWKV_COLS), prev] + [_full(prm[n].shape) for n in names],
        out_specs=[row(HDIM)] * 8,
        out_shape=[jax.ShapeDtypeStruct((T, HDIM), F32)] * 8,
        compiler_params=_params(("parallel",)),
        name="rwkv_prep",
    )(pr, pr, *[prm[n] for n in names])


def _scan_kernel(r_ref, lw_ref, k_ref, v_ref, a_ref, b_ref, g_ref, bon_ref, lng_ref, lnb_ref, bd_ref,
                 o_ref, st_ref, y_ref):
    C, N, H = CHUNK, HEAD_DIM, N_HEADS

    @pl.when(pl.program_id(1) == 0)
    def _():
        st_ref[...] = jnp.zeros(st_ref.shape, F32)

    ri = lax.broadcasted_iota(I32, (C, C), 0)
    ci = lax.broadcasted_iota(I32, (C, C), 1)
    incl = ri >= ci
    strict = ri > ci
    eye = ri == ci
    eye_f = jnp.where(eye, 1.0, 0.0)
    lmat = jnp.where(incl, 1.0, 0.0).astype(BF16)
    sls = [slice(h * N, (h + 1) * N) for h in range(H)]
    units = [(s, h) for s in range(SCAN_SUB) for h in range(H)]
    ah, rh, vh, bT, kT, bhT, khT, gam = {}, {}, {}, {}, {}, {}, {}, {}
    for s in range(SCAN_SUB):
        rows = slice(s * C, (s + 1) * C)
        lw = lw_ref[rows, :]
        cum = _mm_exact_lhs(lmat, lw)
        last = cum[C - 1:C, :]
        e_i = jnp.exp(-cum)
        e_end = jnp.exp(last - cum)
        g_s = jnp.exp(last)
        r_t = r_ref[rows, :] * jnp.exp(cum)
        a_t = a_ref[rows, :] * jnp.exp(cum - lw)
        v_s = v_ref[rows, :]
        b_s, k_s = b_ref[rows, :], k_ref[rows, :]
        bT_s, kT_s = (b_s * e_i).T, (k_s * e_i).T
        bhT_s, khT_s = (b_s * e_end).T, (k_s * e_end).T
        for h in range(H):
            un = (s, h)
            ah[un], rh[un], vh[un] = a_t[:, sls[h]], r_t[:, sls[h]], v_s[:, sls[h]]
            bT[un], kT[un] = bT_s[sls[h], :], kT_s[sls[h], :]
            bhT[un], khT[un] = bhT_s[sls[h], :], khT_s[sls[h], :]
            gam[un] = g_s[:, sls[h]]

    ar = {un: jnp.concatenate([ah[un], rh[un]], axis=0) for un in units}
    sb = {un: _mm_misc(ar[un], bT[un]) for un in units}
    sk = {un: _mm_misc(ar[un], kT[un]) for un in units}
    a_ab = {un: jnp.where(strict, sb[un][:C], 0.0) for un in units}
    a_rb = {un: jnp.where(incl, sb[un][C:], 0.0) for un in units}
    a_ak = {un: jnp.where(strict, sk[un][:C], 0.0) for un in units}
    a_rk = {un: jnp.where(incl, sk[un][C:], 0.0) for un in units}
    u = {un: _mm_misc(a_ak[un], vh[un]) for un in units}
    tinv = {un: eye_f + a_ab[un] for un in units}
    xp = a_ab
    for _ in range(5):
        xp = {un: _mm_inv(xp[un], xp[un]) for un in units}
        tinv = {un: tinv[un] + _mm_inv(tinv[un], xp[un]) for un in units}
    pm = {un: _mm_misc(tinv[un], ah[un]) for un in units}
    qm = {un: _mm_misc(tinv[un], u[un]) for un in units}
    r2 = {un: rh[un] + _mm_misc(a_rb[un], pm[un]) for un in units}
    mmat = {un: jnp.where(eye, gam[un], 0.0) + _mm_misc(bhT[un], pm[un]) for un in units}
    y0 = {un: _mm_misc(a_rb[un], qm[un]) + _mm_misc(a_rk[un], vh[un]) for un in units}
    gmat = {un: _mm_misc(bhT[un], qm[un]) + _mm_misc(khT[un], vh[un]) for un in units}
    st = [st_ref[h] for h in range(H)]
    for s in range(SCAN_SUB):
        for h in range(H):
            y_ref[s * C:(s + 1) * C, sls[h]] = _mm_state(r2[(s, h)], st[h]) + y0[(s, h)]
        st = [_mm_state(mmat[(s, h)], st[h]) + gmat[(s, h)] for h in range(H)]
    for h in range(H):
        st_ref[h] = st[h]

    y = y_ref[...]
    bd = bd_ref[...]
    mu = _mm_exact_rhs(y, bd) * (1.0 / N)
    d = y - mu
    var = _mm_exact_rhs(d * d, bd) * (1.0 / N)
    yn = d * lax.rsqrt(var + GN_EPS) * lng_ref[...] + lnb_ref[...]
    o_ref[...] = ((yn + bon_ref[...]) * g_ref[...]).astype(BF16)


def _rwkv_scan(arrs, lng, lnb, bd, B, S):
    rows = SCAN_SUB * CHUNK
    nc = S // rows
    row = pl.BlockSpec((rows, HDIM), lambda b, c: (b * nc + c, 0))
    return pl.pallas_call(
        _scan_kernel,
        grid=(B, nc),
        in_specs=[row] * 8 + [_full(lng.shape), _full(lnb.shape), _full(bd.shape)],
        out_specs=row,
        out_shape=jax.ShapeDtypeStruct((B * S, HDIM), BF16),
        scratch_shapes=[pltpu.VMEM((N_HEADS, HEAD_DIM, HEAD_DIM), F32), pltpu.VMEM((rows, HDIM), F32)],
        compiler_params=_params(("parallel", "arbitrary")),
        name="rwkv_scan",
    )(*arrs, lng, lnb, bd)


def _sort_key(x):
    bits = pltpu.bitcast(x, I32)
    return bits ^ ((bits >> 31) & 0x7FFFFFFF)


def _index_kernel(top_k, qi_ref, kz_ref, kw_ref, mask_ref, hi16, lo16, wb):
    ib = pl.program_id(1)
    t0 = ib * IQB
    n_ch = ib + 1
    nrb, nsub = IQB // QB, KCH // LANES
    lane = lax.broadcasted_iota(I32, (QB, LANES), 1)
    rowi = lax.broadcasted_iota(I32, (QB, LANES), 0)
    blocks = [slice(rb * QB, (rb + 1) * QB) for rb in range(nrb)]

    def tile(c0, s):
        return pl.ds(pl.multiple_of(c0 + s * LANES, LANES), LANES)

    kw = kw_ref[...]
    for h in range(N_HEADS):
        wb[h] = jnp.broadcast_to(kw[:, IDX_DIM + h:IDX_DIM + h + 1] * (N_HEADS ** -0.5), (IQB, LANES)) \
            * (IDX_DIM ** -0.5)

    def score_chunk(c, carry):
        c0 = pl.multiple_of(c * KCH, KCH)
        for rb, rows in enumerate(blocks):
            tot = [jnp.zeros((QB, LANES), F32) for _ in range(nsub)]
            for p in range(N_HEADS // 2):
                qp = qi_ref[rows, p * LANES:(p + 1) * LANES]
                for e in range(2):
                    z = _dot(qp, kz_ref[pl.ds(c0, KCH), e * LANES:(e + 1) * LANES], NT)
                    wbh = wb[2 * p + e, rows, :]
                    for s in range(nsub):
                        tot[s] = tot[s] + jnp.maximum(z[:, s * LANES:(s + 1) * LANES], 0.0) * wbh
            for s in range(nsub):
                causal = c0 + s * LANES + lane <= t0 + rb * QB + rowi
                key = jnp.where(causal, _sort_key(tot[s]), INT_MIN)
                hi16[rows, tile(c0, s)] = (key >> 16).astype(I16)
                lo16[rows, tile(c0, s)] = ((key & 0xFFFF) - 32768).astype(I16)
        return carry
    lax.fori_loop(0, n_ch, score_chunk, 0)

    ones_mat = jnp.ones((LANES, LANES), BF16)
    one_bf, zero_bf, neg_bf = jnp.ones((), BF16), jnp.zeros((), BF16), jnp.full((), NEG, BF16)
    per_block = lambda f: [f(rb) for rb in range(nrb)]

    def count_ge(arr, cand):
        cands = [c.astype(I16) for c in cand]

        def body(c, accs):
            c0 = pl.multiple_of(c * KCH, KCH)
            out = []
            for rb, rows in enumerate(blocks):
                a = accs[rb]
                for s in range(nsub):
                    a = a + jnp.where(arr[rows, tile(c0, s)] >= cands[rb], one_bf, zero_bf)
                out.append(a)
            return tuple(out)
        accs = lax.fori_loop(0, n_ch, body, tuple(jnp.zeros((QB, LANES), BF16) for _ in blocks))
        return [_dot(a, ones_mat) for a in accs]

    def count_gt(arr, v):
        cnt = count_ge(arr, [jnp.minimum(x + 1, 32767) for x in v])
        return [jnp.where(x >= 32767, 0.0, c) for x, c in zip(v, cnt)]

    few = per_block(lambda rb: t0 + rb * QB + rowi < top_k)

    def search16(arr, want, all_count, unchecked_groups):
        def step(bit, carry):
            u, cu = carry
            cand = [x | (jnp.int32(1) << (15 - bit)) for x in u]
            cnt = count_ge(arr, [x - 32768 for x in cand])
            ok = [c >= w for c, w in zip(cnt, want)]
            return (tuple(jnp.where(o, x, y) for o, x, y in zip(ok, cand, u)),
                    tuple(jnp.where(o, x, y) for o, x, y in zip(ok, cnt, cu)))

        def unsettled(carry):
            g, _, cu = carry
            done = [jnp.where(f | (c == w), 1, 0) for f, c, w in zip(few, cu, want)]
            return (g < 4) & (jnp.min(functools.reduce(jnp.minimum, done)) == 0)

        def four_bits(carry):
            g, u, cu = carry
            u, cu = lax.fori_loop(4 * g, 4 * g + 4, step, (u, cu))
            return g + 1, u, cu
        zeros = tuple(jnp.zeros((QB, LANES), I32) for _ in blocks)
        u, cu = lax.fori_loop(0, 4 * unchecked_groups, step, (zeros, tuple(all_count)))
        if unchecked_groups < 4:
            _, u, cu = lax.while_loop(unsettled, four_bits, (jnp.int32(unchecked_groups), u, cu))
        return u, cu

    want = per_block(lambda rb: jnp.full((QB, LANES), top_k, F32))
    u1, c1 = search16(hi16, want, per_block(lambda rb: jnp.full((QB, LANES), n_ch * KCH, F32)), 4)
    p_hi = [x - 32768 for x in u1]
    above = count_gt(hi16, p_hi)
    want2 = [w - a for w, a in zip(want, above)]
    p16 = [x.astype(I16) for x in p_hi]

    def keep_low(c, carry):
        c0 = pl.multiple_of(c * KCH, KCH)
        for rb, rows in enumerate(blocks):
            for s in range(nsub):
                sl = tile(c0, s)
                lo16[rows, sl] = jnp.where(hi16[rows, sl] == p16[rb], lo16[rows, sl], jnp.full((), MIN16, I16))
        return carry
    lax.fori_loop(0, n_ch, keep_low, 0)

    u2, c2 = search16(lo16, want2, [c - a for c, a in zip(c1, above)], 2)
    p_lo = [jnp.maximum(x - 32768, jnp.where(ph == MIN16, MIN16 + 1, MIN16)) for x, ph in zip(u2, p_hi)]
    extra = [jnp.where(f, 0.0, c - w) for f, c, w in zip(few, c2, want2)]
    has_ties = jnp.max(functools.reduce(jnp.maximum, extra)) > 0.0

    @pl.when(jnp.logical_not(has_ties))
    def _():
        l16 = [x.astype(I16) for x in p_lo]

        def body(c, carry):
            c0 = pl.multiple_of(c * KCH, KCH)
            for rb, rows in enumerate(blocks):
                for s in range(nsub):
                    sl = tile(c0, s)
                    h, l = hi16[rows, sl], lo16[rows, sl]
                    sel = (h > p16[rb]) | ((h == p16[rb]) & (l >= l16[rb]))
                    mask_ref[rows, sl] = jnp.where(sel, zero_bf, neg_bf)
            return carry
        lax.fori_loop(0, n_ch, body, 0)

    @pl.when(has_ties)
    def _():
        budget = [w - c for w, c in zip(want2, count_gt(lo16, p_lo))]
        ut = jnp.where(rowi <= lane, 1.0, 0.0).astype(BF16)
        for rb, rows in enumerate(blocks):
            ph, plo, bud = p_hi[rb], p_lo[rb], budget[rb]

            def body(j, before):
                sl = pl.ds(pl.multiple_of(j * LANES, LANES), LANES)
                h, l = hi16[rows, sl].astype(I32), lo16[rows, sl].astype(I32)
                eq = (h == ph) & (l == plo)
                eqf = jnp.where(eq, 1.0, 0.0)
                eqb = eqf.astype(BF16)
                rank = before + _dot(eqb, ut) - eqf
                sel = (h > ph) | ((h == ph) & (l > plo)) | (eq & (rank < bud))
                mask_ref[rows, sl] = jnp.where(sel, 0.0, NEG).astype(BF16)
                return before + _dot(eqb, ones_mat)
            lax.fori_loop(0, n_ch * nsub, body, jnp.zeros((QB, LANES), F32))

    def tail(c, carry):
        mask_ref[:, pl.ds(pl.multiple_of(c * KCH, KCH), KCH)] = jnp.full((IQB, KCH), NEG, BF16)
        return carry
    lax.fori_loop(n_ch, mask_ref.shape[1] // KCH, tail, 0)


def _indexer(qi, kz, kw, B, S):
    nb = S // IQB
    top_k = min(MAX_TOPK, S // 4)
    rows = lambda n: pl.BlockSpec((IQB, n), lambda b, i: (b * nb + i, 0))
    return pl.pallas_call(
        functools.partial(_index_kernel, top_k),
        grid=(B, nb),
        in_specs=[rows(HDIM), pl.BlockSpec((S, 2 * LANES), lambda b, i: (b, 0)), rows(LANES)],
        out_specs=rows(S),
        out_shape=jax.ShapeDtypeStruct((B * S, S), BF16),
        scratch_shapes=[pltpu.VMEM((IQB, S), I16),
                        pltpu.VMEM((IQB, S), I16),
                        pltpu.VMEM((N_HEADS, IQB, LANES), F32)],
        compiler_params=_params(("parallel", "arbitrary")),
        name="dsa_indexer",
    )(qi, kz, kw)


def _attn_kernel(rb_ref, q_ref, k_ref, ve_ref, vo_ref, mask_ref, o_ref,
                 mbias, btab, qm, s_scr, p_scr, *state):
    m_scr, acc, a_scr = (state[n * N_HEADS:(n + 1) * N_HEADS] for n in range(3))
    i = pl.program_id(1)
    lane = lax.broadcasted_iota(I32, (QB, LANES), 1)
    rowi = lax.broadcasted_iota(I32, (QB, LANES), 0)

    @pl.when(i == 0)
    def _build_bias():
        for h in range(N_HEADS):
            btab[h, 2] = jnp.zeros((QB, LANES), F32)
        for m in range(2):
            n = jnp.maximum(m * QB + rowi - lane, 0)
            nf = jnp.maximum(n, 1).astype(F32)
            large = MAX_EXACT + (jnp.log(nf / MAX_EXACT) / math.log(MAX_DISTANCE / MAX_EXACT)
                                 * (N_BUCKETS - MAX_EXACT)).astype(I32)
            bucket = jnp.where(n < MAX_EXACT, n, jnp.minimum(large, N_BUCKETS - 1))
            for h in range(N_HEADS):
                t = jnp.zeros((QB, LANES), F32)
                for bk in range(N_BUCKETS):
                    t = jnp.where(bucket == bk, rb_ref[bk, h], t)
                btab[h, m] = (t - rb_ref[N_BUCKETS - 1, h]) * LOG2E

    n_ch = i // (KCH // QB) + 1

    q = q_ref[...]
    even = lane < HEAD_DIM
    for p in range(N_HEADS // 2):
        qp = q[:, p * LANES:(p + 1) * LANES]
        qm[2 * p] = jnp.where(even, qp, jnp.zeros_like(qp))
        qm[2 * p + 1] = jnp.where(even, jnp.zeros_like(qp), qp)
    for h in range(N_HEADS):
        m_scr[h][...] = jnp.full((QB, LANES), NEG, F32)
        acc[h][...] = jnp.zeros((QB, LANES), F32)

    def key_block(c, near):
        start = pl.multiple_of(c * KCH, KCH)
        rows = pl.ds(start, KCH)
        nsub = KCH // LANES
        for h in range(N_HEADS):
            cols = slice((h // 2) * LANES, (h // 2 + 1) * LANES)
            s_scr[h] = _dot(qm[h], k_ref[rows, cols], NT)
        mbias[...] = mask_ref[:, rows].astype(F32)
        for h in range(N_HEADS):
            sub = []
            for n in range(nsub):
                sc = s_scr[h, :, n * LANES:(n + 1) * LANES] + mbias[:, n * LANES:(n + 1) * LANES]
                if near:
                    sc = sc + btab[h, jnp.clip(i - (c * nsub + n), 0, 2)]
                sub.append(sc)
            mx = jnp.maximum(jnp.maximum(sub[0], sub[1]), jnp.maximum(sub[2], sub[3]))
            m_old = m_scr[h][...]
            m_new = jnp.maximum(m_old, jnp.max(mx, axis=1, keepdims=True))
            for n in range(nsub):
                p_scr[h, :, n * LANES:(n + 1) * LANES] = jnp.exp2(sub[n] - m_new).astype(BF16)
            a_scr[h][...] = jnp.exp2(m_old - m_new)
            m_scr[h][...] = m_new
        for h in range(N_HEADS):
            cols = slice((h // 2) * LANES, (h // 2 + 1) * LANES)
            v_ref = vo_ref if h % 2 else ve_ref
            acc[h][...] = a_scr[h][...] * acc[h][...] + _dot(p_scr[h], v_ref[rows, cols])

    n_far = jnp.maximum(i - 1, 0) // (KCH // QB)

    def far_body(c, carry):
        key_block(c, False)
        return carry
    lax.fori_loop(0, n_far, far_body, 0)

    def near_body(c, carry):
        key_block(c, True)
        return carry
    lax.fori_loop(n_far, n_ch, near_body, 0)

    for p in range(N_HEADS // 2):
        ae, ao = acc[2 * p][...], acc[2 * p + 1][...]
        oe = ae / pltpu.roll(ae, HEAD_DIM, 1)
        oo = ao / pltpu.roll(ao, HEAD_DIM, 1)
        o_ref[:, p * LANES:(p + 1) * LANES] = jnp.where(even, oe, oo).astype(BF16)


def _attention(q, k, ve, vo, mask, rel_bias, B, S):
    nq = S // QB
    qrow = lambda n: pl.BlockSpec((QB, n), lambda b, i: (b * nq + i, 0))
    seq = lambda n: pl.BlockSpec((S, n), lambda b, i: (b, 0))
    return pl.pallas_call(
        _attn_kernel,
        grid=(B, nq),
        in_specs=[pl.BlockSpec(memory_space=pltpu.SMEM), qrow(HDIM), seq(HDIM), seq(HDIM), seq(HDIM), qrow(S)],
        out_specs=qrow(HDIM),
        out_shape=jax.ShapeDtypeStruct((B * S, HDIM), BF16),
        scratch_shapes=[pltpu.VMEM((QB, KCH), F32),
                        pltpu.VMEM((N_HEADS, 3, QB, LANES), F32),
                        pltpu.VMEM((N_HEADS, QB, LANES), BF16),
                        pltpu.VMEM((N_HEADS, QB, KCH), F32),
                        pltpu.VMEM((N_HEADS, QB, KCH), BF16)]
        + [pltpu.VMEM((QB, LANES), F32)] * (3 * N_HEADS),
        compiler_params=_params(("parallel", "arbitrary")),
        name="dsa_attention",
    )(rel_bias, q, k, ve, vo, mask)


def _layer_norm(x, g, b):
    mu = jnp.mean(x, axis=1, keepdims=True)
    d = x - mu
    var = jnp.mean(d * d, axis=1, keepdims=True)
    return d * lax.rsqrt(var + LN_EPS) * g + b


def _merge_kernel(x_ref, ya_ref, at_ref, wg_ref, wa_ref, wb_ref, wo_ref, g1_ref, b1_ref, wr_ref, br_ref,
                  h_ref, ri_ref, cnt_ref):
    @pl.when(pl.program_id(0) == 0)
    def _():
        cnt_ref[...] = jnp.zeros(cnt_ref.shape, F32)
    half = x_ref.shape[0] // 2
    halves = [slice(sb * half, (sb + 1) * half) for sb in range(2)]
    pre = []
    for rows in halves:
        g = _dot(x_ref[rows, :].astype(BF16), wg_ref[...])
        pre.append((g, _dot(ya_ref[rows, :], wa_ref[...]), _dot(at_ref[rows, :], wb_ref[...])))
    mixes = []
    for g, ya, yb in pre:
        mixin = _sigmoid(g[:, :D_MODEL]) * ya + _sigmoid(g[:, D_MODEL:]) * yb
        mixes.append(_dot(mixin.astype(BF16), wo_ref[...]))
    for rows, mix in zip(halves, mixes):
        h_ref[rows, :] = _layer_norm(ALPHA * x_ref[rows, :] + mix, g1_ref[...], b1_ref[...])
    for rows in halves:
        _route_rows(h_ref.at[rows], wr_ref, br_ref, ri_ref.at[rows], cnt_ref)


def _route_rows(h_ref, wr_ref, br_ref, ri_ref, cnt_ref):
    lg = _mm3(h_ref[...], wr_ref[...]) + br_ref[...]
    lane = lax.broadcasted_iota(I32, lg.shape, 1)
    gl = jnp.where(lane < N_GROUPS, lg, NEG)
    gmax = jnp.max(gl, axis=1, keepdims=True)
    p_g = 1.0 / jnp.sum(jnp.exp(gl - gmax), axis=1, keepdims=True)
    gsel = jnp.min(jnp.where(gl == gmax, lane, LANES), axis=1, keepdims=True)
    lo = N_GROUPS + EXPERTS_PER_GROUP * gsel
    el = jnp.where((lane >= lo) & (lane < lo + EXPERTS_PER_GROUP), lg, NEG)
    e1 = jnp.max(el, axis=1, keepdims=True)
    i1 = jnp.min(jnp.where(el == e1, lane, LANES), axis=1, keepdims=True)
    el2 = jnp.where(lane == i1, NEG, el)
    e2 = jnp.max(el2, axis=1, keepdims=True)
    i2 = jnp.min(jnp.where(el2 == e2, lane, LANES), axis=1, keepdims=True)
    w2 = jnp.exp(e2 - e1)
    gate1 = p_g / (1.0 + w2)
    gate2 = p_g * w2 / (1.0 + w2)
    tm = lg.shape[0]
    oh = jnp.concatenate([jnp.where(lane == i1 - N_GROUPS, 1.0, 0.0),
                          jnp.where(lane == i2 - N_GROUPS, 1.0, 0.0)], axis=0)
    rr = lax.broadcasted_iota(I32, (2 * tm, 2 * tm), 0)
    cc = lax.broadcasted_iota(I32, (2 * tm, 2 * tm), 1)
    before = _dot(jnp.where(rr > cc, 1.0, 0.0).astype(BF16), oh.astype(BF16))
    rank = jnp.sum((before + cnt_ref[0:1, :]) * oh, axis=1, keepdims=True)
    cnt_ref[...] = cnt_ref[...] + jnp.sum(oh, axis=0, keepdims=True)
    cols = [(i1 - N_GROUPS).astype(F32), (i2 - N_GROUPS).astype(F32), gate1, gate2, rank[:tm], rank[tm:]]
    ri = jnp.zeros(lg.shape, F32)
    for n, col in enumerate(cols):
        ri = jnp.where(lane == n, col, ri)
    ri_ref[...] = ri


def _merge(x2, ya, at, wg, wa, wb, wo, g1, b1, wr, br, tm):
    T = x2.shape[0]
    row = lambda n: pl.BlockSpec((tm, n), lambda i: (i, 0))
    ws = [wg, wa, wb, wo, g1, b1, wr, br]
    return pl.pallas_call(
        _merge_kernel,
        grid=(T // tm,),
        in_specs=[row(D_MODEL), row(HDIM), row(HDIM)] + [_full(w.shape) for w in ws],
        out_specs=[row(D_MODEL), row(LANES), _full((8, LANES))],
        out_shape=[jax.ShapeDtypeStruct((T, D_MODEL), F32), jax.ShapeDtypeStruct((T, LANES), F32),
                   jax.ShapeDtypeStruct((8, LANES), F32)],
        compiler_params=_params(("arbitrary",)),
        name="merge_router",
    )(x2, ya, at, *ws)


def _row_gather(idx_ref, n, src_hbm, dst, sem):
    def issue(r, carry):
        pltpu.make_async_copy(src_hbm.at[pl.ds(idx_ref[0, 0, r], 1)], dst.at[pl.ds(r, 1)], sem).start()
        return carry
    lax.fori_loop(0, n, issue, 0, unroll=8)
    pltpu.make_async_copy(src_hbm.at[pl.ds(0, n)], dst, sem).wait()


def _dispatch_kernel(p0_ref, p1_ref, h_ref, xs_in, xs_hbm, sem):
    del xs_in
    tm = h_ref.shape[0]

    def issue(r, carry):
        src = h_ref.at[pl.ds(r, 1)]
        pltpu.make_async_copy(src, xs_hbm.at[pl.ds(p0_ref[0, 0, r], 1)], sem).start()
        pltpu.make_async_copy(src, xs_hbm.at[pl.ds(p1_ref[0, 0, r], 1)], sem).start()
        return carry
    lax.fori_loop(0, tm, issue, 0, unroll=8)
    for _ in range(2):
        pltpu.make_async_copy(h_ref, xs_hbm.at[pl.ds(0, tm)], sem).wait()


def _dispatch(pos0, pos1, h, n_rows, tm):
    T = h.shape[0]
    idx = pl.BlockSpec((1, 1, tm), lambda i: (i, 0, 0), memory_space=pltpu.SMEM)
    return pl.pallas_call(
        _dispatch_kernel,
        grid=(T // tm,),
        in_specs=[idx, idx, pl.BlockSpec((tm, D_MODEL), lambda i: (i, 0)), pl.BlockSpec(memory_space=pl.ANY)],
        out_specs=pl.BlockSpec(memory_space=pl.ANY),
        out_shape=jax.ShapeDtypeStruct((n_rows, D_MODEL), F32),
        scratch_shapes=[pltpu.SemaphoreType.DMA(())],
        input_output_aliases={3: 0},
        compiler_params=_params(("arbitrary",)),
        name="moe_dispatch",
    )(pos0, pos1, h, jnp.zeros((n_rows, D_MODEL), F32))


def _moe_kernel(te_ref, nu_ref, xs_ref, wg_ref, wu_ref, wd_ref, o_ref):
    t = pl.program_id(0)

    @pl.when(t < nu_ref[0])
    def _():
        xb = xs_ref[...].astype(BF16)
        hg = _dot(xb, wg_ref[0])
        hu = _dot(xb, wu_ref[0])
        act = (hg * _sigmoid(hg)) * hu
        o_ref[...] = _dot(act.astype(BF16), wd_ref[0])

    @pl.when(t >= nu_ref[0])
    def _():
        o_ref[...] = jnp.zeros(o_ref.shape, F32)


def _moe(tile_e, n_used, xs, wg, wu, wd, tm):
    n_tiles = xs.shape[0] // tm
    grid_spec = pltpu.PrefetchScalarGridSpec(
        num_scalar_prefetch=2,
        grid=(n_tiles,),
        in_specs=[pl.BlockSpec((tm, D_MODEL), lambda t, te, nu: (t, 0)),
                  pl.BlockSpec((1, D_MODEL, D_EXPERT), lambda t, te, nu: (te[t], 0, 0)),
                  pl.BlockSpec((1, D_MODEL, D_EXPERT), lambda t, te, nu: (te[t], 0, 0)),
                  pl.BlockSpec((1, D_EXPERT, D_MODEL), lambda t, te, nu: (te[t], 0, 0))],
        out_specs=pl.BlockSpec((tm, D_MODEL), lambda t, te, nu: (t, 0)),
    )
    return pl.pallas_call(
        _moe_kernel,
        grid_spec=grid_spec,
        out_shape=jax.ShapeDtypeStruct((n_tiles * tm, D_MODEL), F32),
        compiler_params=_params(("arbitrary",)),
        name="moe_experts",
    )(tile_e, n_used, xs, wg, wu, wd)


def _final_kernel(p0_ref, p1_ref, h_ref, ri_ref, ys_hbm, g2_ref, b2_ref, o_ref, y0, y1, sem0, sem1):
    tm = y0.shape[0]
    _row_gather(p0_ref, tm, ys_hbm, y0, sem0)
    _row_gather(p1_ref, tm, ys_hbm, y1, sem1)
    ri = ri_ref[...]
    moe = y0[...] * ri[:, 2:3] + y1[...] * ri[:, 3:4]
    o_ref[...] = _layer_norm(ALPHA * h_ref[...] + moe, g2_ref[...], b2_ref[...])


def _final(pos0, pos1, h, ri, ys, g2, b2, tm):
    T = h.shape[0]
    idx = pl.BlockSpec((1, 1, tm), lambda i: (i, 0, 0), memory_space=pltpu.SMEM)
    row = lambda n: pl.BlockSpec((tm, n), lambda i: (i, 0))
    return pl.pallas_call(
        _final_kernel,
        grid=(T // tm,),
        in_specs=[idx, idx, row(D_MODEL), row(LANES), pl.BlockSpec(memory_space=pl.ANY),
                  _full(g2.shape), _full(b2.shape)],
        out_specs=row(D_MODEL),
        out_shape=jax.ShapeDtypeStruct((T, D_MODEL), F32),
        scratch_shapes=[pltpu.VMEM((tm, D_MODEL), F32), pltpu.VMEM((tm, D_MODEL), F32),
                        pltpu.SemaphoreType.DMA(()), pltpu.SemaphoreType.DMA(())],
        compiler_params=_params(("arbitrary",)),
        name="combine_ln",
    )(pos0, pos1, h, ri, ys, g2, b2)


def _routing_tables(eid, rank, sizes, tm):
    T = eid.shape[0]
    n_tiles = (2 * T) // tm + N_EXPERTS
    padded = ((sizes + tm - 1) // tm) * tm
    pad_end = jnp.cumsum(padded)
    pad_off = pad_end - padded
    pos = jnp.take(pad_off, eid) + rank
    tile_start = jnp.arange(n_tiles, dtype=I32) * tm
    tile_e = jnp.minimum(jnp.sum(tile_start[:, None] >= pad_end[None, :], axis=1), N_EXPERTS - 1).astype(I32)
    n_used = (pad_end[-1] // tm).astype(I32).reshape(1)
    return tile_e, n_used, pos, n_tiles * tm


def _block_diag_ones():
    hid = jnp.arange(HDIM, dtype=I32) // HEAD_DIM
    return (hid[:, None] == hid[None, :]).astype(BF16)


def _layer(x, w_in, mu_shift, w0, w_lora_up, a0, a_lora_up, g_lora_up, k_k, k_a, r_k, ln_x_g, ln_x_b,
           w_branch_a, idx_k_ln_g, idx_k_ln_b, rel_bias, w_branch_b, w_out, ln1_g, ln1_b,
           w_router_grp, b_router_grp, w_router_exp, b_router_exp, w_gate, w_up, w_down, ln2_g, ln2_b):
    B, S, _ = x.shape
    T = B * S
    x2 = x.reshape(T, D_MODEL)
    row = lambda t: t.reshape(1, -1)

    c_r, c_wd, c_k, c_v, c_ad, c_gd = 0, 512, 576, 1088, 1600, 1664
    perm = jnp.concatenate([jnp.arange(c_r, c_r + 512), jnp.arange(c_k, c_k + 512), jnp.arange(c_v, c_v + 512),
                            jnp.arange(c_wd, c_wd + 64), jnp.arange(c_ad, c_ad + 64),
                            jnp.arange(c_gd, c_gd + 128)])
    o_q = RWKV_COLS
    o_c = o_q + 4 * HDIM
    o_g = o_c + IDX_DIM + N_HEADS
    wr = w_in[:, :RWKV_COLS][:, perm].astype(BF16)
    wq = w_in[:, o_q:o_c].astype(BF16)
    wc = jnp.pad(w_in[:, o_c:o_g], ((0, 0), (0, LANES - IDX_DIM - N_HEADS))).astype(BF16)
    wgates = w_in[:, o_g:].astype(BF16)
    pad_idx = lambda t: jnp.pad(t, (0, LANES - IDX_DIM)).reshape(1, LANES)

    pr, q, k, ve, vo, qi, kz, kw = _inproj(x2, wr, wq, wc, pad_idx(idx_k_ln_g), pad_idx(idx_k_ln_b), tm=512)

    bd = _block_diag_ones()
    prm = {
        "mu": row(mu_shift[perm]), "w0": row(w0), "a0": row(a0), "k_k": row(k_k), "k_a": row(k_a),
        "r_k": row(r_k),
        "wup": jnp.pad(w_lora_up, ((0, AAA_LORA), (0, 0))),
        "aup": jnp.pad(a_lora_up, ((DECAY_LORA, 0), (0, 0))),
        "gup": g_lora_up, "bd": bd,
    }
    arrs = _rwkv_prep(pr, S, prm, tm=256)
    ya = _rwkv_scan(arrs, row(ln_x_g), row(ln_x_b), bd, B, S)

    at = _attention(q, k, ve, vo, _indexer(qi, kz, kw, B, S), rel_bias, B, S)

    w_router = jnp.pad(jnp.concatenate([w_router_grp, w_router_exp], axis=1),
                       ((0, 0), (0, LANES - N_GROUPS - N_EXPERTS)))
    b_router = jnp.pad(jnp.concatenate([b_router_grp, b_router_exp]), (0, LANES - N_GROUPS - N_EXPERTS))
    h1, ri, cnt = _merge(x2, ya, at, wgates, w_branch_a.astype(BF16), w_branch_b.astype(BF16),
                         w_out.astype(BF16), row(ln1_g), row(ln1_b), w_router, row(b_router), tm=512)

    tm_e = 512
    tile_e, n_used, pos, n_rows = _routing_tables(ri[:, 0:2].astype(I32), ri[:, 4:6].astype(I32),
                                                  cnt[0, :N_EXPERTS].astype(I32), tm_e)
    tm_f = 256
    pos0 = pos[:, 0].reshape(T // tm_f, 1, tm_f)
    pos1 = pos[:, 1].reshape(T // tm_f, 1, tm_f)
    xs = _dispatch(pos0, pos1, h1, n_rows, tm_f)
    ys = _moe(tile_e, n_used, xs, w_gate.astype(BF16), w_up.astype(BF16), w_down.astype(BF16), tm_e)
    out = _final(pos0, pos1, h1, ri, ys, row(ln2_g), row(ln2_b), tm_f)
    return out.reshape(B, S, D_MODEL)


def kernel(x, w_in, mu_shift, w0, w_lora_up, a0, a_lora_up, g_lora_up, k_k, k_a, r_k, ln_x_g, ln_x_b, w_branch_a, idx_k_ln_g, idx_k_ln_b, rel_bias, w_branch_b, w_out, ln1_g, ln1_b, w_router_grp, b_router_grp, w_router_exp, b_router_exp, w_expert_gate, w_expert_up, w_expert_down, ln2_g, ln2_b):
    assert w_in.shape[0] == 1, "single-layer (DEPTH = 1) block"
    l = 0
    return _layer(x, w_in[l], mu_shift[l], w0[l], w_lora_up[l], a0[l], a_lora_up[l], g_lora_up[l], k_k[l],
                  k_a[l], r_k[l], ln_x_g[l], ln_x_b[l], w_branch_a[l], idx_k_ln_g[l], idx_k_ln_b[l], rel_bias,
                  w_branch_b[l], w_out[l], ln1_g[l], ln1_b[l], w_router_grp[l], b_router_grp[l],
                  w_router_exp[l], b_router_exp[l], w_expert_gate[l], w_expert_up[l], w_expert_down[l],
                  ln2_g[l], ln2_b[l])
```

```python
import functools
import math

import jax
import jax.numpy as jnp
from jax import lax
from jax.experimental import pallas as pl
from jax.experimental.pallas import tpu as pltpu

F32 = jnp.float32
BF16 = jnp.bfloat16
I32 = jnp.int32
I16 = jnp.int16

D_MODEL = 1024
HEAD_DIM = 64
N_HEADS = 8
HDIM = N_HEADS * HEAD_DIM
DECAY_LORA = 64
AAA_LORA = 64
GATE_LORA = 128
RWKV_COLS = 3 * HDIM + DECAY_LORA + AAA_LORA + GATE_LORA
IDX_DIM = 64
MAX_TOPK = 256
N_BUCKETS = 32
MAX_EXACT = 16
MAX_DISTANCE = 128
N_GROUPS = 4
EXPERTS_PER_GROUP = 8
N_EXPERTS = 32
D_EXPERT = 512
GN_EPS = 64e-5
LN_EPS = 1e-5
ALPHA = 2.0 ** 0.25
LANES = 128
CHUNK = 64
SCAN_SUB = 2
QB = 128
KCH = 512
IQB = 512
INT_MIN = -(2 ** 31)
MIN16 = -(2 ** 15)
NEG = -1e30
LOG2E = 1.4426950408889634
VMEM_LIMIT = 56 * 1024 * 1024

NN = (((1,), (0,)), ((), ()))
NT = (((1,), (1,)), ((), ()))


def _dot(a, b, dims=NN):
    return lax.dot_general(a, b, dims, preferred_element_type=F32)


def _split2(x):
    hi = x.astype(BF16)
    lo = (x - hi.astype(F32)).astype(BF16)
    return hi, lo


def _split3(x):
    hi = x.astype(BF16)
    r1 = x - hi.astype(F32)
    mid = r1.astype(BF16)
    lo = (r1 - mid.astype(F32)).astype(BF16)
    return hi, mid, lo


def _mm1(a, b, dims=NN):
    return _dot(a.astype(BF16), b.astype(BF16), dims)


def _mm3(a, b, dims=NN):
    ah, al = _split2(a)
    bh, bl = _split2(b)
    return _dot(ah, bh, dims) + (_dot(ah, bl, dims) + _dot(al, bh, dims))


_mm_misc = _mm1
_mm_inv = _mm1
_mm_state = _mm3


def _mm_exact_lhs(a_bf, b):
    b0, b1, b2 = _split3(b)
    return _dot(a_bf, b0) + (_dot(a_bf, b1) + _dot(a_bf, b2))


def _mm_exact_rhs(a, b_bf):
    a0, a1, a2 = _split3(a)
    return _dot(a0, b_bf) + (_dot(a1, b_bf) + _dot(a2, b_bf))


def _params(sem, vmem=VMEM_LIMIT):
    return pltpu.CompilerParams(dimension_semantics=sem, vmem_limit_bytes=vmem)


def _full(shape):
    nd = len(shape)
    return pl.BlockSpec(shape, lambda *_: (0,) * nd)


def _inproj_kernel(x_ref, wr_ref, wq_ref, wc_ref, lng_ref, lnb_ref,
                   pr_ref, q_ref, k_ref, ve_ref, vo_ref, qi_ref, kz_ref, kw_ref):
    xb = x_ref[...].astype(BF16)
    pr_ref[...] = _dot(xb, wr_ref[...])
    qkv = _dot(xb, wq_ref[...])
    q_ref[...] = (qkv[:, 0:HDIM] * (HEAD_DIM ** -0.5 * LOG2E)).astype(BF16)
    k_ref[...] = qkv[:, HDIM:2 * HDIM].astype(BF16)
    v = qkv[:, 2 * HDIM:3 * HDIM]
    even = (lax.broadcasted_iota(I32, v.shape, 1) % LANES) < HEAD_DIM
    ve_ref[...] = jnp.where(even, v, 1.0).astype(BF16)
    vo_ref[...] = jnp.where(even, 1.0, v).astype(BF16)
    qi_ref[...] = qkv[:, 3 * HDIM:4 * HDIM].astype(BF16)
    c = _dot(xb, wc_ref[...])
    kw_ref[...] = c
    lane = lax.broadcasted_iota(I32, c.shape, 1)
    isk = lane < IDX_DIM
    mu = jnp.sum(jnp.where(isk, c, 0.0), axis=1, keepdims=True) * (1.0 / IDX_DIM)
    d = jnp.where(isk, c - mu, 0.0)
    var = jnp.sum(d * d, axis=1, keepdims=True) * (1.0 / IDX_DIM)
    kn = d * lax.rsqrt(var + LN_EPS) * lng_ref[...] + lnb_ref[...]
    kz_ref[:, 0:LANES] = kn.astype(BF16)
    kz_ref[:, LANES:2 * LANES] = pltpu.roll(kn, IDX_DIM, 1).astype(BF16)


def _inproj(x2, wr, wq, wc, lng, lnb, tm):
    T = x2.shape[0]
    row = lambda n: pl.BlockSpec((tm, n), lambda i: (i, 0))
    return pl.pallas_call(
        _inproj_kernel,
        grid=(T // tm,),
        in_specs=[row(D_MODEL), _full(wr.shape), _full(wq.shape), _full(wc.shape),
                  _full(lng.shape), _full(lnb.shape)],
        out_specs=[row(RWKV_COLS), row(HDIM), row(HDIM), row(HDIM), row(HDIM), row(HDIM),
                   row(2 * LANES), row(LANES)],
        out_shape=[jax.ShapeDtypeStruct((T, RWKV_COLS), F32)]
        + [jax.ShapeDtypeStruct((T, HDIM), BF16)] * 5
        + [jax.ShapeDtypeStruct((T, 2 * LANES), BF16), jax.ShapeDtypeStruct((T, LANES), F32)],
        compiler_params=_params(("parallel",)),
        name="inproj",
    )(x2, wr, wq, wc, lng, lnb)


def _softplus(x):
    return jnp.maximum(x, 0.0) + jnp.log(1.0 + jnp.exp(-jnp.abs(x)))


def _sigmoid(x):
    return 1.0 / (1.0 + jnp.exp(-x))


def _prep_kernel(tiles_per_seq, p_ref, pp_ref, mu_ref, w0_ref, a0_ref, kk_ref, ka_ref, rk_ref,
                 wup_ref, aup_ref, gup_ref, bd_ref,
                 r_ref, lw_ref, k_ref, v_ref, a_ref, b_ref, g_ref, bon_ref):
    i = pl.program_id(0)
    p = p_ref[...]
    tm = p.shape[0]
    first = (i % tiles_per_seq) == 0
    prow = jnp.where(first, 0.0, pp_ref[7:8, :])
    rowid = lax.broadcasted_iota(I32, p.shape, 0)
    prev = jnp.where(rowid == 0, prow, pltpu.roll(p, 1, 0))
    ps = p + (prev - p) * mu_ref[...]
    r = ps[:, 0:HDIM]
    k = ps[:, HDIM:2 * HDIM]
    v = ps[:, 2 * HDIM:3 * HDIM]
    da = ps[:, 3 * HDIM:3 * HDIM + LANES]
    gd = ps[:, 3 * HDIM + LANES:3 * HDIM + 2 * LANES]
    w = -_softplus(-(w0_ref[...] + _mm3(jnp.tanh(da), wup_ref[...]))) - 0.5
    lw_ref[...] = -jnp.exp(w)
    a = _sigmoid(a0_ref[...] + _mm3(da, aup_ref[...]))
    g_ref[...] = _mm3(_sigmoid(gd), gup_ref[...])
    bd = bd_ref[...]
    kk = k * kk_ref[...]
    ss = _mm_exact_rhs(kk * kk, bd)
    kk = kk / jnp.maximum(jnp.sqrt(ss), 1e-12)
    k2 = k * (1.0 + (a - 1.0) * ka_ref[...])
    r_ref[...] = r
    k_ref[...] = k2
    v_ref[...] = v
    a_ref[...] = -kk
    b_ref[...] = kk * a
    bon_ref[...] = _mm_exact_rhs(r * k2 * rk_ref[...], bd) * v


def _rwkv_prep(pr, S, prm, tm):
    T = pr.shape[0]
    row = lambda n: pl.BlockSpec((tm, n), lambda i: (i, 0))
    prev = pl.BlockSpec((8, RWKV_COLS), lambda i: (jnp.maximum(i * (tm // 8) - 1, 0), 0))
    names = ["mu", "w0", "a0", "k_k", "k_a", "r_k", "wup", "aup", "gup", "bd"]
    return pl.pallas_call(
        functools.partial(_prep_kernel, S // tm),
        grid=(T // tm,),
        in_specs=[row(RWKV_COLS), prev] + [_full(prm[n].shape) for n in names],
        out_specs=[row(HDIM)] * 8,
        out_shape=[jax.ShapeDtypeStruct((T, HDIM), F32)] * 8,
        compiler_params=_params(("parallel",)),
        name="rwkv_prep",
    )(pr, pr, *[prm[n] for n in names])


def _scan_kernel(r_ref, lw_ref, k_ref, v_ref, a_ref, b_ref, g_ref, bon_ref, lng_ref, lnb_ref, bd_ref,
                 o_ref, st_ref, y_ref):
    C, N, H = CHUNK, HEAD_DIM, N_HEADS

    @pl.when(pl.program_id(1) == 0)
    def _():
        st_ref[...] = jnp.zeros(st_ref.shape, F32)

    ri = lax.broadcasted_iota(I32, (C, C), 0)
    ci = lax.broadcasted_iota(I32, (C, C), 1)
    incl = ri >= ci
    strict = ri > ci
    eye = ri == ci
    eye_f = jnp.where(eye, 1.0, 0.0)
    lmat = jnp.where(incl, 1.0, 0.0).astype(BF16)
    sls = [slice(h * N, (h + 1) * N) for h in range(H)]
    units = [(s, h) for s in range(SCAN_SUB) for h in range(H)]
    ah, rh, vh, bT, kT, bhT, khT, gam = {}, {}, {}, {}, {}, {}, {}, {}
    for s in range(SCAN_SUB):
        rows = slice(s * C, (s + 1) * C)
        lw = lw_ref[rows, :]
        cum = _mm_exact_lhs(lmat, lw)
        last = cum[C - 1:C, :]
        e_i = jnp.exp(-cum)
        e_end = jnp.exp(last - cum)
        g_s = jnp.exp(last)
        r_t = r_ref[rows, :] * jnp.exp(cum)
        a_t = a_ref[rows, :] * jnp.exp(cum - lw)
        v_s = v_ref[rows, :]
        b_s, k_s = b_ref[rows, :], k_ref[rows, :]
        bT_s, kT_s = (b_s * e_i).T, (k_s * e_i).T
        bhT_s, khT_s = (b_s * e_end).T, (k_s * e_end).T
        for h in range(H):
            un = (s, h)
            ah[un], rh[un], vh[un] = a_t[:, sls[h]], r_t[:, sls[h]], v_s[:, sls[h]]
            bT[un], kT[un] = bT_s[sls[h], :], kT_s[sls[h], :]
            bhT[un], khT[un] = bhT_s[sls[h], :], khT_s[sls[h], :]
            gam[un] = g_s[:, sls[h]]

    ar = {un: jnp.concatenate([ah[un], rh[un]], axis=0) for un in units}
    sb = {un: _mm_misc(ar[un], bT[un]) for un in units}
    sk = {un: _mm_misc(ar[un], kT[un]) for un in units}
    a_ab = {un: jnp.where(strict, sb[un][:C], 0.0) for un in units}
    a_rb = {un: jnp.where(incl, sb[un][C:], 0.0) for un in units}
    a_ak = {un: jnp.where(strict, sk[un][:C], 0.0) for un in units}
    a_rk = {un: jnp.where(incl, sk[un][C:], 0.0) for un in units}
    u = {un: _mm_misc(a_ak[un], vh[un]) for un in units}
    tinv = {un: eye_f + a_ab[un] for un in units}
    xp = a_ab
    for _ in range(5):
        xp = {un: _mm_inv(xp[un], xp[un]) for un in units}
        tinv = {un: tinv[un] + _mm_inv(tinv[un], xp[un]) for un in units}
    pm = {un: _mm_misc(tinv[un], ah[un]) for un in units}
    qm = {un: _mm_misc(tinv[un], u[un]) for un in units}
    r2 = {un: rh[un] + _mm_misc(a_rb[un], pm[un]) for un in units}
    mmat = {un: jnp.where(eye, gam[un], 0.0) + _mm_misc(bhT[un], pm[un]) for un in units}
    y0 = {un: _mm_misc(a_rb[un], qm[un]) + _mm_misc(a_rk[un], vh[un]) for un in units}
    gmat = {un: _mm_misc(bhT[un], qm[un]) + _mm_misc(khT[un], vh[un]) for un in units}
    st = [st_ref[h] for h in range(H)]
    for s in range(SCAN_SUB):
        for h in range(H):
            y_ref[s * C:(s + 1) * C, sls[h]] = _mm_state(r2[(s, h)], st[h]) + y0[(s, h)]
        st = [_mm_state(mmat[(s, h)], st[h]) + gmat[(s, h)] for h in range(H)]
    for h in range(H):
        st_ref[h] = st[h]

    y = y_ref[...]
    bd = bd_ref[...]
    mu = _mm_exact_rhs(y, bd) * (1.0 / N)
    d = y - mu
    var = _mm_exact_rhs(d * d, bd) * (1.0 / N)
    yn = d * lax.rsqrt(var + GN_EPS) * lng_ref[...] + lnb_ref[...]
    o_ref[...] = ((yn + bon_ref[...]) * g_ref[...]).astype(BF16)


def _rwkv_scan(arrs, lng, lnb, bd, B, S):
    rows = SCAN_SUB * CHUNK
    nc = S // rows
    row = pl.BlockSpec((rows, HDIM), lambda b, c: (b * nc + c, 0))
    return pl.pallas_call(
        _scan_kernel,
        grid=(B, nc),
        in_specs=[row] * 8 + [_full(lng.shape), _full(lnb.shape), _full(bd.shape)],
        out_specs=row,
        out_shape=jax.ShapeDtypeStruct((B * S, HDIM), BF16),
        scratch_shapes=[pltpu.VMEM((N_HEADS, HEAD_DIM, HEAD_DIM), F32), pltpu.VMEM((rows, HDIM), F32)],
        compiler_params=_params(("parallel", "arbitrary")),
        name="rwkv_scan",
    )(*arrs, lng, lnb, bd)


def _sort_key(x):
    bits = pltpu.bitcast(x, I32)
    return bits ^ ((bits >> 31) & 0x7FFFFFFF)


def _index_kernel(top_k, qi_ref, kz_ref, kw_ref, mask_ref, hi16, lo16, wb):
    ib = pl.program_id(1)
    t0 = ib * IQB
    n_ch = ib + 1
    nrb, nsub = IQB // QB, KCH // LANES
    lane = lax.broadcasted_iota(I32, (QB, LANES), 1)
    rowi = lax.broadcasted_iota(I32, (QB, LANES), 0)
    blocks = [slice(rb * QB, (rb + 1) * QB) for rb in range(nrb)]

    def tile(c0, s):
        return pl.ds(pl.multiple_of(c0 + s * LANES, LANES), LANES)

    kw = kw_ref[...]
    for h in range(N_HEADS):
        wb[h] = jnp.broadcast_to(kw[:, IDX_DIM + h:IDX_DIM + h + 1] * (N_HEADS ** -0.5), (IQB, LANES)) \
            * (IDX_DIM ** -0.5)

    def score_chunk(c, carry):
        c0 = pl.multiple_of(c * KCH, KCH)
        for rb, rows in enumerate(blocks):
            tot = [jnp.zeros((QB, LANES), F32) for _ in range(nsub)]
            for p in range(N_HEADS // 2):
                qp = qi_ref[rows, p * LANES:(p + 1) * LANES]
                for e in range(2):
                    z = _dot(qp, kz_ref[pl.ds(c0, KCH), e * LANES:(e + 1) * LANES], NT)
                    wbh = wb[2 * p + e, rows, :]
                    for s in range(nsub):
                        tot[s] = tot[s] + jnp.maximum(z[:, s * LANES:(s + 1) * LANES], 0.0) * wbh
            for s in range(nsub):
                causal = c0 + s * LANES + lane <= t0 + rb * QB + rowi
                key = jnp.where(causal, _sort_key(tot[s]), INT_MIN)
                hi16[rows, tile(c0, s)] = (key >> 16).astype(I16)
                lo16[rows, tile(c0, s)] = ((key & 0xFFFF) - 32768).astype(I16)
        return carry
    lax.fori_loop(0, n_ch, score_chunk, 0)

    ones_mat = jnp.ones((LANES, LANES), BF16)
    one_bf, zero_bf, neg_bf = jnp.ones((), BF16), jnp.zeros((), BF16), jnp.full((), NEG, BF16)
    per_block = lambda f: [f(rb) for rb in range(nrb)]

    def count_ge(arr, cand):
        cands = [c.astype(I16) for c in cand]

        def body(c, accs):
            c0 = pl.multiple_of(c * KCH, KCH)
            out = []
            for rb, rows in enumerate(blocks):
                a = accs[rb]
                for s in range(nsub):
                    a = a + jnp.where(arr[rows, tile(c0, s)] >= cands[rb], one_bf, zero_bf)
                out.append(a)
            return tuple(out)
        accs = lax.fori_loop(0, n_ch, body, tuple(jnp.zeros((QB, LANES), BF16) for _ in blocks))
        return [_dot(a, ones_mat) for a in accs]

    def count_gt(arr, v):
        cnt = count_ge(arr, [jnp.minimum(x + 1, 32767) for x in v])
        return [jnp.where(x >= 32767, 0.0, c) for x, c in zip(v, cnt)]

    few = per_block(lambda rb: t0 + rb * QB + rowi < top_k)

    def search16(arr, want, all_count, unchecked_groups):
        def step(bit, carry):
            u, cu = carry
            cand = [x | (jnp.int32(1) << (15 - bit)) for x in u]
            cnt = count_ge(arr, [x - 32768 for x in cand])
            ok = [c >= w for c, w in zip(cnt, want)]
            return (tuple(jnp.where(o, x, y) for o, x, y in zip(ok, cand, u)),
                    tuple(jnp.where(o, x, y) for o, x, y in zip(ok, cnt, cu)))

        def unsettled(carry):
            g, _, cu = carry
            done = [jnp.where(f | (c == w), 1, 0) for f, c, w in zip(few, cu, want)]
            return (g < 4) & (jnp.min(functools.reduce(jnp.minimum, done)) == 0)

        def four_bits(carry):
            g, u, cu = carry
            u, cu = lax.fori_loop(4 * g, 4 * g + 4, step, (u, cu))
            return g + 1, u, cu
        zeros = tuple(jnp.zeros((QB, LANES), I32) for _ in blocks)
        u, cu = lax.fori_loop(0, 4 * unchecked_groups, step, (zeros, tuple(all_count)))
        if unchecked_groups < 4:
            _, u, cu = lax.while_loop(unsettled, four_bits, (jnp.int32(unchecked_groups), u, cu))
        return u, cu

    want = per_block(lambda rb: jnp.full((QB, LANES), top_k, F32))
    u1, c1 = search16(hi16, want, per_block(lambda rb: jnp.full((QB, LANES), n_ch * KCH, F32)), 4)
    p_hi = [x - 32768 for x in u1]
    above = count_gt(hi16, p_hi)
    want2 = [w - a for w, a in zip(want, above)]
    p16 = [x.astype(I16) for x in p_hi]

    def keep_low(c, carry):
        c0 = pl.multiple_of(c * KCH, KCH)
        for rb, rows in enumerate(blocks):
            for s in range(nsub):
                sl = tile(c0, s)
                lo16[rows, sl] = jnp.where(hi16[rows, sl] == p16[rb], lo16[rows, sl], jnp.full((), MIN16, I16))
        return carry
    lax.fori_loop(0, n_ch, keep_low, 0)

    u2, c2 = search16(lo16, want2, [c - a for c, a in zip(c1, above)], 2)
    p_lo = [jnp.maximum(x - 32768, jnp.where(ph == MIN16, MIN16 + 1, MIN16)) for x, ph in zip(u2, p_hi)]
    extra = [jnp.where(f, 0.0, c - w) for f, c, w in zip(few, c2, want2)]
    has_ties = jnp.max(functools.reduce(jnp.maximum, extra)) > 0.0

    @pl.when(jnp.logical_not(has_ties))
    def _():
        l16 = [x.astype(I16) for x in p_lo]

        def body(c, carry):
            c0 = pl.multiple_of(c * KCH, KCH)
            for rb, rows in enumerate(blocks):
                for s in range(nsub):
                    sl = tile(c0, s)
                    h, l = hi16[rows, sl], lo16[rows, sl]
                    sel = (h > p16[rb]) | ((h == p16[rb]) & (l >= l16[rb]))
                    mask_ref[rows, sl] = jnp.where(sel, zero_bf, neg_bf)
            return carry
        lax.fori_loop(0, n_ch, body, 0)

    @pl.when(has_ties)
    def _():
        budget = [w - c for w, c in zip(want2, count_gt(lo16, p_lo))]
        ut = jnp.where(rowi <= lane, 1.0, 0.0).astype(BF16)
        for rb, rows in enumerate(blocks):
            ph, plo, bud = p_hi[rb], p_lo[rb], budget[rb]

            def body(j, before):
                sl = pl.ds(pl.multiple_of(j * LANES, LANES), LANES)
                h, l = hi16[rows, sl].astype(I32), lo16[rows, sl].astype(I32)
                eq = (h == ph) & (l == plo)
                eqf = jnp.where(eq, 1.0, 0.0)
                eqb = eqf.astype(BF16)
                rank = before + _dot(eqb, ut) - eqf
                sel = (h > ph) | ((h == ph) & (l > plo)) | (eq & (rank < bud))
                mask_ref[rows, sl] = jnp.where(sel, 0.0, NEG).astype(BF16)
                return before + _dot(eqb, ones_mat)
            lax.fori_loop(0, n_ch * nsub, body, jnp.zeros((QB, LANES), F32))

    def tail(c, carry):
        mask_ref[:, pl.ds(pl.multiple_of(c * KCH, KCH), KCH)] = jnp.full((IQB, KCH), NEG, BF16)
        return carry
    lax.fori_loop(n_ch, mask_ref.shape[1] // KCH, tail, 0)


def _indexer(qi, kz, kw, B, S):
    nb = S // IQB
    top_k = min(MAX_TOPK, S // 4)
    rows = lambda n: pl.BlockSpec((IQB, n), lambda b, i: (b * nb + i, 0))
    return pl.pallas_call(
        functools.partial(_index_kernel, top_k),
        grid=(B, nb),
        in_specs=[rows(HDIM), pl.BlockSpec((S, 2 * LANES), lambda b, i: (b, 0)), rows(LANES)],
        out_specs=rows(S),
        out_shape=jax.ShapeDtypeStruct((B * S, S), BF16),
        scratch_shapes=[pltpu.VMEM((IQB, S), I16),
                        pltpu.VMEM((IQB, S), I16),
                        pltpu.VMEM((N_HEADS, IQB, LANES), F32)],
        compiler_params=_params(("parallel", "arbitrary")),
        name="dsa_indexer",
    )(qi, kz, kw)


def _attn_kernel(top_k, rb_ref, q_ref, k_ref, ve_ref, vo_ref, qi_ref, kz_ref, kw_ref, o_ref,
                 key_t, mbias, btab, qm, s_scr, p_scr, *state):
    m_scr, acc, a_scr = (state[n * N_HEADS:(n + 1) * N_HEADS] for n in range(3))
    i = pl.program_id(1)
    t0 = i * QB
    lane = lax.broadcasted_iota(I32, (QB, LANES), 1)
    rowi = lax.broadcasted_iota(I32, (QB, LANES), 0)

    @pl.when(i == 0)
    def _build_bias():
        for h in range(N_HEADS):
            btab[h, 2] = jnp.zeros((QB, LANES), F32)
        for m in range(2):
            n = jnp.maximum(m * QB + rowi - lane, 0)
            nf = jnp.maximum(n, 1).astype(F32)
            large = MAX_EXACT + (jnp.log(nf / MAX_EXACT) / math.log(MAX_DISTANCE / MAX_EXACT)
                                 * (N_BUCKETS - MAX_EXACT)).astype(I32)
            bucket = jnp.where(n < MAX_EXACT, n, jnp.minimum(large, N_BUCKETS - 1))
            for h in range(N_HEADS):
                t = jnp.zeros((QB, LANES), F32)
                for bk in range(N_BUCKETS):
                    t = jnp.where(bucket == bk, rb_ref[bk, h], t)
                btab[h, m] = (t - rb_ref[N_BUCKETS - 1, h]) * LOG2E

    n_ch = i // (KCH // QB) + 1
    nsub = KCH // LANES

    kw_t = kw_ref[...].T
    w_t = [kw_t[IDX_DIM + h:IDX_DIM + h + 1, :] * (N_HEADS ** -0.5) * (IDX_DIM ** -0.5) for h in range(N_HEADS)]

    def score_chunk(c, carry):
        c0 = pl.multiple_of(c * KCH, KCH)
        tot = [jnp.zeros((LANES, QB), F32) for _ in range(nsub)]
        for p in range(N_HEADS // 2):
            qp = qi_ref[:, p * LANES:(p + 1) * LANES]
            for e in range(2):
                z = _dot(kz_ref[pl.ds(c0, KCH), e * LANES:(e + 1) * LANES], qp, NT)
                for s in range(nsub):
                    tot[s] = tot[s] + jnp.maximum(z[s * LANES:(s + 1) * LANES, :], 0.0) * w_t[2 * p + e]
        for s in range(nsub):
            causal = c0 + s * LANES + rowi <= t0 + lane
            key_t[pl.ds(pl.multiple_of(c0 + s * LANES, LANES), LANES), :] = \
                jnp.where(causal, _sort_key(tot[s]), INT_MIN)
        return carry
    lax.fori_loop(0, n_ch, score_chunk, 0)

    def count_ge(cand):
        def body(c, cnt):
            kc = key_t[pl.ds(pl.multiple_of(c * KCH, KCH), KCH), :]
            ones = jnp.where(kc >= cand, 1, 0)
            return cnt + jnp.sum(ones.reshape(KCH // 8, 8, QB), axis=0)
        cnt = lax.fori_loop(0, n_ch, body, jnp.zeros((8, QB), I32))
        return jnp.sum(cnt, axis=0, keepdims=True)

    def search(bit, carry):
        u, cu = carry
        cand = u | (jnp.int32(1) << (31 - bit))
        cnt = count_ge(cand ^ INT_MIN)
        ok = cnt >= top_k
        return jnp.where(ok, cand, u), jnp.where(ok, cnt, cu)

    few = (t0 + lax.broadcasted_iota(I32, (1, QB), 1)) < top_k

    def unsettled(carry):
        g, _, cu = carry
        return (g < 8) & (jnp.min(jnp.where(few | (cu == top_k), 1, 0)) == 0)

    def four_bits(carry):
        g, u, cu = carry
        u, cu = lax.fori_loop(4 * g, 4 * g + 4, search, (u, cu))
        return g + 1, u, cu

    u, cu = lax.fori_loop(0, 20, search, (jnp.zeros((1, QB), I32), jnp.zeros((1, QB), I32)))
    _, u, cu = lax.while_loop(unsettled, four_bits, (jnp.int32(5), u, cu))
    thr = jnp.maximum(u ^ INT_MIN, INT_MIN + 1)

    @pl.when(jnp.max(cu) > top_k)
    def _fix_ties():
        budget = (top_k - count_ge(thr + 1)).astype(F32)
        lt = jnp.where(rowi >= lane, 1.0, 0.0).astype(BF16)

        def body(j, before):
            sl = pl.ds(pl.multiple_of(j * LANES, LANES), LANES)
            kc = key_t[sl, :]
            eq = kc == thr
            eqf = jnp.where(eq, 1.0, 0.0)
            rank = before + _dot(lt, eqf.astype(BF16)) - eqf
            key_t[sl, :] = jnp.where(eq & (rank >= budget), INT_MIN, kc)
            return before + jnp.sum(eqf, axis=0, keepdims=True)
        lax.fori_loop(0, i + 1, body, jnp.zeros((1, QB), F32))

    def mask_chunk(c, carry):
        c0 = pl.multiple_of(c * KCH, KCH)
        for s in range(nsub):
            sl = pl.ds(pl.multiple_of(c0 + s * LANES, LANES), LANES)
            mbias[:, sl] = jnp.where(key_t[sl, :] >= thr, 0.0, NEG).T
        return carry
    lax.fori_loop(0, n_ch, mask_chunk, 0)

    q = q_ref[...]
    even = lane < HEAD_DIM
    for p in range(N_HEADS // 2):
        qp = q[:, p * LANES:(p + 1) * LANES]
        qm[2 * p] = jnp.where(even, qp, jnp.zeros_like(qp))
        qm[2 * p + 1] = jnp.where(even, jnp.zeros_like(qp), qp)
    for h in range(N_HEADS):
        m_scr[h][...] = jnp.full((QB, LANES), NEG, F32)
        acc[h][...] = jnp.zeros((QB, LANES), F32)

    def key_block(c, near):
        start = pl.multiple_of(c * KCH, KCH)
        rows = pl.ds(start, KCH)
        nsub = KCH // LANES
        for h in range(N_HEADS):
            cols = slice((h // 2) * LANES, (h // 2 + 1) * LANES)
            s_scr[h] = _dot(qm[h], k_ref[rows, cols], NT)
        for h in range(N_HEADS):
            sub = []
            for n in range(nsub):
                sc = s_scr[h, :, n * LANES:(n + 1) * LANES] + mbias[:, pl.ds(start + n * LANES, LANES)]
                if near:
                    sc = sc + btab[h, jnp.clip(i - (c * nsub + n), 0, 2)]
                sub.append(sc)
            mx = jnp.maximum(jnp.maximum(sub[0], sub[1]), jnp.maximum(sub[2], sub[3]))
            m_old = m_scr[h][...]
            m_new = jnp.maximum(m_old, jnp.max(mx, axis=1, keepdims=True))
            for n in range(nsub):
                p_scr[h, :, n * LANES:(n + 1) * LANES] = jnp.exp2(sub[n] - m_new).astype(BF16)
            a_scr[h][...] = jnp.exp2(m_old - m_new)
            m_scr[h][...] = m_new
        for h in range(N_HEADS):
            cols = slice((h // 2) * LANES, (h // 2 + 1) * LANES)
            v_ref = vo_ref if h % 2 else ve_ref
            acc[h][...] = a_scr[h][...] * acc[h][...] + _dot(p_scr[h], v_ref[rows, cols])

    n_far = jnp.maximum(i - 1, 0) // (KCH // QB)

    def far_body(c, carry):
        key_block(c, False)
        return carry
    lax.fori_loop(0, n_far, far_body, 0)

    def near_body(c, carry):
        key_block(c, True)
        return carry
    lax.fori_loop(n_far, n_ch, near_body, 0)

    for p in range(N_HEADS // 2):
        ae, ao = acc[2 * p][...], acc[2 * p + 1][...]
        oe = ae / pltpu.roll(ae, HEAD_DIM, 1)
        oo = ao / pltpu.roll(ao, HEAD_DIM, 1)
        o_ref[:, p * LANES:(p + 1) * LANES] = jnp.where(even, oe, oo).astype(BF16)


def _attention(q, k, ve, vo, qi, kz, kw, rel_bias, B, S):
    nq = S // QB
    top_k = min(MAX_TOPK, S // 4)
    qrow = lambda n: pl.BlockSpec((QB, n), lambda b, i: (b * nq + i, 0))
    seq = lambda n: pl.BlockSpec((S, n), lambda b, i: (b, 0))
    return pl.pallas_call(
        functools.partial(_attn_kernel, top_k),
        grid=(B, nq),
        in_specs=[pl.BlockSpec(memory_space=pltpu.SMEM), qrow(HDIM), seq(HDIM), seq(HDIM), seq(HDIM),
                  qrow(HDIM), seq(2 * LANES), qrow(LANES)],
        out_specs=qrow(HDIM),
        out_shape=jax.ShapeDtypeStruct((B * S, HDIM), BF16),
        scratch_shapes=[pltpu.VMEM((S, QB), I32),
                        pltpu.VMEM((QB, S), F32),
                        pltpu.VMEM((N_HEADS, 3, QB, LANES), F32),
                        pltpu.VMEM((N_HEADS, QB, LANES), BF16),
                        pltpu.VMEM((N_HEADS, QB, KCH), F32),
                        pltpu.VMEM((N_HEADS, QB, KCH), BF16)]
        + [pltpu.VMEM((QB, LANES), F32)] * (3 * N_HEADS),
        compiler_params=_params(("parallel", "arbitrary")),
        name="dsa_attention",
    )(rel_bias, q, k, ve, vo, qi, kz, kw)


def _layer_norm(x, g, b):
    mu = jnp.mean(x, axis=1, keepdims=True)
    d = x - mu
    var = jnp.mean(d * d, axis=1, keepdims=True)
    return d * lax.rsqrt(var + LN_EPS) * g + b


def _merge_kernel(x_ref, ya_ref, at_ref, wg_ref, wa_ref, wb_ref, wo_ref, g1_ref, b1_ref, wr_ref, br_ref,
                  h_ref, ri_ref, cnt_ref):
    @pl.when(pl.program_id(0) == 0)
    def _():
        cnt_ref[...] = jnp.zeros(cnt_ref.shape, F32)
    half = x_ref.shape[0] // 2
    halves = [slice(sb * half, (sb + 1) * half) for sb in range(2)]
    pre = []
    for rows in halves:
        g = _dot(x_ref[rows, :].astype(BF16), wg_ref[...])
        pre.append((g, _dot(ya_ref[rows, :], wa_ref[...]), _dot(at_ref[rows, :], wb_ref[...])))
    mixes = []
    for g, ya, yb in pre:
        mixin = _sigmoid(g[:, :D_MODEL]) * ya + _sigmoid(g[:, D_MODEL:]) * yb
        mixes.append(_dot(mixin.astype(BF16), wo_ref[...]))
    for rows, mix in zip(halves, mixes):
        h_ref[rows, :] = _layer_norm(ALPHA * x_ref[rows, :] + mix, g1_ref[...], b1_ref[...])
    for rows in halves:
        _route_rows(h_ref.at[rows], wr_ref, br_ref, ri_ref.at[rows], cnt_ref)


def _route_rows(h_ref, wr_ref, br_ref, ri_ref, cnt_ref):
    lg = _mm3(h_ref[...], wr_ref[...]) + br_ref[...]
    lane = lax.broadcasted_iota(I32, lg.shape, 1)
    gl = jnp.where(lane < N_GROUPS, lg, NEG)
    gmax = jnp.max(gl, axis=1, keepdims=True)
    p_g = 1.0 / jnp.sum(jnp.exp(gl - gmax), axis=1, keepdims=True)
    gsel = jnp.min(jnp.where(gl == gmax, lane, LANES), axis=1, keepdims=True)
    lo = N_GROUPS + EXPERTS_PER_GROUP * gsel
    el = jnp.where((lane >= lo) & (lane < lo + EXPERTS_PER_GROUP), lg, NEG)
    e1 = jnp.max(el, axis=1, keepdims=True)
    i1 = jnp.min(jnp.where(el == e1, lane, LANES), axis=1, keepdims=True)
    el2 = jnp.where(lane == i1, NEG, el)
    e2 = jnp.max(el2, axis=1, keepdims=True)
    i2 = jnp.min(jnp.where(el2 == e2, lane, LANES), axis=1, keepdims=True)
    w2 = jnp.exp(e2 - e1)
    gate1 = p_g / (1.0 + w2)
    gate2 = p_g * w2 / (1.0 + w2)
    tm = lg.shape[0]
    oh = jnp.concatenate([jnp.where(lane == i1 - N_GROUPS, 1.0, 0.0),
                          jnp.where(lane == i2 - N_GROUPS, 1.0, 0.0)], axis=0)
    rr = lax.broadcasted_iota(I32, (2 * tm, 2 * tm), 0)
    cc = lax.broadcasted_iota(I32, (2 * tm, 2 * tm), 1)
    before = _dot(jnp.where(rr > cc, 1.0, 0.0).astype(BF16), oh.astype(BF16))
    rank = jnp.sum((before + cnt_ref[0:1, :]) * oh, axis=1, keepdims=True)
    cnt_ref[...] = cnt_ref[...] + jnp.sum(oh, axis=0, keepdims=True)
    cols = [(i1 - N_GROUPS).astype(F32), (i2 - N_GROUPS).astype(F32), gate1, gate2, rank[:tm], rank[tm:]]
    ri = jnp.zeros(lg.shape, F32)
    for n, col in enumerate(cols):
        ri = jnp.where(lane == n, col, ri)
    ri_ref[...] = ri


def _merge(x2, ya, at, wg, wa, wb, wo, g1, b1, wr, br, tm):
    T = x2.shape[0]
    row = lambda n: pl.BlockSpec((tm, n), lambda i: (i, 0))
    ws = [wg, wa, wb, wo, g1, b1, wr, br]
    return pl.pallas_call(
        _merge_kernel,
        grid=(T // tm,),
        in_specs=[row(D_MODEL), row(HDIM), row(HDIM)] + [_full(w.shape) for w in ws],
        out_specs=[row(D_MODEL), row(LANES), _full((8, LANES))],
        out_shape=[jax.ShapeDtypeStruct((T, D_MODEL), F32), jax.ShapeDtypeStruct((T, LANES), F32),
                   jax.ShapeDtypeStruct((8, LANES), F32)],
        compiler_params=_params(("arbitrary",)),
        name="merge_router",
    )(x2, ya, at, *ws)


def _row_gather(idx_ref, n, src_hbm, dst, sem):
    def issue(r, carry):
        pltpu.make_async_copy(src_hbm.at[pl.ds(idx_ref[0, 0, r], 1)], dst.at[pl.ds(r, 1)], sem).start()
        return carry
    lax.fori_loop(0, n, issue, 0, unroll=8)
    pltpu.make_async_copy(src_hbm.at[pl.ds(0, n)], dst, sem).wait()


def _dispatch_kernel(p0_ref, p1_ref, h_ref, xs_in, xs_hbm, sem):
    del xs_in
    tm = h_ref.shape[0]

    def issue(r, carry):
        src = h_ref.at[pl.ds(r, 1)]
        pltpu.make_async_copy(src, xs_hbm.at[pl.ds(p0_ref[0, 0, r], 1)], sem).start()
        pltpu.make_async_copy(src, xs_hbm.at[pl.ds(p1_ref[0, 0, r], 1)], sem).start()
        return carry
    lax.fori_loop(0, tm, issue, 0, unroll=8)
    for _ in range(2):
        pltpu.make_async_copy(h_ref, xs_hbm.at[pl.ds(0, tm)], sem).wait()


def _dispatch(pos0, pos1, h, n_rows, tm):
    T = h.shape[0]
    idx = pl.BlockSpec((1, 1, tm), lambda i: (i, 0, 0), memory_space=pltpu.SMEM)
    return pl.pallas_call(
        _dispatch_kernel,
        grid=(T // tm,),
        in_specs=[idx, idx, pl.BlockSpec((tm, D_MODEL), lambda i: (i, 0)), pl.BlockSpec(memory_space=pl.ANY)],
        out_specs=pl.BlockSpec(memory_space=pl.ANY),
        out_shape=jax.ShapeDtypeStruct((n_rows, D_MODEL), F32),
        scratch_shapes=[pltpu.SemaphoreType.DMA(())],
        input_output_aliases={3: 0},
        compiler_params=_params(("arbitrary",)),
        name="moe_dispatch",
    )(pos0, pos1, h, jnp.zeros((n_rows, D_MODEL), F32))


def _moe_kernel(te_ref, nu_ref, xs_ref, wg_ref, wu_ref, wd_ref, o_ref):
    t = pl.program_id(0)

    @pl.when(t < nu_ref[0])
    def _():
        xb = xs_ref[...].astype(BF16)
        hg = _dot(xb, wg_ref[0])
        hu = _dot(xb, wu_ref[0])
        act = (hg * _sigmoid(hg)) * hu
        o_ref[...] = _dot(act.astype(BF16), wd_ref[0])

    @pl.when(t >= nu_ref[0])
    def _():
        o_ref[...] = jnp.zeros(o_ref.shape, F32)


def _moe(tile_e, n_used, xs, wg, wu, wd, tm):
    n_tiles = xs.shape[0] // tm
    grid_spec = pltpu.PrefetchScalarGridSpec(
        num_scalar_prefetch=2,
        grid=(n_tiles,),
        in_specs=[pl.BlockSpec((tm, D_MODEL), lambda t, te, nu: (t, 0)),
                  pl.BlockSpec((1, D_MODEL, D_EXPERT), lambda t, te, nu: (te[t], 0, 0)),
                  pl.BlockSpec((1, D_MODEL, D_EXPERT), lambda t, te, nu: (te[t], 0, 0)),
                  pl.BlockSpec((1, D_EXPERT, D_MODEL), lambda t, te, nu: (te[t], 0, 0))],
        out_specs=pl.BlockSpec((tm, D_MODEL), lambda t, te, nu: (t, 0)),
    )
    return pl.pallas_call(
        _moe_kernel,
        grid_spec=grid_spec,
        out_shape=jax.ShapeDtypeStruct((n_tiles * tm, D_MODEL), F32),
        compiler_params=_params(("arbitrary",)),
        name="moe_experts",
    )(tile_e, n_used, xs, wg, wu, wd)


def _final_kernel(p0_ref, p1_ref, h_ref, ri_ref, ys_hbm, g2_ref, b2_ref, o_ref, y0, y1, sem0, sem1):
    tm = y0.shape[0]
    _row_gather(p0_ref, tm, ys_hbm, y0, sem0)
    _row_gather(p1_ref, tm, ys_hbm, y1, sem1)
    ri = ri_ref[...]
    moe = y0[...] * ri[:, 2:3] + y1[...] * ri[:, 3:4]
    o_ref[...] = _layer_norm(ALPHA * h_ref[...] + moe, g2_ref[...], b2_ref[...])


def _final(pos0, pos1, h, ri, ys, g2, b2, tm):
    T = h.shape[0]
    idx = pl.BlockSpec((1, 1, tm), lambda i: (i, 0, 0), memory_space=pltpu.SMEM)
    row = lambda n: pl.BlockSpec((tm, n), lambda i: (i, 0))
    return pl.pallas_call(
        _final_kernel,
        grid=(T // tm,),
        in_specs=[idx, idx, row(D_MODEL), row(LANES), pl.BlockSpec(memory_space=pl.ANY),
                  _full(g2.shape), _full(b2.shape)],
        out_specs=row(D_MODEL),
        out_shape=jax.ShapeDtypeStruct((T, D_MODEL), F32),
        scratch_shapes=[pltpu.VMEM((tm, D_MODEL), F32), pltpu.VMEM((tm, D_MODEL), F32),
                        pltpu.SemaphoreType.DMA(()), pltpu.SemaphoreType.DMA(())],
        compiler_params=_params(("arbitrary",)),
        name="combine_ln",
    )(pos0, pos1, h, ri, ys, g2, b2)


def _routing_tables(eid, rank, sizes, tm):
    T = eid.shape[0]
    n_tiles = (2 * T) // tm + N_EXPERTS
    padded = ((sizes + tm - 1) // tm) * tm
    pad_end = jnp.cumsum(padded)
    pad_off = pad_end - padded
    pos = jnp.take(pad_off, eid) + rank
    tile_start = jnp.arange(n_tiles, dtype=I32) * tm
    tile_e = jnp.minimum(jnp.sum(tile_start[:, None] >= pad_end[None, :], axis=1), N_EXPERTS - 1).astype(I32)
    n_used = (pad_end[-1] // tm).astype(I32).reshape(1)
    return tile_e, n_used, pos, n_tiles * tm


def _block_diag_ones():
    hid = jnp.arange(HDIM, dtype=I32) // HEAD_DIM
    return (hid[:, None] == hid[None, :]).astype(BF16)


def _layer(x, w_in, mu_shift, w0, w_lora_up, a0, a_lora_up, g_lora_up, k_k, k_a, r_k, ln_x_g, ln_x_b,
           w_branch_a, idx_k_ln_g, idx_k_ln_b, rel_bias, w_branch_b, w_out, ln1_g, ln1_b,
           w_router_grp, b_router_grp, w_router_exp, b_router_exp, w_gate, w_up, w_down, ln2_g, ln2_b):
    B, S, _ = x.shape
    T = B * S
    x2 = x.reshape(T, D_MODEL)
    row = lambda t: t.reshape(1, -1)

    c_r, c_wd, c_k, c_v, c_ad, c_gd = 0, 512, 576, 1088, 1600, 1664
    perm = jnp.concatenate([jnp.arange(c_r, c_r + 512), jnp.arange(c_k, c_k + 512), jnp.arange(c_v, c_v + 512),
                            jnp.arange(c_wd, c_wd + 64), jnp.arange(c_ad, c_ad + 64),
                            jnp.arange(c_gd, c_gd + 128)])
    o_q = RWKV_COLS
    o_c = o_q + 4 * HDIM
    o_g = o_c + IDX_DIM + N_HEADS
    wr = w_in[:, :RWKV_COLS][:, perm].astype(BF16)
    wq = w_in[:, o_q:o_c].astype(BF16)
    wc = jnp.pad(w_in[:, o_c:o_g], ((0, 0), (0, LANES - IDX_DIM - N_HEADS))).astype(BF16)
    wgates = w_in[:, o_g:].astype(BF16)
    pad_idx = lambda t: jnp.pad(t, (0, LANES - IDX_DIM)).reshape(1, LANES)

    pr, q, k, ve, vo, qi, kz, kw = _inproj(x2, wr, wq, wc, pad_idx(idx_k_ln_g), pad_idx(idx_k_ln_b), tm=512)

    bd = _block_diag_ones()
    prm = {
        "mu": row(mu_shift[perm]), "w0": row(w0), "a0": row(a0), "k_k": row(k_k), "k_a": row(k_a),
        "r_k": row(r_k),
        "wup": jnp.pad(w_lora_up, ((0, AAA_LORA), (0, 0))),
        "aup": jnp.pad(a_lora_up, ((DECAY_LORA, 0), (0, 0))),
        "gup": g_lora_up, "bd": bd,
    }
    arrs = _rwkv_prep(pr, S, prm, tm=256)
    ya = _rwkv_scan(arrs, row(ln_x_g), row(ln_x_b), bd, B, S)

    at = _attention(q, k, ve, vo, qi, kz, kw, rel_bias, B, S)

    w_router = jnp.pad(jnp.concatenate([w_router_grp, w_router_exp], axis=1),
                       ((0, 0), (0, LANES - N_GROUPS - N_EXPERTS)))
    b_router = jnp.pad(jnp.concatenate([b_router_grp, b_router_exp]), (0, LANES - N_GROUPS - N_EXPERTS))
    h1, ri, cnt = _merge(x2, ya, at, wgates, w_branch_a.astype(BF16), w_branch_b.astype(BF16),
                         w_out.astype(BF16), row(ln1_g), row(ln1_b), w_router, row(b_router), tm=512)

    tm_e = 512
    tile_e, n_used, pos, n_rows = _routing_tables(ri[:, 0:2].astype(I32), ri[:, 4:6].astype(I32),
                                                  cnt[0, :N_EXPERTS].astype(I32), tm_e)
    tm_f = 256
    pos0 = pos[:, 0].reshape(T // tm_f, 1, tm_f)
    pos1 = pos[:, 1].reshape(T // tm_f, 1, tm_f)
    xs = _dispatch(pos0, pos1, h1, n_rows, tm_f)
    ys = _moe(tile_e, n_used, xs, w_gate.astype(BF16), w_up.astype(BF16), w_down.astype(BF16), tm_e)
    out = _final(pos0, pos1, h1, ri, ys, row(ln2_g), row(ln2_b), tm_f)
    return out.reshape(B, S, D_MODEL)


def kernel(x, w_in, mu_shift, w0, w_lora_up, a0, a_lora_up, g_lora_up, k_k, k_a, r_k, ln_x_g, ln_x_b, w_branch_a, idx_k_ln_g, idx_k_ln_b, rel_bias, w_branch_b, w_out, ln1_g, ln1_b, w_router_grp, b_router_grp, w_router_exp, b_router_exp, w_expert_gate, w_expert_up, w_expert_down, ln2_g, ln2_b):
    assert w_in.shape[0] == 1, "single-layer (DEPTH = 1) block"
    l = 0
    return _layer(x, w_in[l], mu_shift[l], w0[l], w_lora_up[l], a0[l], a_lora_up[l], g_lora_up[l], k_k[l],
                  k_a[l], r_k[l], ln_x_g[l], ln_x_b[l], w_branch_a[l], idx_k_ln_g[l], idx_k_ln_b[l], rel_bias,
                  w_branch_b[l], w_out[l], ln1_g[l], ln1_b[l], w_router_grp[l], b_router_grp[l],
                  w_router_exp[l], b_router_exp[l], w_expert_gate[l], w_expert_up[l], w_expert_down[l],
                  ln2_g[l], ln2_b[l])
```

```python
import functools
import math

import jax
import jax.numpy as jnp
from jax import lax
from jax.experimental import pallas as pl
from jax.experimental.pallas import tpu as pltpu

F32 = jnp.float32
BF16 = jnp.bfloat16
I32 = jnp.int32
I16 = jnp.int16

D_MODEL = 1024
HEAD_DIM = 64
N_HEADS = 8
HDIM = N_HEADS * HEAD_DIM
DECAY_LORA = 64
AAA_LORA = 64
GATE_LORA = 128
RWKV_COLS = 3 * HDIM + DECAY_LORA + AAA_LORA + GATE_LORA
IDX_DIM = 64
MAX_TOPK = 256
N_BUCKETS = 32
MAX_EXACT = 16
MAX_DISTANCE = 128
N_GROUPS = 4
EXPERTS_PER_GROUP = 8
N_EXPERTS = 32
D_EXPERT = 512
GN_EPS = 64e-5
LN_EPS = 1e-5
ALPHA = 2.0 ** 0.25
LANES = 128
CHUNK = 64
SCAN_SUB = 4
QB = 128
KCH = 512
IQB = 512
INT_MIN = -(2 ** 31)
MIN16 = -(2 ** 15)
NEG = -1e30
LOG2E = 1.4426950408889634
VMEM_LIMIT = 56 * 1024 * 1024

NN = (((1,), (0,)), ((), ()))
NT = (((1,), (1,)), ((), ()))


def _dot(a, b, dims=NN):
    return lax.dot_general(a, b, dims, preferred_element_type=F32)


def _split2(x):
    hi = x.astype(BF16)
    lo = (x - hi.astype(F32)).astype(BF16)
    return hi, lo


def _split3(x):
    hi = x.astype(BF16)
    r1 = x - hi.astype(F32)
    mid = r1.astype(BF16)
    lo = (r1 - mid.astype(F32)).astype(BF16)
    return hi, mid, lo


def _mm1(a, b, dims=NN):
    return _dot(a.astype(BF16), b.astype(BF16), dims)


def _mm3(a, b, dims=NN):
    ah, al = _split2(a)
    bh, bl = _split2(b)
    return _dot(ah, bh, dims) + (_dot(ah, bl, dims) + _dot(al, bh, dims))


_mm_misc = _mm1
_mm_inv = _mm1
_mm_state = _mm3


def _mm_exact_lhs(a_bf, b):
    b0, b1, b2 = _split3(b)
    return _dot(a_bf, b0) + (_dot(a_bf, b1) + _dot(a_bf, b2))


def _mm_exact_rhs(a, b_bf):
    a0, a1, a2 = _split3(a)
    return _dot(a0, b_bf) + (_dot(a1, b_bf) + _dot(a2, b_bf))


def _params(sem, vmem=VMEM_LIMIT):
    return pltpu.CompilerParams(dimension_semantics=sem, vmem_limit_bytes=vmem)


def _full(shape):
    nd = len(shape)
    return pl.BlockSpec(shape, lambda *_: (0,) * nd)


def _inproj_kernel(x_ref, wr_ref, wq_ref, wc_ref, lng_ref, lnb_ref,
                   pr_ref, q_ref, k_ref, ve_ref, vo_ref, qi_ref, kz_ref, kw_ref):
    xb = x_ref[...].astype(BF16)
    pr_ref[...] = _dot(xb, wr_ref[...])
    qkv = _dot(xb, wq_ref[...])
    q_ref[...] = (qkv[:, 0:HDIM] * (HEAD_DIM ** -0.5 * LOG2E)).astype(BF16)
    k_ref[...] = qkv[:, HDIM:2 * HDIM].astype(BF16)
    v = qkv[:, 2 * HDIM:3 * HDIM]
    even = (lax.broadcasted_iota(I32, v.shape, 1) % LANES) < HEAD_DIM
    ve_ref[...] = jnp.where(even, v, 1.0).astype(BF16)
    vo_ref[...] = jnp.where(even, 1.0, v).astype(BF16)
    qi_ref[...] = qkv[:, 3 * HDIM:4 * HDIM].astype(BF16)
    c = _dot(xb, wc_ref[...])
    kw_ref[...] = c
    lane = lax.broadcasted_iota(I32, c.shape, 1)
    isk = lane < IDX_DIM
    mu = jnp.sum(jnp.where(isk, c, 0.0), axis=1, keepdims=True) * (1.0 / IDX_DIM)
    d = jnp.where(isk, c - mu, 0.0)
    var = jnp.sum(d * d, axis=1, keepdims=True) * (1.0 / IDX_DIM)
    kn = d * lax.rsqrt(var + LN_EPS) * lng_ref[...] + lnb_ref[...]
    kz_ref[:, 0:LANES] = kn.astype(BF16)
    kz_ref[:, LANES:2 * LANES] = pltpu.roll(kn, IDX_DIM, 1).astype(BF16)


def _inproj(x2, wr, wq, wc, lng, lnb, tm):
    T = x2.shape[0]
    row = lambda n: pl.BlockSpec((tm, n), lambda i: (i, 0))
    return pl.pallas_call(
        _inproj_kernel,
        grid=(T // tm,),
        in_specs=[row(D_MODEL), _full(wr.shape), _full(wq.shape), _full(wc.shape),
                  _full(lng.shape), _full(lnb.shape)],
        out_specs=[row(RWKV_COLS), row(HDIM), row(HDIM), row(HDIM), row(HDIM), row(HDIM),
                   row(2 * LANES), row(LANES)],
        out_shape=[jax.ShapeDtypeStruct((T, RWKV_COLS), F32)]
        + [jax.ShapeDtypeStruct((T, HDIM), BF16)] * 5
        + [jax.ShapeDtypeStruct((T, 2 * LANES), BF16), jax.ShapeDtypeStruct((T, LANES), F32)],
        compiler_params=_params(("parallel",)),
        name="inproj",
    )(x2, wr, wq, wc, lng, lnb)


def _softplus(x):
    return jnp.maximum(x, 0.0) + jnp.log(1.0 + jnp.exp(-jnp.abs(x)))


def _sigmoid(x):
    return 1.0 / (1.0 + jnp.exp(-x))


def _prep_kernel(tiles_per_seq, p_ref, pp_ref, mu_ref, w0_ref, a0_ref, kk_ref, ka_ref, rk_ref,
                 wup_ref, aup_ref, gup_ref, bd_ref,
                 r_ref, lw_ref, k_ref, v_ref, a_ref, b_ref, g_ref, bon_ref):
    i = pl.program_id(0)
    p = p_ref[...]
    tm = p.shape[0]
    first = (i % tiles_per_seq) == 0
    prow = jnp.where(first, 0.0, pp_ref[7:8, :])
    rowid = lax.broadcasted_iota(I32, p.shape, 0)
    prev = jnp.where(rowid == 0, prow, pltpu.roll(p, 1, 0))
    ps = p + (prev - p) * mu_ref[...]
    r = ps[:, 0:HDIM]
    k = ps[:, HDIM:2 * HDIM]
    v = ps[:, 2 * HDIM:3 * HDIM]
    da = ps[:, 3 * HDIM:3 * HDIM + LANES]
    gd = ps[:, 3 * HDIM + LANES:3 * HDIM + 2 * LANES]
    w = -_softplus(-(w0_ref[...] + _mm3(jnp.tanh(da), wup_ref[...]))) - 0.5
    lw_ref[...] = -jnp.exp(w)
    a = _sigmoid(a0_ref[...] + _mm3(da, aup_ref[...]))
    g_ref[...] = _mm3(_sigmoid(gd), gup_ref[...])
    bd = bd_ref[...]
    kk = k * kk_ref[...]
    ss = _mm_exact_rhs(kk * kk, bd)
    kk = kk / jnp.maximum(jnp.sqrt(ss), 1e-12)
    k2 = k * (1.0 + (a - 1.0) * ka_ref[...])
    r_ref[...] = r
    k_ref[...] = k2
    v_ref[...] = v
    a_ref[...] = -kk
    b_ref[...] = kk * a
    bon_ref[...] = _mm_exact_rhs(r * k2 * rk_ref[...], bd) * v


def _rwkv_prep(pr, S, prm, tm):
    T = pr.shape[0]
    row = lambda n: pl.BlockSpec((tm, n), lambda i: (i, 0))
    prev = pl.BlockSpec((8, RWKV_COLS), lambda i: (jnp.maximum(i * (tm // 8) - 1, 0), 0))
    names = ["mu", "w0", "a0", "k_k", "k_a", "r_k", "wup", "aup", "gup", "bd"]
    return pl.pallas_call(
        functools.partial(_prep_kernel, S // tm),
        grid=(T // tm,),
        in_specs=[row(RWKV_COLS), prev] + [_full(prm[n].shape) for n in names],
        out_specs=[row(HDIM)] * 8,
        out_shape=[jax.ShapeDtypeStruct((T, HDIM), F32)] * 8,
        compiler_params=_params(("parallel",)),
        name="rwkv_prep",
    )(pr, pr, *[prm[n] for n in names])


def _scan_kernel(r_ref, lw_ref, k_ref, v_ref, a_ref, b_ref, g_ref, bon_ref, lng_ref, lnb_ref, bd_ref,
                 o_ref, st_ref, y_ref):
    C, N, H = CHUNK, HEAD_DIM, N_HEADS

    @pl.when(pl.program_id(1) == 0)
    def _():
        st_ref[...] = jnp.zeros(st_ref.shape, F32)

    ri = lax.broadcasted_iota(I32, (C, C), 0)
    ci = lax.broadcasted_iota(I32, (C, C), 1)
    incl = ri >= ci
    strict = ri > ci
    eye = ri == ci
    eye_f = jnp.where(eye, 1.0, 0.0)
    lmat = jnp.where(incl, 1.0, 0.0).astype(BF16)
    sls = [slice(h * N, (h + 1) * N) for h in range(H)]
    units = [(s, h) for s in range(SCAN_SUB) for h in range(H)]
    ah, rh, vh, bT, kT, bhT, khT, gam = {}, {}, {}, {}, {}, {}, {}, {}
    for s in range(SCAN_SUB):
        rows = slice(s * C, (s + 1) * C)
        lw = lw_ref[rows, :]
        cum = _mm_exact_lhs(lmat, lw)
        last = cum[C - 1:C, :]
        e_i = jnp.exp(-cum)
        e_end = jnp.exp(last - cum)
        g_s = jnp.exp(last)
        r_t = r_ref[rows, :] * jnp.exp(cum)
        a_t = a_ref[rows, :] * jnp.exp(cum - lw)
        v_s = v_ref[rows, :]
        b_s, k_s = b_ref[rows, :], k_ref[rows, :]
        bT_s, kT_s = (b_s * e_i).T, (k_s * e_i).T
        bhT_s, khT_s = (b_s * e_end).T, (k_s * e_end).T
        for h in range(H):
            un = (s, h)
            ah[un], rh[un], vh[un] = a_t[:, sls[h]], r_t[:, sls[h]], v_s[:, sls[h]]
            bT[un], kT[un] = bT_s[sls[h], :], kT_s[sls[h], :]
            bhT[un], khT[un] = bhT_s[sls[h], :], khT_s[sls[h], :]
            gam[un] = g_s[:, sls[h]]

    ar = {un: jnp.concatenate([ah[un], rh[un]], axis=0) for un in units}
    sb = {un: _mm_misc(ar[un], bT[un]) for un in units}
    sk = {un: _mm_misc(ar[un], kT[un]) for un in units}
    a_ab = {un: jnp.where(strict, sb[un][:C], 0.0) for un in units}
    a_rb = {un: jnp.where(incl, sb[un][C:], 0.0) for un in units}
    a_ak = {un: jnp.where(strict, sk[un][:C], 0.0) for un in units}
    a_rk = {un: jnp.where(incl, sk[un][C:], 0.0) for un in units}
    u = {un: _mm_misc(a_ak[un], vh[un]) for un in units}
    tinv = {un: eye_f + a_ab[un] for un in units}
    xp = a_ab
    for _ in range(5):
        xp = {un: _mm_inv(xp[un], xp[un]) for un in units}
        tinv = {un: tinv[un] + _mm_inv(tinv[un], xp[un]) for un in units}
    pm = {un: _mm_misc(tinv[un], ah[un]) for un in units}
    qm = {un: _mm_misc(tinv[un], u[un]) for un in units}
    r2 = {un: rh[un] + _mm_misc(a_rb[un], pm[un]) for un in units}
    mmat = {un: jnp.where(eye, gam[un], 0.0) + _mm_misc(bhT[un], pm[un]) for un in units}
    y0 = {un: _mm_misc(a_rb[un], qm[un]) + _mm_misc(a_rk[un], vh[un]) for un in units}
    gmat = {un: _mm_misc(bhT[un], qm[un]) + _mm_misc(khT[un], vh[un]) for un in units}
    st = [st_ref[h] for h in range(H)]
    for s in range(SCAN_SUB):
        for h in range(H):
            y_ref[s * C:(s + 1) * C, sls[h]] = _mm_state(r2[(s, h)], st[h]) + y0[(s, h)]
        st = [_mm_state(mmat[(s, h)], st[h]) + gmat[(s, h)] for h in range(H)]
    for h in range(H):
        st_ref[h] = st[h]

    y = y_ref[...]
    bd = bd_ref[...]
    mu = _mm_exact_rhs(y, bd) * (1.0 / N)
    d = y - mu
    var = _mm_exact_rhs(d * d, bd) * (1.0 / N)
    yn = d * lax.rsqrt(var + GN_EPS) * lng_ref[...] + lnb_ref[...]
    o_ref[...] = ((yn + bon_ref[...]) * g_ref[...]).astype(BF16)


def _rwkv_scan(arrs, lng, lnb, bd, B, S):
    rows = SCAN_SUB * CHUNK
    nc = S // rows
    row = pl.BlockSpec((rows, HDIM), lambda b, c: (b * nc + c, 0))
    return pl.pallas_call(
        _scan_kernel,
        grid=(B, nc),
        in_specs=[row] * 8 + [_full(lng.shape), _full(lnb.shape), _full(bd.shape)],
        out_specs=row,
        out_shape=jax.ShapeDtypeStruct((B * S, HDIM), BF16),
        scratch_shapes=[pltpu.VMEM((N_HEADS, HEAD_DIM, HEAD_DIM), F32), pltpu.VMEM((rows, HDIM), F32)],
        compiler_params=_params(("parallel", "arbitrary")),
        name="rwkv_scan",
    )(*arrs, lng, lnb, bd)


def _sort_key(x):
    bits = pltpu.bitcast(x, I32)
    return bits ^ ((bits >> 31) & 0x7FFFFFFF)


def _index_kernel(top_k, qi_ref, kz_ref, kw_ref, mask_ref, hi16, lo16, wb):
    ib = pl.program_id(1)
    t0 = ib * IQB
    n_ch = ib + 1
    nrb, nsub = IQB // QB, KCH // LANES
    lane = lax.broadcasted_iota(I32, (QB, LANES), 1)
    rowi = lax.broadcasted_iota(I32, (QB, LANES), 0)
    blocks = [slice(rb * QB, (rb + 1) * QB) for rb in range(nrb)]

    def tile(c0, s):
        return pl.ds(pl.multiple_of(c0 + s * LANES, LANES), LANES)

    kw = kw_ref[...]
    for h in range(N_HEADS):
        wb[h] = jnp.broadcast_to(kw[:, IDX_DIM + h:IDX_DIM + h + 1] * (N_HEADS ** -0.5), (IQB, LANES)) \
            * (IDX_DIM ** -0.5)

    def score_chunk(c, carry):
        c0 = pl.multiple_of(c * KCH, KCH)
        for rb, rows in enumerate(blocks):
            tot = [jnp.zeros((QB, LANES), F32) for _ in range(nsub)]
            for p in range(N_HEADS // 2):
                qp = qi_ref[rows, p * LANES:(p + 1) * LANES]
                for e in range(2):
                    z = _dot(qp, kz_ref[pl.ds(c0, KCH), e * LANES:(e + 1) * LANES], NT)
                    wbh = wb[2 * p + e, rows, :]
                    for s in range(nsub):
                        tot[s] = tot[s] + jnp.maximum(z[:, s * LANES:(s + 1) * LANES], 0.0) * wbh
            for s in range(nsub):
                causal = c0 + s * LANES + lane <= t0 + rb * QB + rowi
                key = jnp.where(causal, _sort_key(tot[s]), INT_MIN)
                hi16[rows, tile(c0, s)] = (key >> 16).astype(I16)
                lo16[rows, tile(c0, s)] = ((key & 0xFFFF) - 32768).astype(I16)
        return carry
    lax.fori_loop(0, n_ch, score_chunk, 0)

    ones_mat = jnp.ones((LANES, LANES), BF16)
    one_bf, zero_bf, neg_bf = jnp.ones((), BF16), jnp.zeros((), BF16), jnp.full((), NEG, BF16)
    per_block = lambda f: [f(rb) for rb in range(nrb)]

    def count_ge(arr, cand):
        cands = [c.astype(I16) for c in cand]

        def body(c, accs):
            c0 = pl.multiple_of(c * KCH, KCH)
            out = []
            for rb, rows in enumerate(blocks):
                a = accs[rb]
                for s in range(nsub):
                    a = a + jnp.where(arr[rows, tile(c0, s)] >= cands[rb], one_bf, zero_bf)
                out.append(a)
            return tuple(out)
        accs = lax.fori_loop(0, n_ch, body, tuple(jnp.zeros((QB, LANES), BF16) for _ in blocks))
        return [_dot(a, ones_mat) for a in accs]

    def count_gt(arr, v):
        cnt = count_ge(arr, [jnp.minimum(x + 1, 32767) for x in v])
        return [jnp.where(x >= 32767, 0.0, c) for x, c in zip(v, cnt)]

    few = per_block(lambda rb: t0 + rb * QB + rowi < top_k)

    def search16(arr, want, all_count, unchecked_groups):
        def step(bit, carry):
            u, cu = carry
            cand = [x | (jnp.int32(1) << (15 - bit)) for x in u]
            cnt = count_ge(arr, [x - 32768 for x in cand])
            ok = [c >= w for c, w in zip(cnt, want)]
            return (tuple(jnp.where(o, x, y) for o, x, y in zip(ok, cand, u)),
                    tuple(jnp.where(o, x, y) for o, x, y in zip(ok, cnt, cu)))

        def unsettled(carry):
            g, _, cu = carry
            done = [jnp.where(f | (c == w), 1, 0) for f, c, w in zip(few, cu, want)]
            return (g < 4) & (jnp.min(functools.reduce(jnp.minimum, done)) == 0)

        def four_bits(carry):
            g, u, cu = carry
            u, cu = lax.fori_loop(4 * g, 4 * g + 4, step, (u, cu))
            return g + 1, u, cu
        zeros = tuple(jnp.zeros((QB, LANES), I32) for _ in blocks)
        u, cu = lax.fori_loop(0, 4 * unchecked_groups, step, (zeros, tuple(all_count)))
        if unchecked_groups < 4:
            _, u, cu = lax.while_loop(unsettled, four_bits, (jnp.int32(unchecked_groups), u, cu))
        return u, cu

    want = per_block(lambda rb: jnp.full((QB, LANES), top_k, F32))
    u1, c1 = search16(hi16, want, per_block(lambda rb: jnp.full((QB, LANES), n_ch * KCH, F32)), 4)
    p_hi = [x - 32768 for x in u1]
    above = count_gt(hi16, p_hi)
    want2 = [w - a for w, a in zip(want, above)]
    p16 = [x.astype(I16) for x in p_hi]

    def keep_low(c, carry):
        c0 = pl.multiple_of(c * KCH, KCH)
        for rb, rows in enumerate(blocks):
            for s in range(nsub):
                sl = tile(c0, s)
                lo16[rows, sl] = jnp.where(hi16[rows, sl] == p16[rb], lo16[rows, sl], jnp.full((), MIN16, I16))
        return carry
    lax.fori_loop(0, n_ch, keep_low, 0)

    u2, c2 = search16(lo16, want2, [c - a for c, a in zip(c1, above)], 2)
    p_lo = [jnp.maximum(x - 32768, jnp.where(ph == MIN16, MIN16 + 1, MIN16)) for x, ph in zip(u2, p_hi)]
    extra = [jnp.where(f, 0.0, c - w) for f, c, w in zip(few, c2, want2)]
    has_ties = jnp.max(functools.reduce(jnp.maximum, extra)) > 0.0

    @pl.when(jnp.logical_not(has_ties))
    def _():
        l16 = [x.astype(I16) for x in p_lo]

        def body(c, carry):
            c0 = pl.multiple_of(c * KCH, KCH)
            for rb, rows in enumerate(blocks):
                for s in range(nsub):
                    sl = tile(c0, s)
                    h, l = hi16[rows, sl], lo16[rows, sl]
                    sel = (h > p16[rb]) | ((h == p16[rb]) & (l >= l16[rb]))
                    mask_ref[rows, sl] = jnp.where(sel, zero_bf, neg_bf)
            return carry
        lax.fori_loop(0, n_ch, body, 0)

    @pl.when(has_ties)
    def _():
        budget = [w - c for w, c in zip(want2, count_gt(lo16, p_lo))]
        ut = jnp.where(rowi <= lane, 1.0, 0.0).astype(BF16)
        for rb, rows in enumerate(blocks):
            ph, plo, bud = p_hi[rb], p_lo[rb], budget[rb]

            def body(j, before):
                sl = pl.ds(pl.multiple_of(j * LANES, LANES), LANES)
                h, l = hi16[rows, sl].astype(I32), lo16[rows, sl].astype(I32)
                eq = (h == ph) & (l == plo)
                eqf = jnp.where(eq, 1.0, 0.0)
                eqb = eqf.astype(BF16)
                rank = before + _dot(eqb, ut) - eqf
                sel = (h > ph) | ((h == ph) & (l > plo)) | (eq & (rank < bud))
                mask_ref[rows, sl] = jnp.where(sel, 0.0, NEG).astype(BF16)
                return before + _dot(eqb, ones_mat)
            lax.fori_loop(0, n_ch * nsub, body, jnp.zeros((QB, LANES), F32))

    def tail(c, carry):
        mask_ref[:, pl.ds(pl.multiple_of(c * KCH, KCH), KCH)] = jnp.full((IQB, KCH), NEG, BF16)
        return carry
    lax.fori_loop(n_ch, mask_ref.shape[1] // KCH, tail, 0)


def _indexer(qi, kz, kw, B, S):
    nb = S // IQB
    top_k = min(MAX_TOPK, S // 4)
    rows = lambda n: pl.BlockSpec((IQB, n), lambda b, i: (b * nb + i, 0))
    return pl.pallas_call(
        functools.partial(_index_kernel, top_k),
        grid=(B, nb),
        in_specs=[rows(HDIM), pl.BlockSpec((S, 2 * LANES), lambda b, i: (b, 0)), rows(LANES)],
        out_specs=rows(S),
        out_shape=jax.ShapeDtypeStruct((B * S, S), BF16),
        scratch_shapes=[pltpu.VMEM((IQB, S), I16),
                        pltpu.VMEM((IQB, S), I16),
                        pltpu.VMEM((N_HEADS, IQB, LANES), F32)],
        compiler_params=_params(("parallel", "arbitrary")),
        name="dsa_indexer",
    )(qi, kz, kw)


def _attn_kernel(top_k, rb_ref, q_ref, k_ref, ve_ref, vo_ref, qi_ref, kz_ref, kw_ref, o_ref,
                 hi_t, lo_t, mbias, btab, qm, s_scr, p_scr, *state):
    m_scr, acc, a_scr = (state[n * N_HEADS:(n + 1) * N_HEADS] for n in range(3))
    i = pl.program_id(1)
    t0 = i * QB
    lane = lax.broadcasted_iota(I32, (QB, LANES), 1)
    rowi = lax.broadcasted_iota(I32, (QB, LANES), 0)

    @pl.when(i == 0)
    def _build_bias():
        for h in range(N_HEADS):
            btab[h, 2] = jnp.zeros((QB, LANES), F32)
        for m in range(2):
            n = jnp.maximum(m * QB + rowi - lane, 0)
            nf = jnp.maximum(n, 1).astype(F32)
            large = MAX_EXACT + (jnp.log(nf / MAX_EXACT) / math.log(MAX_DISTANCE / MAX_EXACT)
                                 * (N_BUCKETS - MAX_EXACT)).astype(I32)
            bucket = jnp.where(n < MAX_EXACT, n, jnp.minimum(large, N_BUCKETS - 1))
            for h in range(N_HEADS):
                t = jnp.zeros((QB, LANES), F32)
                for bk in range(N_BUCKETS):
                    t = jnp.where(bucket == bk, rb_ref[bk, h], t)
                btab[h, m] = (t - rb_ref[N_BUCKETS - 1, h]) * LOG2E

    n_ch = i // (KCH // QB) + 1
    nsub = KCH // LANES

    kw_t = kw_ref[...].T
    w_t = [kw_t[IDX_DIM + h:IDX_DIM + h + 1, :] * (N_HEADS ** -0.5) * (IDX_DIM ** -0.5) for h in range(N_HEADS)]

    def score_chunk(c, carry):
        c0 = pl.multiple_of(c * KCH, KCH)
        tot = [jnp.zeros((LANES, QB), F32) for _ in range(nsub)]
        for p in range(N_HEADS // 2):
            qp = qi_ref[:, p * LANES:(p + 1) * LANES]
            for e in range(2):
                z = _dot(kz_ref[pl.ds(c0, KCH), e * LANES:(e + 1) * LANES], qp, NT)
                for s in range(nsub):
                    tot[s] = tot[s] + jnp.maximum(z[s * LANES:(s + 1) * LANES, :], 0.0) * w_t[2 * p + e]
        for s in range(nsub):
            causal = c0 + s * LANES + rowi <= t0 + lane
            key = jnp.where(causal, _sort_key(tot[s]), INT_MIN)
            sl = pl.ds(pl.multiple_of(c0 + s * LANES, LANES), LANES)
            hi_t[sl, :] = (key >> 16).astype(I16)
            lo_t[sl, :] = ((key & 0xFFFF) - 32768).astype(I16)
        return carry
    lax.fori_loop(0, n_ch, score_chunk, 0)

    pk = 16
    one16, zero16, min16 = jnp.ones((), I16), jnp.zeros((), I16), jnp.full((), MIN16, I16)

    def packed(v):
        return jnp.broadcast_to(v, (pk, QB)).astype(I16)[None]

    def chunk(arr, c):
        return arr[pl.ds(pl.multiple_of(c * KCH, KCH), KCH), :].reshape(KCH // pk, pk, QB)

    def count_ge(arr, cand):
        c16 = packed(cand)

        def body(c, acc):
            ones = jnp.where(chunk(arr, c) >= c16, one16, zero16)
            parts = [ones[j] for j in range(KCH // pk)]
            while len(parts) > 1:
                parts = [parts[a] + parts[a + 1] for a in range(0, len(parts), 2)]
            return acc + parts[0]
        acc = lax.fori_loop(0, n_ch, body, jnp.zeros((pk, QB), I16))
        return jnp.sum(acc.astype(I32), axis=0, keepdims=True)

    def count_gt(arr, v):
        return jnp.where(v >= 32767, 0, count_ge(arr, jnp.minimum(v + 1, 32767)))

    few = (t0 + lax.broadcasted_iota(I32, (1, QB), 1)) < top_k

    def search16(arr, want, all_count, unchecked_groups):
        def step(bit, carry):
            u, cu = carry
            cand = u | (jnp.int32(1) << (15 - bit))
            cnt = count_ge(arr, cand - 32768)
            ok = cnt >= want
            return jnp.where(ok, cand, u), jnp.where(ok, cnt, cu)

        def unsettled(carry):
            g, _, cu = carry
            return (g < 4) & (jnp.min(jnp.where(few | (cu == want), 1, 0)) == 0)

        def four_bits(carry):
            g, u, cu = carry
            u, cu = lax.fori_loop(4 * g, 4 * g + 4, step, (u, cu))
            return g + 1, u, cu
        u, cu = lax.fori_loop(0, 4 * unchecked_groups, step, (jnp.zeros((1, QB), I32), all_count))
        if unchecked_groups < 4:
            _, u, cu = lax.while_loop(unsettled, four_bits, (jnp.int32(unchecked_groups), u, cu))
        return u, cu

    want = jnp.full((1, QB), top_k, I32)
    u1, c1 = search16(hi_t, want, jnp.full((1, QB), n_ch * KCH, I32), 4)
    p_hi = u1 - 32768
    above = count_gt(hi_t, p_hi)
    want2 = want - above
    p16 = packed(p_hi)

    def keep_low(c, carry):
        sl = pl.ds(pl.multiple_of(c * KCH, KCH), KCH)
        lo_t[sl, :] = jnp.where(chunk(hi_t, c) == p16, chunk(lo_t, c), min16).reshape(KCH, QB)
        return carry
    lax.fori_loop(0, n_ch, keep_low, 0)

    u2, c2 = search16(lo_t, want2, c1 - above, 2)
    p_lo = jnp.maximum(u2 - 32768, jnp.where(p_hi == MIN16, MIN16 + 1, MIN16))
    has_ties = jnp.max(jnp.where(few, 0, c2 - want2)) > 0

    @pl.when(jnp.logical_not(has_ties))
    def _():
        l16 = packed(p_lo)

        def body(c, carry):
            h, l = chunk(hi_t, c), chunk(lo_t, c)
            sel = (h > p16) | ((h == p16) & (l >= l16))
            mb = jnp.where(sel, jnp.zeros((), BF16), jnp.full((), NEG, BF16)).reshape(KCH, QB)
            for s in range(nsub):
                sl = pl.ds(pl.multiple_of(c * KCH + s * LANES, LANES), LANES)
                mbias[:, sl] = mb[s * LANES:(s + 1) * LANES, :].astype(F32).T
            return carry
        lax.fori_loop(0, n_ch, body, 0)

    @pl.when(has_ties)
    def _():
        budget = (want2 - count_gt(lo_t, p_lo)).astype(F32)
        lt = jnp.where(rowi >= lane, 1.0, 0.0).astype(BF16)

        def body(j, before):
            sl = pl.ds(pl.multiple_of(j * LANES, LANES), LANES)
            h, l = hi_t[sl, :].astype(I32), lo_t[sl, :].astype(I32)
            eq = (h == p_hi) & (l == p_lo)
            eqf = jnp.where(eq, 1.0, 0.0)
            rank = before + _dot(lt, eqf.astype(BF16)) - eqf
            sel = (h > p_hi) | ((h == p_hi) & (l > p_lo)) | (eq & (rank < budget))
            mbias[:, sl] = jnp.where(sel, 0.0, NEG).T
            return before + jnp.sum(eqf, axis=0, keepdims=True)
        lax.fori_loop(0, n_ch * nsub, body, jnp.zeros((1, QB), F32))

    q = q_ref[...]
    even = lane < HEAD_DIM
    for p in range(N_HEADS // 2):
        qp = q[:, p * LANES:(p + 1) * LANES]
        qm[2 * p] = jnp.where(even, qp, jnp.zeros_like(qp))
        qm[2 * p + 1] = jnp.where(even, jnp.zeros_like(qp), qp)
    for h in range(N_HEADS):
        m_scr[h][...] = jnp.full((QB, LANES), NEG, F32)
        acc[h][...] = jnp.zeros((QB, LANES), F32)

    def key_block(c, near):
        start = pl.multiple_of(c * KCH, KCH)
        rows = pl.ds(start, KCH)
        nsub = KCH // LANES
        for h in range(N_HEADS):
            cols = slice((h // 2) * LANES, (h // 2 + 1) * LANES)
            s_scr[h] = _dot(qm[h], k_ref[rows, cols], NT)
        for h in range(N_HEADS):
            sub = []
            for n in range(nsub):
                sc = s_scr[h, :, n * LANES:(n + 1) * LANES] + mbias[:, pl.ds(start + n * LANES, LANES)]
                if near:
                    sc = sc + btab[h, jnp.clip(i - (c * nsub + n), 0, 2)]
                sub.append(sc)
            mx = jnp.maximum(jnp.maximum(sub[0], sub[1]), jnp.maximum(sub[2], sub[3]))
            m_old = m_scr[h][...]
            m_new = jnp.maximum(m_old, jnp.max(mx, axis=1, keepdims=True))
            for n in range(nsub):
                p_scr[h, :, n * LANES:(n + 1) * LANES] = jnp.exp2(sub[n] - m_new).astype(BF16)
            a_scr[h][...] = jnp.exp2(m_old - m_new)
            m_scr[h][...] = m_new
        for h in range(N_HEADS):
            cols = slice((h // 2) * LANES, (h // 2 + 1) * LANES)
            v_ref = vo_ref if h % 2 else ve_ref
            acc[h][...] = a_scr[h][...] * acc[h][...] + _dot(p_scr[h], v_ref[rows, cols])

    n_far = jnp.maximum(i - 1, 0) // (KCH // QB)

    def far_body(c, carry):
        key_block(c, False)
        return carry
    lax.fori_loop(0, n_far, far_body, 0)

    def near_body(c, carry):
        key_block(c, True)
        return carry
    lax.fori_loop(n_far, n_ch, near_body, 0)

    for p in range(N_HEADS // 2):
        ae, ao = acc[2 * p][...], acc[2 * p + 1][...]
        oe = ae / pltpu.roll(ae, HEAD_DIM, 1)
        oo = ao / pltpu.roll(ao, HEAD_DIM, 1)
        o_ref[:, p * LANES:(p + 1) * LANES] = jnp.where(even, oe, oo).astype(BF16)


def _attention(q, k, ve, vo, qi, kz, kw, rel_bias, B, S):
    nq = S // QB
    top_k = min(MAX_TOPK, S // 4)
    qrow = lambda n: pl.BlockSpec((QB, n), lambda b, i: (b * nq + i, 0))
    seq = lambda n: pl.BlockSpec((S, n), lambda b, i: (b, 0))
    return pl.pallas_call(
        functools.partial(_attn_kernel, top_k),
        grid=(B, nq),
        in_specs=[pl.BlockSpec(memory_space=pltpu.SMEM), qrow(HDIM), seq(HDIM), seq(HDIM), seq(HDIM),
                  qrow(HDIM), seq(2 * LANES), qrow(LANES)],
        out_specs=qrow(HDIM),
        out_shape=jax.ShapeDtypeStruct((B * S, HDIM), BF16),
        scratch_shapes=[pltpu.VMEM((S, QB), I16),
                        pltpu.VMEM((S, QB), I16),
                        pltpu.VMEM((QB, S), F32),
                        pltpu.VMEM((N_HEADS, 3, QB, LANES), F32),
                        pltpu.VMEM((N_HEADS, QB, LANES), BF16),
                        pltpu.VMEM((N_HEADS, QB, KCH), F32),
                        pltpu.VMEM((N_HEADS, QB, KCH), BF16)]
        + [pltpu.VMEM((QB, LANES), F32)] * (3 * N_HEADS),
        compiler_params=_params(("parallel", "arbitrary")),
        name="dsa_attention",
    )(rel_bias, q, k, ve, vo, qi, kz, kw)


def _layer_norm(x, g, b):
    mu = jnp.mean(x, axis=1, keepdims=True)
    d = x - mu
    var = jnp.mean(d * d, axis=1, keepdims=True)
    return d * lax.rsqrt(var + LN_EPS) * g + b


def _merge_kernel(x_ref, ya_ref, at_ref, wg_ref, wa_ref, wb_ref, wo_ref, g1_ref, b1_ref, wr_ref, br_ref,
                  h_ref, ri_ref, cnt_ref):
    @pl.when(pl.program_id(0) == 0)
    def _():
        cnt_ref[...] = jnp.zeros(cnt_ref.shape, F32)
    half = x_ref.shape[0] // 2
    halves = [slice(sb * half, (sb + 1) * half) for sb in range(2)]
    pre = []
    for rows in halves:
        g = _dot(x_ref[rows, :].astype(BF16), wg_ref[...])
        pre.append((g, _dot(ya_ref[rows, :], wa_ref[...]), _dot(at_ref[rows, :], wb_ref[...])))
    mixes = []
    for g, ya, yb in pre:
        mixin = _sigmoid(g[:, :D_MODEL]) * ya + _sigmoid(g[:, D_MODEL:]) * yb
        mixes.append(_dot(mixin.astype(BF16), wo_ref[...]))
    for rows, mix in zip(halves, mixes):
        h_ref[rows, :] = _layer_norm(ALPHA * x_ref[rows, :] + mix, g1_ref[...], b1_ref[...])
    for rows in halves:
        _route_rows(h_ref.at[rows], wr_ref, br_ref, ri_ref.at[rows], cnt_ref)


def _route_rows(h_ref, wr_ref, br_ref, ri_ref, cnt_ref):
    lg = _mm3(h_ref[...], wr_ref[...]) + br_ref[...]
    lane = lax.broadcasted_iota(I32, lg.shape, 1)
    gl = jnp.where(lane < N_GROUPS, lg, NEG)
    gmax = jnp.max(gl, axis=1, keepdims=True)
    p_g = 1.0 / jnp.sum(jnp.exp(gl - gmax), axis=1, keepdims=True)
    gsel = jnp.min(jnp.where(gl == gmax, lane, LANES), axis=1, keepdims=True)
    lo = N_GROUPS + EXPERTS_PER_GROUP * gsel
    el = jnp.where((lane >= lo) & (lane < lo + EXPERTS_PER_GROUP), lg, NEG)
    e1 = jnp.max(el, axis=1, keepdims=True)
    i1 = jnp.min(jnp.where(el == e1, lane, LANES), axis=1, keepdims=True)
    el2 = jnp.where(lane == i1, NEG, el)
    e2 = jnp.max(el2, axis=1, keepdims=True)
    i2 = jnp.min(jnp.where(el2 == e2, lane, LANES), axis=1, keepdims=True)
    w2 = jnp.exp(e2 - e1)
    gate1 = p_g / (1.0 + w2)
    gate2 = p_g * w2 / (1.0 + w2)
    tm = lg.shape[0]
    oh = jnp.concatenate([jnp.where(lane == i1 - N_GROUPS, 1.0, 0.0),
                          jnp.where(lane == i2 - N_GROUPS, 1.0, 0.0)], axis=0)
    rr = lax.broadcasted_iota(I32, (2 * tm, 2 * tm), 0)
    cc = lax.broadcasted_iota(I32, (2 * tm, 2 * tm), 1)
    before = _dot(jnp.where(rr > cc, 1.0, 0.0).astype(BF16), oh.astype(BF16))
    rank = jnp.sum((before + cnt_ref[0:1, :]) * oh, axis=1, keepdims=True)
    cnt_ref[...] = cnt_ref[...] + jnp.sum(oh, axis=0, keepdims=True)
    cols = [(i1 - N_GROUPS).astype(F32), (i2 - N_GROUPS).astype(F32), gate1, gate2, rank[:tm], rank[tm:]]
    ri = jnp.zeros(lg.shape, F32)
    for n, col in enumerate(cols):
        ri = jnp.where(lane == n, col, ri)
    ri_ref[...] = ri


def _merge(x2, ya, at, wg, wa, wb, wo, g1, b1, wr, br, tm):
    T = x2.shape[0]
    row = lambda n: pl.BlockSpec((tm, n), lambda i: (i, 0))
    ws = [wg, wa, wb, wo, g1, b1, wr, br]
    return pl.pallas_call(
        _merge_kernel,
        grid=(T // tm,),
        in_specs=[row(D_MODEL), row(HDIM), row(HDIM)] + [_full(w.shape) for w in ws],
        out_specs=[row(D_MODEL), row(LANES), _full((8, LANES))],
        out_shape=[jax.ShapeDtypeStruct((T, D_MODEL), F32), jax.ShapeDtypeStruct((T, LANES), F32),
                   jax.ShapeDtypeStruct((8, LANES), F32)],
        compiler_params=_params(("arbitrary",)),
        name="merge_router",
    )(x2, ya, at, *ws)


def _row_gather(idx_ref, n, src_hbm, dst, sem):
    def issue(r, carry):
        pltpu.make_async_copy(src_hbm.at[pl.ds(idx_ref[0, 0, r], 1)], dst.at[pl.ds(r, 1)], sem).start()
        return carry
    lax.fori_loop(0, n, issue, 0, unroll=8)
    pltpu.make_async_copy(src_hbm.at[pl.ds(0, n)], dst, sem).wait()


def _dispatch_kernel(p0_ref, p1_ref, h_ref, xs_in, xs_hbm, sem):
    del xs_in
    tm = h_ref.shape[0]

    def issue(r, carry):
        src = h_ref.at[pl.ds(r, 1)]
        pltpu.make_async_copy(src, xs_hbm.at[pl.ds(p0_ref[0, 0, r], 1)], sem).start()
        pltpu.make_async_copy(src, xs_hbm.at[pl.ds(p1_ref[0, 0, r], 1)], sem).start()
        return carry
    lax.fori_loop(0, tm, issue, 0, unroll=8)
    for _ in range(2):
        pltpu.make_async_copy(h_ref, xs_hbm.at[pl.ds(0, tm)], sem).wait()


def _dispatch(pos0, pos1, h, n_rows, tm):
    T = h.shape[0]
    idx = pl.BlockSpec((1, 1, tm), lambda i: (i, 0, 0), memory_space=pltpu.SMEM)
    return pl.pallas_call(
        _dispatch_kernel,
        grid=(T // tm,),
        in_specs=[idx, idx, pl.BlockSpec((tm, D_MODEL), lambda i: (i, 0)), pl.BlockSpec(memory_space=pl.ANY)],
        out_specs=pl.BlockSpec(memory_space=pl.ANY),
        out_shape=jax.ShapeDtypeStruct((n_rows, D_MODEL), F32),
        scratch_shapes=[pltpu.SemaphoreType.DMA(())],
        input_output_aliases={3: 0},
        compiler_params=_params(("arbitrary",)),
        name="moe_dispatch",
    )(pos0, pos1, h, jnp.zeros((n_rows, D_MODEL), F32))


def _moe_kernel(te_ref, nu_ref, xs_ref, wg_ref, wu_ref, wd_ref, o_ref):
    t = pl.program_id(0)

    @pl.when(t < nu_ref[0])
    def _():
        xb = xs_ref[...].astype(BF16)
        hg = _dot(xb, wg_ref[0])
        hu = _dot(xb, wu_ref[0])
        act = (hg * _sigmoid(hg)) * hu
        o_ref[...] = _dot(act.astype(BF16), wd_ref[0])

    @pl.when(t >= nu_ref[0])
    def _():
        o_ref[...] = jnp.zeros(o_ref.shape, F32)


def _moe(tile_e, n_used, xs, wg, wu, wd, tm):
    n_tiles = xs.shape[0] // tm
    grid_spec = pltpu.PrefetchScalarGridSpec(
        num_scalar_prefetch=2,
        grid=(n_tiles,),
        in_specs=[pl.BlockSpec((tm, D_MODEL), lambda t, te, nu: (t, 0)),
                  pl.BlockSpec((1, D_MODEL, D_EXPERT), lambda t, te, nu: (te[t], 0, 0)),
                  pl.BlockSpec((1, D_MODEL, D_EXPERT), lambda t, te, nu: (te[t], 0, 0)),
                  pl.BlockSpec((1, D_EXPERT, D_MODEL), lambda t, te, nu: (te[t], 0, 0))],
        out_specs=pl.BlockSpec((tm, D_MODEL), lambda t, te, nu: (t, 0)),
    )
    return pl.pallas_call(
        _moe_kernel,
        grid_spec=grid_spec,
        out_shape=jax.ShapeDtypeStruct((n_tiles * tm, D_MODEL), F32),
        compiler_params=_params(("arbitrary",)),
        name="moe_experts",
    )(tile_e, n_used, xs, wg, wu, wd)


def _final_kernel(p0_ref, p1_ref, n0_ref, n1_ref, h_ref, ri_ref, ys_hbm, g2_ref, b2_ref, o_ref, y0, y1, sems):
    i = pl.program_id(0)
    tm = h_ref.shape[0]
    slot = i % 2

    def issue(idx0, idx1, buf):
        def body(r, carry):
            pltpu.make_async_copy(ys_hbm.at[pl.ds(idx0[0, 0, r], 1)], y0.at[buf, pl.ds(r, 1)], sems.at[buf]).start()
            pltpu.make_async_copy(ys_hbm.at[pl.ds(idx1[0, 0, r], 1)], y1.at[buf, pl.ds(r, 1)], sems.at[buf]).start()
            return carry
        lax.fori_loop(0, tm, body, 0, unroll=8)

    @pl.when(i == 0)
    def _():
        issue(p0_ref, p1_ref, 0)

    @pl.when(i + 1 < pl.num_programs(0))
    def _():
        issue(n0_ref, n1_ref, 1 - slot)

    pltpu.make_async_copy(ys_hbm.at[pl.ds(0, tm)], y0.at[slot], sems.at[slot]).wait()
    pltpu.make_async_copy(ys_hbm.at[pl.ds(0, tm)], y1.at[slot], sems.at[slot]).wait()
    ri = ri_ref[...]
    moe = y0[slot] * ri[:, 2:3] + y1[slot] * ri[:, 3:4]
    o_ref[...] = _layer_norm(ALPHA * h_ref[...] + moe, g2_ref[...], b2_ref[...])


def _final(pos0, pos1, h, ri, ys, g2, b2, tm):
    T = h.shape[0]
    n = T // tm
    idx = pl.BlockSpec((1, 1, tm), lambda i: (i, 0, 0), memory_space=pltpu.SMEM)
    nxt = pl.BlockSpec((1, 1, tm), lambda i: (jnp.minimum(i + 1, n - 1), 0, 0), memory_space=pltpu.SMEM)
    row = lambda c: pl.BlockSpec((tm, c), lambda i: (i, 0))
    return pl.pallas_call(
        _final_kernel,
        grid=(n,),
        in_specs=[idx, idx, nxt, nxt, row(D_MODEL), row(LANES), pl.BlockSpec(memory_space=pl.ANY),
                  _full(g2.shape), _full(b2.shape)],
        out_specs=row(D_MODEL),
        out_shape=jax.ShapeDtypeStruct((T, D_MODEL), F32),
        scratch_shapes=[pltpu.VMEM((2, tm, D_MODEL), F32), pltpu.VMEM((2, tm, D_MODEL), F32),
                        pltpu.SemaphoreType.DMA((2,))],
        compiler_params=_params(("arbitrary",)),
        name="combine_ln",
    )(pos0, pos1, pos0, pos1, h, ri, ys, g2, b2)


def _routing_tables(eid, rank, sizes, tm):
    T = eid.shape[0]
    n_tiles = (2 * T) // tm + N_EXPERTS
    padded = ((sizes + tm - 1) // tm) * tm
    pad_end = jnp.cumsum(padded)
    pad_off = pad_end - padded
    pos = jnp.take(pad_off, eid) + rank
    tile_start = jnp.arange(n_tiles, dtype=I32) * tm
    tile_e = jnp.minimum(jnp.sum(tile_start[:, None] >= pad_end[None, :], axis=1), N_EXPERTS - 1).astype(I32)
    n_used = (pad_end[-1] // tm).astype(I32).reshape(1)
    return tile_e, n_used, pos, n_tiles * tm


def _block_diag_ones():
    hid = jnp.arange(HDIM, dtype=I32) // HEAD_DIM
    return (hid[:, None] == hid[None, :]).astype(BF16)


def _layer(x, w_in, mu_shift, w0, w_lora_up, a0, a_lora_up, g_lora_up, k_k, k_a, r_k, ln_x_g, ln_x_b,
           w_branch_a, idx_k_ln_g, idx_k_ln_b, rel_bias, w_branch_b, w_out, ln1_g, ln1_b,
           w_router_grp, b_router_grp, w_router_exp, b_router_exp, w_gate, w_up, w_down, ln2_g, ln2_b):
    B, S, _ = x.shape
    T = B * S
    x2 = x.reshape(T, D_MODEL)
    row = lambda t: t.reshape(1, -1)

    c_r, c_wd, c_k, c_v, c_ad, c_gd = 0, 512, 576, 1088, 1600, 1664
    perm = jnp.concatenate([jnp.arange(c_r, c_r + 512), jnp.arange(c_k, c_k + 512), jnp.arange(c_v, c_v + 512),
                            jnp.arange(c_wd, c_wd + 64), jnp.arange(c_ad, c_ad + 64),
                            jnp.arange(c_gd, c_gd + 128)])
    o_q = RWKV_COLS
    o_c = o_q + 4 * HDIM
    o_g = o_c + IDX_DIM + N_HEADS
    wr = w_in[:, :RWKV_COLS][:, perm].astype(BF16)
    wq = w_in[:, o_q:o_c].astype(BF16)
    wc = jnp.pad(w_in[:, o_c:o_g], ((0, 0), (0, LANES - IDX_DIM - N_HEADS))).astype(BF16)
    wgates = w_in[:, o_g:].astype(BF16)
    pad_idx = lambda t: jnp.pad(t, (0, LANES - IDX_DIM)).reshape(1, LANES)

    pr, q, k, ve, vo, qi, kz, kw = _inproj(x2, wr, wq, wc, pad_idx(idx_k_ln_g), pad_idx(idx_k_ln_b), tm=512)

    bd = _block_diag_ones()
    prm = {
        "mu": row(mu_shift[perm]), "w0": row(w0), "a0": row(a0), "k_k": row(k_k), "k_a": row(k_a),
        "r_k": row(r_k),
        "wup": jnp.pad(w_lora_up, ((0, AAA_LORA), (0, 0))),
        "aup": jnp.pad(a_lora_up, ((DECAY_LORA, 0), (0, 0))),
        "gup": g_lora_up, "bd": bd,
    }
    arrs = _rwkv_prep(pr, S, prm, tm=256)
    ya = _rwkv_scan(arrs, row(ln_x_g), row(ln_x_b), bd, B, S)

    at = _attention(q, k, ve, vo, qi, kz, kw, rel_bias, B, S)

    w_router = jnp.pad(jnp.concatenate([w_router_grp, w_router_exp], axis=1),
                       ((0, 0), (0, LANES - N_GROUPS - N_EXPERTS)))
    b_router = jnp.pad(jnp.concatenate([b_router_grp, b_router_exp]), (0, LANES - N_GROUPS - N_EXPERTS))
    h1, ri, cnt = _merge(x2, ya, at, wgates, w_branch_a.astype(BF16), w_branch_b.astype(BF16),
                         w_out.astype(BF16), row(ln1_g), row(ln1_b), w_router, row(b_router), tm=512)

    tm_e = 512
    tile_e, n_used, pos, n_rows = _routing_tables(ri[:, 0:2].astype(I32), ri[:, 4:6].astype(I32),
                                                  cnt[0, :N_EXPERTS].astype(I32), tm_e)
    tm_f = 256
    pos0 = pos[:, 0].reshape(T // tm_f, 1, tm_f)
    pos1 = pos[:, 1].reshape(T // tm_f, 1, tm_f)
    xs = _dispatch(pos0, pos1, h1, n_rows, tm_f)
    ys = _moe(tile_e, n_used, xs, w_gate.astype(BF16), w_up.astype(BF16), w_down.astype(BF16), tm_e)
    out = _final(pos0, pos1, h1, ri, ys, row(ln2_g), row(ln2_b), tm_f)
    return out.reshape(B, S, D_MODEL)


def kernel(x, w_in, mu_shift, w0, w_lora_up, a0, a_lora_up, g_lora_up, k_k, k_a, r_k, ln_x_g, ln_x_b, w_branch_a, idx_k_ln_g, idx_k_ln_b, rel_bias, w_branch_b, w_out, ln1_g, ln1_b, w_router_grp, b_router_grp, w_router_exp, b_router_exp, w_expert_gate, w_expert_up, w_expert_down, ln2_g, ln2_b):
    assert w_in.shape[0] == 1, "single-layer (DEPTH = 1) block"
    l = 0
    return _layer(x, w_in[l], mu_shift[l], w0[l], w_lora_up[l], a0[l], a_lora_up[l], g_lora_up[l], k_k[l],
                  k_a[l], r_k[l], ln_x_g[l], ln_x_b[l], w_branch_a[l], idx_k_ln_g[l], idx_k_ln_b[l], rel_bias,
                  w_branch_b[l], w_out[l], ln1_g[l], ln1_b[l], w_router_grp[l], b_router_grp[l],
                  w_router_exp[l], b_router_exp[l], w_expert_gate[l], w_expert_up[l], w_expert_down[l],
                  ln2_g[l], ln2_b[l])
```

```python
import functools
import math

import jax
import jax.numpy as jnp
from jax import lax
from jax.experimental import pallas as pl
from jax.experimental.pallas import tpu as pltpu

F32 = jnp.float32
BF16 = jnp.bfloat16
I32 = jnp.int32

D_MODEL = 1024
HEAD_DIM = 64
N_HEADS = 8
HDIM = N_HEADS * HEAD_DIM
DECAY_LORA = 64
AAA_LORA = 64
GATE_LORA = 128
RWKV_COLS = 3 * HDIM + DECAY_LORA + AAA_LORA + GATE_LORA
IDX_DIM = 64
MAX_TOPK = 256
N_BUCKETS = 32
MAX_EXACT = 16
MAX_DISTANCE = 128
N_GROUPS = 4
EXPERTS_PER_GROUP = 8
N_EXPERTS = 32
D_EXPERT = 512
GN_EPS = 64e-5
LN_EPS = 1e-5
ALPHA = 2.0 ** 0.25
LANES = 128
CHUNK = 64
SCAN_SUB = 4
QB = 128
KCH = 512
INT_MIN = -(2 ** 31)
NEG = -1e30
LOG2E = 1.4426950408889634
VMEM_LIMIT = 56 * 1024 * 1024

NN = (((1,), (0,)), ((), ()))
NT = (((1,), (1,)), ((), ()))


def _dot(a, b, dims=NN):
    return lax.dot_general(a, b, dims, preferred_element_type=F32)


def _split2(x):
    hi = x.astype(BF16)
    lo = (x - hi.astype(F32)).astype(BF16)
    return hi, lo


def _split3(x):
    hi = x.astype(BF16)
    r1 = x - hi.astype(F32)
    mid = r1.astype(BF16)
    lo = (r1 - mid.astype(F32)).astype(BF16)
    return hi, mid, lo


def _mm1(a, b, dims=NN):
    return _dot(a.astype(BF16), b.astype(BF16), dims)


def _mm3(a, b, dims=NN):
    ah, al = _split2(a)
    bh, bl = _split2(b)
    return _dot(ah, bh, dims) + (_dot(ah, bl, dims) + _dot(al, bh, dims))


_mm_misc = _mm1
_mm_inv = _mm1
_mm_state = _mm3


def _mm_exact_lhs(a_bf, b):
    b0, b1, b2 = _split3(b)
    return _dot(a_bf, b0) + (_dot(a_bf, b1) + _dot(a_bf, b2))


def _mm_exact_rhs(a, b_bf):
    a0, a1, a2 = _split3(a)
    return _dot(a0, b_bf) + (_dot(a1, b_bf) + _dot(a2, b_bf))


def _params(sem, vmem=VMEM_LIMIT):
    return pltpu.CompilerParams(dimension_semantics=sem, vmem_limit_bytes=vmem)


def _full(shape):
    nd = len(shape)
    return pl.BlockSpec(shape, lambda *_: (0,) * nd)


def _inproj_kernel(x_ref, wr_ref, wq_ref, wc_ref, lng_ref, lnb_ref,
                   pr_ref, q_ref, k_ref, ve_ref, vo_ref, qi_ref, kz_ref, kw_ref):
    xb = x_ref[...].astype(BF16)
    pr_ref[...] = _dot(xb, wr_ref[...])
    qkv = _dot(xb, wq_ref[...])
    q_ref[...] = (qkv[:, 0:HDIM] * (HEAD_DIM ** -0.5 * LOG2E)).astype(BF16)
    k_ref[...] = qkv[:, HDIM:2 * HDIM].astype(BF16)
    v = qkv[:, 2 * HDIM:3 * HDIM]
    even = (lax.broadcasted_iota(I32, v.shape, 1) % LANES) < HEAD_DIM
    ve_ref[...] = jnp.where(even, v, 1.0).astype(BF16)
    vo_ref[...] = jnp.where(even, 1.0, v).astype(BF16)
    qi_ref[...] = qkv[:, 3 * HDIM:4 * HDIM].astype(BF16)
    c = _dot(xb, wc_ref[...])
    kw_ref[...] = c
    lane = lax.broadcasted_iota(I32, c.shape, 1)
    isk = lane < IDX_DIM
    mu = jnp.sum(jnp.where(isk, c, 0.0), axis=1, keepdims=True) * (1.0 / IDX_DIM)
    d = jnp.where(isk, c - mu, 0.0)
    var = jnp.sum(d * d, axis=1, keepdims=True) * (1.0 / IDX_DIM)
    kn = d * lax.rsqrt(var + LN_EPS) * lng_ref[...] + lnb_ref[...]
    kz_ref[:, 0:LANES] = kn.astype(BF16)
    kz_ref[:, LANES:2 * LANES] = pltpu.roll(kn, IDX_DIM, 1).astype(BF16)


def _inproj(x2, wr, wq, wc, lng, lnb, tm):
    T = x2.shape[0]
    row = lambda n: pl.BlockSpec((tm, n), lambda i: (i, 0))
    return pl.pallas_call(
        _inproj_kernel,
        grid=(T // tm,),
        in_specs=[row(D_MODEL), _full(wr.shape), _full(wq.shape), _full(wc.shape),
                  _full(lng.shape), _full(lnb.shape)],
        out_specs=[row(RWKV_COLS), row(HDIM), row(HDIM), row(HDIM), row(HDIM), row(HDIM),
                   row(2 * LANES), row(LANES)],
        out_shape=[jax.ShapeDtypeStruct((T, RWKV_COLS), F32)]
        + [jax.ShapeDtypeStruct((T, HDIM), BF16)] * 5
        + [jax.ShapeDtypeStruct((T, 2 * LANES), BF16), jax.ShapeDtypeStruct((T, LANES), F32)],
        compiler_params=_params(("parallel",)),
        name="inproj",
    )(x2, wr, wq, wc, lng, lnb)


def _softplus(x):
    return jnp.maximum(x, 0.0) + jnp.log(1.0 + jnp.exp(-jnp.abs(x)))


def _sigmoid(x):
    return 1.0 / (1.0 + jnp.exp(-x))


def _prep_kernel(tiles_per_seq, p_ref, pp_ref, mu_ref, w0_ref, a0_ref, kk_ref, ka_ref, rk_ref,
                 wup_ref, aup_ref, gup_ref, bd_ref,
                 r_ref, lw_ref, k_ref, v_ref, a_ref, b_ref, g_ref, bon_ref):
    i = pl.program_id(0)
    p = p_ref[...]
    tm = p.shape[0]
    first = (i % tiles_per_seq) == 0
    prow = jnp.where(first, 0.0, pp_ref[7:8, :])
    rowid = lax.broadcasted_iota(I32, p.shape, 0)
    prev = jnp.where(rowid == 0, prow, pltpu.roll(p, 1, 0))
    ps = p + (prev - p) * mu_ref[...]
    r = ps[:, 0:HDIM]
    k = ps[:, HDIM:2 * HDIM]
    v = ps[:, 2 * HDIM:3 * HDIM]
    da = ps[:, 3 * HDIM:3 * HDIM + LANES]
    gd = ps[:, 3 * HDIM + LANES:3 * HDIM + 2 * LANES]
    w = -_softplus(-(w0_ref[...] + _mm3(jnp.tanh(da), wup_ref[...]))) - 0.5
    lw_ref[...] = -jnp.exp(w)
    a = _sigmoid(a0_ref[...] + _mm3(da, aup_ref[...]))
    g_ref[...] = _mm3(_sigmoid(gd), gup_ref[...])
    bd = bd_ref[...]
    kk = k * kk_ref[...]
    ss = _mm_exact_rhs(kk * kk, bd)
    kk = kk / jnp.maximum(jnp.sqrt(ss), 1e-12)
    k2 = k * (1.0 + (a - 1.0) * ka_ref[...])
    r_ref[...] = r
    k_ref[...] = k2
    v_ref[...] = v
    a_ref[...] = -kk
    b_ref[...] = kk * a
    bon_ref[...] = _mm_exact_rhs(r * k2 * rk_ref[...], bd) * v


def _rwkv_prep(pr, S, prm, tm):
    T = pr.shape[0]
    row = lambda n: pl.BlockSpec((tm, n), lambda i: (i, 0))
    prev = pl.BlockSpec((8, RWKV_COLS), lambda i: (jnp.maximum(i * (tm // 8) - 1, 0), 0))
    names = ["mu", "w0", "a0", "k_k", "k_a", "r_k", "wup", "aup", "gup", "bd"]
    return pl.pallas_call(
        functools.partial(_prep_kernel, S // tm),
        grid=(T // tm,),
        in_specs=[row(RWKV_COLS), prev] + [_full(prm[n].shape) for n in names],
        out_specs=[row(HDIM)] * 8,
        out_shape=[jax.ShapeDtypeStruct((T, HDIM), F32)] * 8,
        compiler_params=_params(("parallel",)),
        name="rwkv_prep",
    )(pr, pr, *[prm[n] for n in names])


def _scan_kernel(r_ref, lw_ref, k_ref, v_ref, a_ref, b_ref, g_ref, bon_ref, lng_ref, lnb_ref, bd_ref,
                 o_ref, st_ref, y_ref):
    C, N, H = CHUNK, HEAD_DIM, N_HEADS

    @pl.when(pl.program_id(1) == 0)
    def _():
        st_ref[...] = jnp.zeros(st_ref.shape, F32)

    ri = lax.broadcasted_iota(I32, (C, C), 0)
    ci = lax.broadcasted_iota(I32, (C, C), 1)
    incl = ri >= ci
    strict = ri > ci
    eye = ri == ci
    eye_f = jnp.where(eye, 1.0, 0.0)
    lmat = jnp.where(incl, 1.0, 0.0).astype(BF16)
    sls = [slice(h * N, (h + 1) * N) for h in range(H)]
    units = [(s, h) for s in range(SCAN_SUB) for h in range(H)]
    ah, rh, vh, bT, kT, bhT, khT, gam = {}, {}, {}, {}, {}, {}, {}, {}
    for s in range(SCAN_SUB):
        rows = slice(s * C, (s + 1) * C)
        lw = lw_ref[rows, :]
        cum = _mm_exact_lhs(lmat, lw)
        last = cum[C - 1:C, :]
        e_i = jnp.exp(-cum)
        e_end = jnp.exp(last - cum)
        g_s = jnp.exp(last)
        r_t = r_ref[rows, :] * jnp.exp(cum)
        a_t = a_ref[rows, :] * jnp.exp(cum - lw)
        v_s = v_ref[rows, :]
        b_s, k_s = b_ref[rows, :], k_ref[rows, :]
        bT_s, kT_s = (b_s * e_i).T, (k_s * e_i).T
        bhT_s, khT_s = (b_s * e_end).T, (k_s * e_end).T
        for h in range(H):
            un = (s, h)
            ah[un], rh[un], vh[un] = a_t[:, sls[h]], r_t[:, sls[h]], v_s[:, sls[h]]
            bT[un], kT[un] = bT_s[sls[h], :], kT_s[sls[h], :]
            bhT[un], khT[un] = bhT_s[sls[h], :], khT_s[sls[h], :]
            gam[un] = g_s[:, sls[h]]

    ar = {un: jnp.concatenate([ah[un], rh[un]], axis=0) for un in units}
    sb = {un: _mm_misc(ar[un], bT[un]) for un in units}
    sk = {un: _mm_misc(ar[un], kT[un]) for un in units}
    a_ab = {un: jnp.where(strict, sb[un][:C], 0.0) for un in units}
    a_rb = {un: jnp.where(incl, sb[un][C:], 0.0) for un in units}
    a_ak = {un: jnp.where(strict, sk[un][:C], 0.0) for un in units}
    a_rk = {un: jnp.where(incl, sk[un][C:], 0.0) for un in units}
    u = {un: _mm_misc(a_ak[un], vh[un]) for un in units}
    tinv = {un: eye_f + a_ab[un] for un in units}
    xp = a_ab
    for _ in range(5):
        xp = {un: _mm_inv(xp[un], xp[un]) for un in units}
        tinv = {un: tinv[un] + _mm_inv(tinv[un], xp[un]) for un in units}
    pm = {un: _mm_misc(tinv[un], ah[un]) for un in units}
    qm = {un: _mm_misc(tinv[un], u[un]) for un in units}
    r2 = {un: rh[un] + _mm_misc(a_rb[un], pm[un]) for un in units}
    mmat = {un: jnp.where(eye, gam[un], 0.0) + _mm_misc(bhT[un], pm[un]) for un in units}
    y0 = {un: _mm_misc(a_rb[un], qm[un]) + _mm_misc(a_rk[un], vh[un]) for un in units}
    gmat = {un: _mm_misc(bhT[un], qm[un]) + _mm_misc(khT[un], vh[un]) for un in units}
    st = [st_ref[h] for h in range(H)]
    for s in range(SCAN_SUB):
        for h in range(H):
            y_ref[s * C:(s + 1) * C, sls[h]] = _mm_state(r2[(s, h)], st[h]) + y0[(s, h)]
        st = [_mm_state(mmat[(s, h)], st[h]) + gmat[(s, h)] for h in range(H)]
    for h in range(H):
        st_ref[h] = st[h]

    y = y_ref[...]
    bd = bd_ref[...]
    mu = _mm_exact_rhs(y, bd) * (1.0 / N)
    d = y - mu
    var = _mm_exact_rhs(d * d, bd) * (1.0 / N)
    yn = d * lax.rsqrt(var + GN_EPS) * lng_ref[...] + lnb_ref[...]
    o_ref[...] = ((yn + bon_ref[...]) * g_ref[...]).astype(BF16)


def _rwkv_scan(arrs, lng, lnb, bd, B, S):
    rows = SCAN_SUB * CHUNK
    nc = S // rows
    row = pl.BlockSpec((rows, HDIM), lambda b, c: (b * nc + c, 0))
    return pl.pallas_call(
        _scan_kernel,
        grid=(B, nc),
        in_specs=[row] * 8 + [_full(lng.shape), _full(lnb.shape), _full(bd.shape)],
        out_specs=row,
        out_shape=jax.ShapeDtypeStruct((B * S, HDIM), BF16),
        scratch_shapes=[pltpu.VMEM((N_HEADS, HEAD_DIM, HEAD_DIM), F32), pltpu.VMEM((rows, HDIM), F32)],
        compiler_params=_params(("parallel", "arbitrary")),
        name="rwkv_scan",
    )(*arrs, lng, lnb, bd)


def _sort_key(x):
    bits = pltpu.bitcast(x, I32)
    return bits ^ ((bits >> 31) & 0x7FFFFFFF)


def _attn_kernel(top_k, rb_ref, q_ref, k_ref, ve_ref, vo_ref, qi_ref, kz_ref, kw_ref, o_ref,
                 key_t, mbias, btab, qm, s_scr, p_scr, *state):
    m_scr, acc, a_scr = (state[n * N_HEADS:(n + 1) * N_HEADS] for n in range(3))
    i = pl.program_id(1)
    t0 = i * QB
    lane = lax.broadcasted_iota(I32, (QB, LANES), 1)
    rowi = lax.broadcasted_iota(I32, (QB, LANES), 0)

    @pl.when(i == 0)
    def _build_bias():
        for h in range(N_HEADS):
            btab[h, 2] = jnp.zeros((QB, LANES), F32)
        for m in range(2):
            n = jnp.maximum(m * QB + rowi - lane, 0)
            nf = jnp.maximum(n, 1).astype(F32)
            large = MAX_EXACT + (jnp.log(nf / MAX_EXACT) / math.log(MAX_DISTANCE / MAX_EXACT)
                                 * (N_BUCKETS - MAX_EXACT)).astype(I32)
            bucket = jnp.where(n < MAX_EXACT, n, jnp.minimum(large, N_BUCKETS - 1))
            for h in range(N_HEADS):
                t = jnp.zeros((QB, LANES), F32)
                for bk in range(N_BUCKETS):
                    t = jnp.where(bucket == bk, rb_ref[bk, h], t)
                btab[h, m] = (t - rb_ref[N_BUCKETS - 1, h]) * LOG2E

    n_ch = i // (KCH // QB) + 1
    nsub = KCH // LANES

    kw_t = kw_ref[...].T
    w_t = [kw_t[IDX_DIM + h:IDX_DIM + h + 1, :] * (N_HEADS ** -0.5) * (IDX_DIM ** -0.5) for h in range(N_HEADS)]

    def score_chunk(c, carry):
        c0 = pl.multiple_of(c * KCH, KCH)
        tot = [jnp.zeros((LANES, QB), F32) for _ in range(nsub)]
        for p in range(N_HEADS // 2):
            qp = qi_ref[:, p * LANES:(p + 1) * LANES]
            for e in range(2):
                z = _dot(kz_ref[pl.ds(c0, KCH), e * LANES:(e + 1) * LANES], qp, NT)
                for s in range(nsub):
                    tot[s] = tot[s] + jnp.maximum(z[s * LANES:(s + 1) * LANES, :], 0.0) * w_t[2 * p + e]
        for s in range(nsub):
            causal = c0 + s * LANES + rowi <= t0 + lane
            key_t[pl.ds(pl.multiple_of(c0 + s * LANES, LANES), LANES), :] = \
                jnp.where(causal, _sort_key(tot[s]), INT_MIN)
        return carry
    lax.fori_loop(0, n_ch, score_chunk, 0)

    def count_ge(cand):
        def body(c, cnt):
            kc = key_t[pl.ds(pl.multiple_of(c * KCH, KCH), KCH), :]
            ones = jnp.where(kc >= cand, 1, 0)
            return cnt + jnp.sum(ones.reshape(KCH // 8, 8, QB), axis=0)
        cnt = lax.fori_loop(0, n_ch, body, jnp.zeros((8, QB), I32))
        return jnp.sum(cnt, axis=0, keepdims=True)

    def search(bit, carry):
        u, cu = carry
        cand = u | (jnp.int32(1) << (31 - bit))
        cnt = count_ge(cand ^ INT_MIN)
        ok = cnt >= top_k
        return jnp.where(ok, cand, u), jnp.where(ok, cnt, cu)

    few = (t0 + lax.broadcasted_iota(I32, (1, QB), 1)) < top_k

    def unsettled(carry):
        g, _, cu = carry
        return (g < 8) & (jnp.min(jnp.where(few | (cu == top_k), 1, 0)) == 0)

    def four_bits(carry):
        g, u, cu = carry
        u, cu = lax.fori_loop(4 * g, 4 * g + 4, search, (u, cu))
        return g + 1, u, cu

    u, cu = lax.fori_loop(0, 20, search, (jnp.zeros((1, QB), I32), jnp.zeros((1, QB), I32)))
    _, u, cu = lax.while_loop(unsettled, four_bits, (jnp.int32(5), u, cu))
    thr = jnp.maximum(u ^ INT_MIN, INT_MIN + 1)

    @pl.when(jnp.max(cu) > top_k)
    def _fix_ties():
        budget = (top_k - count_ge(thr + 1)).astype(F32)
        lt = jnp.where(rowi >= lane, 1.0, 0.0).astype(BF16)

        def body(j, before):
            sl = pl.ds(pl.multiple_of(j * LANES, LANES), LANES)
            kc = key_t[sl, :]
            eq = kc == thr
            eqf = jnp.where(eq, 1.0, 0.0)
            rank = before + _dot(lt, eqf.astype(BF16)) - eqf
            key_t[sl, :] = jnp.where(eq & (rank >= budget), INT_MIN, kc)
            return before + jnp.sum(eqf, axis=0, keepdims=True)
        lax.fori_loop(0, i + 1, body, jnp.zeros((1, QB), F32))

    def mask_chunk(c, carry):
        c0 = pl.multiple_of(c * KCH, KCH)
        for s in range(nsub):
            sl = pl.ds(pl.multiple_of(c0 + s * LANES, LANES), LANES)
            mbias[:, sl] = jnp.where(key_t[sl, :] >= thr, 0.0, NEG).T
        return carry
    lax.fori_loop(0, n_ch, mask_chunk, 0)

    q = q_ref[...]
    even = lane < HEAD_DIM
    for p in range(N_HEADS // 2):
        qp = q[:, p * LANES:(p + 1) * LANES]
        qm[2 * p] = jnp.where(even, qp, jnp.zeros_like(qp))
        qm[2 * p + 1] = jnp.where(even, jnp.zeros_like(qp), qp)
    for h in range(N_HEADS):
        m_scr[h][...] = jnp.full((QB, LANES), NEG, F32)
        acc[h][...] = jnp.zeros((QB, LANES), F32)

    def key_block(c, near):
        start = pl.multiple_of(c * KCH, KCH)
        rows = pl.ds(start, KCH)
        nsub = KCH // LANES
        for h in range(N_HEADS):
            cols = slice((h // 2) * LANES, (h // 2 + 1) * LANES)
            s_scr[h] = _dot(qm[h], k_ref[rows, cols], NT)
        for h in range(N_HEADS):
            sub = []
            for n in range(nsub):
                sc = s_scr[h, :, n * LANES:(n + 1) * LANES] + mbias[:, pl.ds(start + n * LANES, LANES)]
                if near:
                    sc = sc + btab[h, jnp.clip(i - (c * nsub + n), 0, 2)]
                sub.append(sc)
            mx = jnp.maximum(jnp.maximum(sub[0], sub[1]), jnp.maximum(sub[2], sub[3]))
            m_old = m_scr[h][...]
            m_new = jnp.maximum(m_old, jnp.max(mx, axis=1, keepdims=True))
            for n in range(nsub):
                p_scr[h, :, n * LANES:(n + 1) * LANES] = jnp.exp2(sub[n] - m_new).astype(BF16)
            a_scr[h][...] = jnp.exp2(m_old - m_new)
            m_scr[h][...] = m_new
        for h in range(N_HEADS):
            cols = slice((h // 2) * LANES, (h // 2 + 1) * LANES)
            v_ref = vo_ref if h % 2 else ve_ref
            acc[h][...] = a_scr[h][...] * acc[h][...] + _dot(p_scr[h], v_ref[rows, cols])

    n_far = jnp.maximum(i - 1, 0) // (KCH // QB)

    def far_body(c, carry):
        key_block(c, False)
        return carry
    lax.fori_loop(0, n_far, far_body, 0)

    def near_body(c, carry):
        key_block(c, True)
        return carry
    lax.fori_loop(n_far, n_ch, near_body, 0)

    for p in range(N_HEADS // 2):
        ae, ao = acc[2 * p][...], acc[2 * p + 1][...]
        oe = ae / pltpu.roll(ae, HEAD_DIM, 1)
        oo = ao / pltpu.roll(ao, HEAD_DIM, 1)
        o_ref[:, p * LANES:(p + 1) * LANES] = jnp.where(even, oe, oo).astype(BF16)


def _attention(q, k, ve, vo, qi, kz, kw, rel_bias, B, S):
    nq = S // QB
    top_k = min(MAX_TOPK, S // 4)
    qrow = lambda n: pl.BlockSpec((QB, n), lambda b, i: (b * nq + i, 0))
    seq = lambda n: pl.BlockSpec((S, n), lambda b, i: (b, 0))
    return pl.pallas_call(
        functools.partial(_attn_kernel, top_k),
        grid=(B, nq),
        in_specs=[pl.BlockSpec(memory_space=pltpu.SMEM), qrow(HDIM), seq(HDIM), seq(HDIM), seq(HDIM),
                  qrow(HDIM), seq(2 * LANES), qrow(LANES)],
        out_specs=qrow(HDIM),
        out_shape=jax.ShapeDtypeStruct((B * S, HDIM), BF16),
        scratch_shapes=[pltpu.VMEM((S, QB), I32),
                        pltpu.VMEM((QB, S), F32),
                        pltpu.VMEM((N_HEADS, 3, QB, LANES), F32),
                        pltpu.VMEM((N_HEADS, QB, LANES), BF16),
                        pltpu.VMEM((N_HEADS, QB, KCH), F32),
                        pltpu.VMEM((N_HEADS, QB, KCH), BF16)]
        + [pltpu.VMEM((QB, LANES), F32)] * (3 * N_HEADS),
        compiler_params=_params(("parallel", "arbitrary")),
        name="dsa_attention",
    )(rel_bias, q, k, ve, vo, qi, kz, kw)


def _layer_norm(x, g, b):
    mu = jnp.mean(x, axis=1, keepdims=True)
    d = x - mu
    var = jnp.mean(d * d, axis=1, keepdims=True)
    return d * lax.rsqrt(var + LN_EPS) * g + b


def _merge_kernel(x_ref, ya_ref, at_ref, wg_ref, wa_ref, wb_ref, wo_ref, g1_ref, b1_ref, wr_ref, br_ref,
                  h_ref, ri_ref, cnt_ref):
    @pl.when(pl.program_id(0) == 0)
    def _():
        cnt_ref[...] = jnp.zeros(cnt_ref.shape, F32)
    half = x_ref.shape[0] // 2
    halves = [slice(sb * half, (sb + 1) * half) for sb in range(2)]
    pre = []
    for rows in halves:
        g = _dot(x_ref[rows, :].astype(BF16), wg_ref[...])
        pre.append((g, _dot(ya_ref[rows, :], wa_ref[...]), _dot(at_ref[rows, :], wb_ref[...])))
    mixes = []
    for g, ya, yb in pre:
        mixin = _sigmoid(g[:, :D_MODEL]) * ya + _sigmoid(g[:, D_MODEL:]) * yb
        mixes.append(_dot(mixin.astype(BF16), wo_ref[...]))
    for rows, mix in zip(halves, mixes):
        h_ref[rows, :] = _layer_norm(ALPHA * x_ref[rows, :] + mix, g1_ref[...], b1_ref[...])
    for rows in halves:
        _route_rows(h_ref.at[rows], wr_ref, br_ref, ri_ref.at[rows], cnt_ref)


def _route_rows(h_ref, wr_ref, br_ref, ri_ref, cnt_ref):
    lg = _mm3(h_ref[...], wr_ref[...]) + br_ref[...]
    lane = lax.broadcasted_iota(I32, lg.shape, 1)
    gl = jnp.where(lane < N_GROUPS, lg, NEG)
    gmax = jnp.max(gl, axis=1, keepdims=True)
    p_g = 1.0 / jnp.sum(jnp.exp(gl - gmax), axis=1, keepdims=True)
    gsel = jnp.min(jnp.where(gl == gmax, lane, LANES), axis=1, keepdims=True)
    lo = N_GROUPS + EXPERTS_PER_GROUP * gsel
    el = jnp.where((lane >= lo) & (lane < lo + EXPERTS_PER_GROUP), lg, NEG)
    e1 = jnp.max(el, axis=1, keepdims=True)
    i1 = jnp.min(jnp.where(el == e1, lane, LANES), axis=1, keepdims=True)
    el2 = jnp.where(lane == i1, NEG, el)
    e2 = jnp.max(el2, axis=1, keepdims=True)
    i2 = jnp.min(jnp.where(el2 == e2, lane, LANES), axis=1, keepdims=True)
    w2 = jnp.exp(e2 - e1)
    gate1 = p_g / (1.0 + w2)
    gate2 = p_g * w2 / (1.0 + w2)
    tm = lg.shape[0]
    oh = jnp.concatenate([jnp.where(lane == i1 - N_GROUPS, 1.0, 0.0),
                          jnp.where(lane == i2 - N_GROUPS, 1.0, 0.0)], axis=0)
    rr = lax.broadcasted_iota(I32, (2 * tm, 2 * tm), 0)
    cc = lax.broadcasted_iota(I32, (2 * tm, 2 * tm), 1)
    before = _dot(jnp.where(rr > cc, 1.0, 0.0).astype(BF16), oh.astype(BF16))
    rank = jnp.sum((before + cnt_ref[0:1, :]) * oh, axis=1, keepdims=True)
    cnt_ref[...] = cnt_ref[...] + jnp.sum(oh, axis=0, keepdims=True)
    cols = [(i1 - N_GROUPS).astype(F32), (i2 - N_GROUPS).astype(F32), gate1, gate2, rank[:tm], rank[tm:]]
    ri = jnp.zeros(lg.shape, F32)
    for n, col in enumerate(cols):
        ri = jnp.where(lane == n, col, ri)
    ri_ref[...] = ri


def _merge(x2, ya, at, wg, wa, wb, wo, g1, b1, wr, br, tm):
    T = x2.shape[0]
    row = lambda n: pl.BlockSpec((tm, n), lambda i: (i, 0))
    ws = [wg, wa, wb, wo, g1, b1, wr, br]
    return pl.pallas_call(
        _merge_kernel,
        grid=(T // tm,),
        in_specs=[row(D_MODEL), row(HDIM), row(HDIM)] + [_full(w.shape) for w in ws],
        out_specs=[row(D_MODEL), row(LANES), _full((8, LANES))],
        out_shape=[jax.ShapeDtypeStruct((T, D_MODEL), F32), jax.ShapeDtypeStruct((T, LANES), F32),
                   jax.ShapeDtypeStruct((8, LANES), F32)],
        compiler_params=_params(("arbitrary",)),
        name="merge_router",
    )(x2, ya, at, *ws)


def _row_gather(idx_ref, n, src_hbm, dst, sem):
    def issue(r, carry):
        pltpu.make_async_copy(src_hbm.at[pl.ds(idx_ref[0, 0, r], 1)], dst.at[pl.ds(r, 1)], sem).start()
        return carry
    lax.fori_loop(0, n, issue, 0, unroll=8)
    pltpu.make_async_copy(src_hbm.at[pl.ds(0, n)], dst, sem).wait()


def _dispatch_kernel(p0_ref, p1_ref, h_ref, xs_in, xs_hbm, sem):
    del xs_in
    tm = h_ref.shape[0]

    def issue(r, carry):
        src = h_ref.at[pl.ds(r, 1)]
        pltpu.make_async_copy(src, xs_hbm.at[pl.ds(p0_ref[0, 0, r], 1)], sem).start()
        pltpu.make_async_copy(src, xs_hbm.at[pl.ds(p1_ref[0, 0, r], 1)], sem).start()
        return carry
    lax.fori_loop(0, tm, issue, 0, unroll=8)
    for _ in range(2):
        pltpu.make_async_copy(h_ref, xs_hbm.at[pl.ds(0, tm)], sem).wait()


def _dispatch(pos0, pos1, h, n_rows, tm):
    T = h.shape[0]
    idx = pl.BlockSpec((1, 1, tm), lambda i: (i, 0, 0), memory_space=pltpu.SMEM)
    return pl.pallas_call(
        _dispatch_kernel,
        grid=(T // tm,),
        in_specs=[idx, idx, pl.BlockSpec((tm, D_MODEL), lambda i: (i, 0)), pl.BlockSpec(memory_space=pl.ANY)],
        out_specs=pl.BlockSpec(memory_space=pl.ANY),
        out_shape=jax.ShapeDtypeStruct((n_rows, D_MODEL), F32),
        scratch_shapes=[pltpu.SemaphoreType.DMA(())],
        input_output_aliases={3: 0},
        compiler_params=_params(("arbitrary",)),
        name="moe_dispatch",
    )(pos0, pos1, h, jnp.zeros((n_rows, D_MODEL), F32))


def _moe_kernel(te_ref, nu_ref, xs_ref, wg_ref, wu_ref, wd_ref, o_ref):
    t = pl.program_id(0)

    @pl.when(t < nu_ref[0])
    def _():
        xb = xs_ref[...].astype(BF16)
        hg = _dot(xb, wg_ref[0])
        hu = _dot(xb, wu_ref[0])
        act = (hg * _sigmoid(hg)) * hu
        o_ref[...] = _dot(act.astype(BF16), wd_ref[0])

    @pl.when(t >= nu_ref[0])
    def _():
        o_ref[...] = jnp.zeros(o_ref.shape, F32)


def _moe(tile_e, n_used, xs, wg, wu, wd, tm):
    n_tiles = xs.shape[0] // tm
    grid_spec = pltpu.PrefetchScalarGridSpec(
        num_scalar_prefetch=2,
        grid=(n_tiles,),
        in_specs=[pl.BlockSpec((tm, D_MODEL), lambda t, te, nu: (t, 0)),
                  pl.BlockSpec((1, D_MODEL, D_EXPERT), lambda t, te, nu: (te[t], 0, 0)),
                  pl.BlockSpec((1, D_MODEL, D_EXPERT), lambda t, te, nu: (te[t], 0, 0)),
                  pl.BlockSpec((1, D_EXPERT, D_MODEL), lambda t, te, nu: (te[t], 0, 0))],
        out_specs=pl.BlockSpec((tm, D_MODEL), lambda t, te, nu: (t, 0)),
    )
    return pl.pallas_call(
        _moe_kernel,
        grid_spec=grid_spec,
        out_shape=jax.ShapeDtypeStruct((n_tiles * tm, D_MODEL), F32),
        compiler_params=_params(("arbitrary",)),
        name="moe_experts",
    )(tile_e, n_used, xs, wg, wu, wd)


def _final_kernel(p0_ref, p1_ref, n0_ref, n1_ref, h_ref, ri_ref, ys_hbm, g2_ref, b2_ref, o_ref, y0, y1, sems):
    i = pl.program_id(0)
    tm = h_ref.shape[0]
    slot = i % 2

    def issue(idx0, idx1, buf):
        def body(r, carry):
            pltpu.make_async_copy(ys_hbm.at[pl.ds(idx0[0, 0, r], 1)], y0.at[buf, pl.ds(r, 1)], sems.at[buf]).start()
            pltpu.make_async_copy(ys_hbm.at[pl.ds(idx1[0, 0, r], 1)], y1.at[buf, pl.ds(r, 1)], sems.at[buf]).start()
            return carry
        lax.fori_loop(0, tm, body, 0, unroll=8)

    @pl.when(i == 0)
    def _():
        issue(p0_ref, p1_ref, 0)

    @pl.when(i + 1 < pl.num_programs(0))
    def _():
        issue(n0_ref, n1_ref, 1 - slot)

    pltpu.make_async_copy(ys_hbm.at[pl.ds(0, tm)], y0.at[slot], sems.at[slot]).wait()
    pltpu.make_async_copy(ys_hbm.at[pl.ds(0, tm)], y1.at[slot], sems.at[slot]).wait()
    ri = ri_ref[...]
    moe = y0[slot] * ri[:, 2:3] + y1[slot] * ri[:, 3:4]
    o_ref[...] = _layer_norm(ALPHA * h_ref[...] + moe, g2_ref[...], b2_ref[...])


def _final(pos0, pos1, h, ri, ys, g2, b2, tm):
    T = h.shape[0]
    n = T // tm
    idx = pl.BlockSpec((1, 1, tm), lambda i: (i, 0, 0), memory_space=pltpu.SMEM)
    nxt = pl.BlockSpec((1, 1, tm), lambda i: (jnp.minimum(i + 1, n - 1), 0, 0), memory_space=pltpu.SMEM)
    row = lambda c: pl.BlockSpec((tm, c), lambda i: (i, 0))
    return pl.pallas_call(
        _final_kernel,
        grid=(n,),
        in_specs=[idx, idx, nxt, nxt, row(D_MODEL), row(LANES), pl.BlockSpec(memory_space=pl.ANY),
                  _full(g2.shape), _full(b2.shape)],
        out_specs=row(D_MODEL),
        out_shape=jax.ShapeDtypeStruct((T, D_MODEL), F32),
        scratch_shapes=[pltpu.VMEM((2, tm, D_MODEL), F32), pltpu.VMEM((2, tm, D_MODEL), F32),
                        pltpu.SemaphoreType.DMA((2,))],
        compiler_params=_params(("arbitrary",)),
        name="combine_ln",
    )(pos0, pos1, pos0, pos1, h, ri, ys, g2, b2)


def _routing_tables(eid, rank, sizes, tm):
    T = eid.shape[0]
    n_tiles = (2 * T) // tm + N_EXPERTS
    padded = ((sizes + tm - 1) // tm) * tm
    pad_end = jnp.cumsum(padded)
    pad_off = pad_end - padded
    pos = jnp.take(pad_off, eid) + rank
    tile_start = jnp.arange(n_tiles, dtype=I32) * tm
    tile_e = jnp.minimum(jnp.sum(tile_start[:, None] >= pad_end[None, :], axis=1), N_EXPERTS - 1).astype(I32)
    n_used = (pad_end[-1] // tm).astype(I32).reshape(1)
    return tile_e, n_used, pos, n_tiles * tm


def _block_diag_ones():
    hid = jnp.arange(HDIM, dtype=I32) // HEAD_DIM
    return (hid[:, None] == hid[None, :]).astype(BF16)


def _layer(x, w_in, mu_shift, w0, w_lora_up, a0, a_lora_up, g_lora_up, k_k, k_a, r_k, ln_x_g, ln_x_b,
           w_branch_a, idx_k_ln_g, idx_k_ln_b, rel_bias, w_branch_b, w_out, ln1_g, ln1_b,
           w_router_grp, b_router_grp, w_router_exp, b_router_exp, w_gate, w_up, w_down, ln2_g, ln2_b):
    B, S, _ = x.shape
    T = B * S
    x2 = x.reshape(T, D_MODEL)
    row = lambda t: t.reshape(1, -1)

    c_r, c_wd, c_k, c_v, c_ad, c_gd = 0, 512, 576, 1088, 1600, 1664
    perm = jnp.concatenate([jnp.arange(c_r, c_r + 512), jnp.arange(c_k, c_k + 512), jnp.arange(c_v, c_v + 512),
                            jnp.arange(c_wd, c_wd + 64), jnp.arange(c_ad, c_ad + 64),
                            jnp.arange(c_gd, c_gd + 128)])
    o_q = RWKV_COLS
    o_c = o_q + 4 * HDIM
    o_g = o_c + IDX_DIM + N_HEADS
    wr = w_in[:, :RWKV_COLS][:, perm].astype(BF16)
    wq = w_in[:, o_q:o_c].astype(BF16)
    wc = jnp.pad(w_in[:, o_c:o_g], ((0, 0), (0, LANES - IDX_DIM - N_HEADS))).astype(BF16)
    wgates = w_in[:, o_g:].astype(BF16)
    pad_idx = lambda t: jnp.pad(t, (0, LANES - IDX_DIM)).reshape(1, LANES)

    pr, q, k, ve, vo, qi, kz, kw = _inproj(x2, wr, wq, wc, pad_idx(idx_k_ln_g), pad_idx(idx_k_ln_b), tm=512)

    bd = _block_diag_ones()
    prm = {
        "mu": row(mu_shift[perm]), "w0": row(w0), "a0": row(a0), "k_k": row(k_k), "k_a": row(k_a),
        "r_k": row(r_k),
        "wup": jnp.pad(w_lora_up, ((0, AAA_LORA), (0, 0))),
        "aup": jnp.pad(a_lora_up, ((DECAY_LORA, 0), (0, 0))),
        "gup": g_lora_up, "bd": bd,
    }
    arrs = _rwkv_prep(pr, S, prm, tm=256)
    ya = _rwkv_scan(arrs, row(ln_x_g), row(ln_x_b), bd, B, S)

    at = _attention(q, k, ve, vo, qi, kz, kw, rel_bias, B, S)

    w_router = jnp.pad(jnp.concatenate([w_router_grp, w_router_exp], axis=1),
                       ((0, 0), (0, LANES - N_GROUPS - N_EXPERTS)))
    b_router = jnp.pad(jnp.concatenate([b_router_grp, b_router_exp]), (0, LANES - N_GROUPS - N_EXPERTS))
    h1, ri, cnt = _merge(x2, ya, at, wgates, w_branch_a.astype(BF16), w_branch_b.astype(BF16),
                         w_out.astype(BF16), row(ln1_g), row(ln1_b), w_router, row(b_router), tm=512)

    tm_e = 512
    tile_e, n_used, pos, n_rows = _routing_tables(ri[:, 0:2].astype(I32), ri[:, 4:6].astype(I32),
                                                  cnt[0, :N_EXPERTS].astype(I32), tm_e)
    tm_f = 256
    pos0 = pos[:, 0].reshape(T // tm_f, 1, tm_f)
    pos1 = pos[:, 1].reshape(T // tm_f, 1, tm_f)
    xs = _dispatch(pos0, pos1, h1, n_rows, tm_f)
    ys = _moe(tile_e, n_used, xs, w_gate.astype(BF16), w_up.astype(BF16), w_down.astype(BF16), tm_e)
    out = _final(pos0, pos1, h1, ri, ys, row(ln2_g), row(ln2_b), tm_f)
    return out.reshape(B, S, D_MODEL)


def kernel(x, w_in, mu_shift, w0, w_lora_up, a0, a_lora_up, g_lora_up, k_k, k_a, r_k, ln_x_g, ln_x_b, w_branch_a, idx_k_ln_g, idx_k_ln_b, rel_bias, w_branch_b, w_out, ln1_g, ln1_b, w_router_grp, b_router_grp, w_router_exp, b_router_exp, w_expert_gate, w_expert_up, w_expert_down, ln2_g, ln2_b):
    assert w_in.shape[0] == 1, "single-layer (DEPTH = 1) block"
    l = 0
    return _layer(x, w_in[l], mu_shift[l], w0[l], w_lora_up[l], a0[l], a_lora_up[l], g_lora_up[l], k_k[l],
                  k_a[l], r_k[l], ln_x_g[l], ln_x_b[l], w_branch_a[l], idx_k_ln_g[l], idx_k_ln_b[l], rel_bias,
                  w_branch_b[l], w_out[l], ln1_g[l], ln1_b[l], w_router_grp[l], b_router_grp[l],
                  w_router_exp[l], b_router_exp[l], w_expert_gate[l], w_expert_up[l], w_expert_down[l],
                  ln2_g[l], ln2_b[l])
```

```python
import functools
import math

import jax
import jax.numpy as jnp
from jax import lax
from jax.experimental import pallas as pl
from jax.experimental.pallas import tpu as pltpu

F32 = jnp.float32
BF16 = jnp.bfloat16
I32 = jnp.int32

D_MODEL = 1024
HEAD_DIM = 64
N_HEADS = 8
HDIM = N_HEADS * HEAD_DIM
DECAY_LORA = 64
AAA_LORA = 64
GATE_LORA = 128
RWKV_COLS = 3 * HDIM + DECAY_LORA + AAA_LORA + GATE_LORA
IDX_DIM = 64
MAX_TOPK = 256
N_BUCKETS = 32
MAX_EXACT = 16
MAX_DISTANCE = 128
N_GROUPS = 4
EXPERTS_PER_GROUP = 8
N_EXPERTS = 32
D_EXPERT = 512
GN_EPS = 64e-5
LN_EPS = 1e-5
ALPHA = 2.0 ** 0.25
LANES = 128
CHUNK = 64
SCAN_SUB = 4
QB = 128
KCH = 512
ICH = 512
INT_MIN = -(2 ** 31)
NEG = -1e30
LOG2E = 1.4426950408889634
VMEM_LIMIT = 56 * 1024 * 1024

NN = (((1,), (0,)), ((), ()))
NT = (((1,), (1,)), ((), ()))


def _dot(a, b, dims=NN):
    return lax.dot_general(a, b, dims, preferred_element_type=F32)


def _split2(x):
    hi = x.astype(BF16)
    lo = (x - hi.astype(F32)).astype(BF16)
    return hi, lo


def _split3(x):
    hi = x.astype(BF16)
    r1 = x - hi.astype(F32)
    mid = r1.astype(BF16)
    lo = (r1 - mid.astype(F32)).astype(BF16)
    return hi, mid, lo


def _mm1(a, b, dims=NN):
    return _dot(a.astype(BF16), b.astype(BF16), dims)


def _mm3(a, b, dims=NN):
    ah, al = _split2(a)
    bh, bl = _split2(b)
    return _dot(ah, bh, dims) + (_dot(ah, bl, dims) + _dot(al, bh, dims))


_mm_misc = _mm1
_mm_inv = _mm1
_mm_state = _mm3


def _mm_exact_lhs(a_bf, b):
    b0, b1, b2 = _split3(b)
    return _dot(a_bf, b0) + (_dot(a_bf, b1) + _dot(a_bf, b2))


def _mm_exact_rhs(a, b_bf):
    a0, a1 = _split2(a)
    return _dot(a0, b_bf) + _dot(a1, b_bf)


def _params(sem, vmem=VMEM_LIMIT):
    return pltpu.CompilerParams(dimension_semantics=sem, vmem_limit_bytes=vmem)


def _full(shape):
    nd = len(shape)
    return pl.BlockSpec(shape, lambda *_: (0,) * nd)


def _inproj_kernel(x_ref, wr_ref, wq_ref, wc_ref, lng_ref, lnb_ref,
                   pr_ref, q_ref, k_ref, ve_ref, vo_ref, qi_ref, kz_ref, kw_ref):
    xb = x_ref[...].astype(BF16)
    pr_ref[...] = _dot(xb, wr_ref[...])
    qkv = _dot(xb, wq_ref[...])
    q_ref[...] = (qkv[:, 0:HDIM] * (HEAD_DIM ** -0.5 * LOG2E)).astype(BF16)
    k_ref[...] = qkv[:, HDIM:2 * HDIM].astype(BF16)
    v = qkv[:, 2 * HDIM:3 * HDIM]
    even = (lax.broadcasted_iota(I32, v.shape, 1) % LANES) < HEAD_DIM
    ve_ref[...] = jnp.where(even, v, 1.0).astype(BF16)
    vo_ref[...] = jnp.where(even, 1.0, v).astype(BF16)
    qi_ref[...] = qkv[:, 3 * HDIM:4 * HDIM].astype(BF16)
    c = _dot(xb, wc_ref[...])
    kw_ref[...] = c
    lane = lax.broadcasted_iota(I32, c.shape, 1)
    isk = lane < IDX_DIM
    mu = jnp.sum(jnp.where(isk, c, 0.0), axis=1, keepdims=True) * (1.0 / IDX_DIM)
    d = jnp.where(isk, c - mu, 0.0)
    var = jnp.sum(d * d, axis=1, keepdims=True) * (1.0 / IDX_DIM)
    kn = d * lax.rsqrt(var + LN_EPS) * lng_ref[...] + lnb_ref[...]
    kz_ref[:, 0:LANES] = kn.astype(BF16)
    kz_ref[:, LANES:2 * LANES] = pltpu.roll(kn, IDX_DIM, 1).astype(BF16)


def _inproj(x2, wr, wq, wc, lng, lnb, tm):
    T = x2.shape[0]
    row = lambda n: pl.BlockSpec((tm, n), lambda i: (i, 0))
    return pl.pallas_call(
        _inproj_kernel,
        grid=(T // tm,),
        in_specs=[row(D_MODEL), _full(wr.shape), _full(wq.shape), _full(wc.shape),
                  _full(lng.shape), _full(lnb.shape)],
        out_specs=[row(RWKV_COLS), row(HDIM), row(HDIM), row(HDIM), row(HDIM), row(HDIM),
                   row(2 * LANES), row(LANES)],
        out_shape=[jax.ShapeDtypeStruct((T, RWKV_COLS), F32)]
        + [jax.ShapeDtypeStruct((T, HDIM), BF16)] * 5
        + [jax.ShapeDtypeStruct((T, 2 * LANES), BF16), jax.ShapeDtypeStruct((T, LANES), F32)],
        compiler_params=_params(("parallel",)),
        name="inproj",
    )(x2, wr, wq, wc, lng, lnb)


def _softplus(x):
    return jnp.maximum(x, 0.0) + jnp.log(1.0 + jnp.exp(-jnp.abs(x)))


def _sigmoid(x):
    return 1.0 / (1.0 + jnp.exp(-x))


def _prep_kernel(tiles_per_seq, p_ref, pp_ref, mu_ref, w0_ref, a0_ref, kk_ref, ka_ref, rk_ref,
                 wup_ref, aup_ref, gup_ref, bd_ref,
                 r_ref, lw_ref, k_ref, v_ref, a_ref, b_ref, g_ref, bon_ref):
    i = pl.program_id(0)
    p = p_ref[...]
    tm = p.shape[0]
    first = (i % tiles_per_seq) == 0
    prow = jnp.where(first, 0.0, pp_ref[7:8, :])
    rowid = lax.broadcasted_iota(I32, p.shape, 0)
    prev = jnp.where(rowid == 0, prow, pltpu.roll(p, 1, 0))
    ps = p + (prev - p) * mu_ref[...]
    r = ps[:, 0:HDIM]
    k = ps[:, HDIM:2 * HDIM]
    v = ps[:, 2 * HDIM:3 * HDIM]
    da = ps[:, 3 * HDIM:3 * HDIM + LANES]
    gd = ps[:, 3 * HDIM + LANES:3 * HDIM + 2 * LANES]
    w = -_softplus(-(w0_ref[...] + _mm3(jnp.tanh(da), wup_ref[...]))) - 0.5
    lw_ref[...] = -jnp.exp(w)
    a = _sigmoid(a0_ref[...] + _mm3(da, aup_ref[...]))
    g_ref[...] = _mm3(_sigmoid(gd), gup_ref[...])
    bd = bd_ref[...]
    kk = k * kk_ref[...]
    ss = _mm_exact_rhs(kk * kk, bd)
    kk = kk / jnp.maximum(jnp.sqrt(ss), 1e-12)
    k2 = k * (1.0 + (a - 1.0) * ka_ref[...])
    r_ref[...] = r
    k_ref[...] = k2
    v_ref[...] = v
    a_ref[...] = -kk
    b_ref[...] = kk * a
    bon_ref[...] = _mm_exact_rhs(r * k2 * rk_ref[...], bd) * v


def _rwkv_prep(pr, S, prm, tm):
    T = pr.shape[0]
    row = lambda n: pl.BlockSpec((tm, n), lambda i: (i, 0))
    prev = pl.BlockSpec((8, RWKV_COLS), lambda i: (jnp.maximum(i * (tm // 8) - 1, 0), 0))
    names = ["mu", "w0", "a0", "k_k", "k_a", "r_k", "wup", "aup", "gup", "bd"]
    return pl.pallas_call(
        functools.partial(_prep_kernel, S // tm),
        grid=(T // tm,),
        in_specs=[row(RWKV_COLS), prev] + [_full(prm[n].shape) for n in names],
        out_specs=[row(HDIM)] * 8,
        out_shape=[jax.ShapeDtypeStruct((T, HDIM), F32)] * 8,
        compiler_params=_params(("parallel",)),
        name="rwkv_prep",
    )(pr, pr, *[prm[n] for n in names])


def _scan_kernel(r_ref, lw_ref, k_ref, v_ref, a_ref, b_ref, g_ref, bon_ref, lng_ref, lnb_ref, bd_ref,
                 o_ref, st_ref, y_ref):
    C, N, H = CHUNK, HEAD_DIM, N_HEADS

    @pl.when(pl.program_id(1) == 0)
    def _():
        st_ref[...] = jnp.zeros(st_ref.shape, F32)

    ri = lax.broadcasted_iota(I32, (C, C), 0)
    ci = lax.broadcasted_iota(I32, (C, C), 1)
    incl = ri >= ci
    strict = ri > ci
    eye = ri == ci
    eye_f = jnp.where(eye, 1.0, 0.0)
    lmat = jnp.where(incl, 1.0, 0.0).astype(BF16)
    sls = [slice(h * N, (h + 1) * N) for h in range(H)]
    units = [(s, h) for s in range(SCAN_SUB) for h in range(H)]
    ah, rh, vh, bT, kT, bhT, khT, gam = {}, {}, {}, {}, {}, {}, {}, {}
    for s in range(SCAN_SUB):
        rows = slice(s * C, (s + 1) * C)
        lw = lw_ref[rows, :]
        cum = _mm_exact_lhs(lmat, lw)
        last = cum[C - 1:C, :]
        e_i = jnp.exp(-cum)
        e_end = jnp.exp(last - cum)
        g_s = jnp.exp(last)
        r_t = r_ref[rows, :] * jnp.exp(cum)
        a_t = a_ref[rows, :] * jnp.exp(cum - lw)
        v_s = v_ref[rows, :]
        b_s, k_s = b_ref[rows, :], k_ref[rows, :]
        bT_s, kT_s = (b_s * e_i).T, (k_s * e_i).T
        bhT_s, khT_s = (b_s * e_end).T, (k_s * e_end).T
        for h in range(H):
            un = (s, h)
            ah[un], rh[un], vh[un] = a_t[:, sls[h]], r_t[:, sls[h]], v_s[:, sls[h]]
            bT[un], kT[un] = bT_s[sls[h], :], kT_s[sls[h], :]
            bhT[un], khT[un] = bhT_s[sls[h], :], khT_s[sls[h], :]
            gam[un] = g_s[:, sls[h]]

    ar = {un: jnp.concatenate([ah[un], rh[un]], axis=0) for un in units}
    sb = {un: _mm_misc(ar[un], bT[un]) for un in units}
    sk = {un: _mm_misc(ar[un], kT[un]) for un in units}
    a_ab = {un: jnp.where(strict, sb[un][:C], 0.0) for un in units}
    a_rb = {un: jnp.where(incl, sb[un][C:], 0.0) for un in units}
    a_ak = {un: jnp.where(strict, sk[un][:C], 0.0) for un in units}
    a_rk = {un: jnp.where(incl, sk[un][C:], 0.0) for un in units}
    u = {un: _mm_misc(a_ak[un], vh[un]) for un in units}
    tinv = {un: eye_f + a_ab[un] for un in units}
    xp = a_ab
    for _ in range(5):
        xp = {un: _mm_inv(xp[un], xp[un]) for un in units}
        tinv = {un: tinv[un] + _mm_inv(tinv[un], xp[un]) for un in units}
    pm = {un: _mm_misc(tinv[un], ah[un]) for un in units}
    qm = {un: _mm_misc(tinv[un], u[un]) for un in units}
    r2 = {un: rh[un] + _mm_misc(a_rb[un], pm[un]) for un in units}
    mmat = {un: jnp.where(eye, gam[un], 0.0) + _mm_misc(bhT[un], pm[un]) for un in units}
    y0 = {un: _mm_misc(a_rb[un], qm[un]) + _mm_misc(a_rk[un], vh[un]) for un in units}
    gmat = {un: _mm_misc(bhT[un], qm[un]) + _mm_misc(khT[un], vh[un]) for un in units}
    st = [st_ref[h] for h in range(H)]
    for s in range(SCAN_SUB):
        for h in range(H):
            y_ref[s * C:(s + 1) * C, sls[h]] = _mm_state(r2[(s, h)], st[h]) + y0[(s, h)]
        st = [_mm_state(mmat[(s, h)], st[h]) + gmat[(s, h)] for h in range(H)]
    for h in range(H):
        st_ref[h] = st[h]

    y = y_ref[...]
    bd = bd_ref[...]
    mu = _mm_exact_rhs(y, bd) * (1.0 / N)
    d = y - mu
    var = _mm_exact_rhs(d * d, bd) * (1.0 / N)
    yn = d * lax.rsqrt(var + GN_EPS) * lng_ref[...] + lnb_ref[...]
    o_ref[...] = ((yn + bon_ref[...]) * g_ref[...]).astype(BF16)


def _rwkv_scan(arrs, lng, lnb, bd, B, S):
    rows = SCAN_SUB * CHUNK
    nc = S // rows
    row = pl.BlockSpec((rows, HDIM), lambda b, c: (b * nc + c, 0))
    return pl.pallas_call(
        _scan_kernel,
        grid=(B, nc),
        in_specs=[row] * 8 + [_full(lng.shape), _full(lnb.shape), _full(bd.shape)],
        out_specs=row,
        out_shape=jax.ShapeDtypeStruct((B * S, HDIM), BF16),
        scratch_shapes=[pltpu.VMEM((N_HEADS, HEAD_DIM, HEAD_DIM), F32), pltpu.VMEM((rows, HDIM), F32)],
        compiler_params=_params(("parallel", "arbitrary")),
        name="rwkv_scan",
    )(*arrs, lng, lnb, bd)


def _sort_key(x):
    bits = pltpu.bitcast(x, I32)
    return bits ^ ((bits >> 31) & 0x7FFFFFFF)


def _attn_kernel(top_k, rb_ref, q_ref, k_ref, ve_ref, vo_ref, qi_ref, kz_ref, kw_ref, o_ref,
                 key_t, mbias, btab, qm, s_scr, p_scr, *state):
    m_scr, acc, a_scr = (state[n * N_HEADS:(n + 1) * N_HEADS] for n in range(3))
    i = pl.program_id(1)
    t0 = i * QB
    lane = lax.broadcasted_iota(I32, (QB, LANES), 1)
    rowi = lax.broadcasted_iota(I32, (QB, LANES), 0)

    @pl.when(i == 0)
    def _build_bias():
        for h in range(N_HEADS):
            btab[h, 2] = jnp.zeros((QB, LANES), F32)
        for m in range(2):
            n = jnp.maximum(m * QB + rowi - lane, 0)
            nf = jnp.maximum(n, 1).astype(F32)
            large = MAX_EXACT + (jnp.log(nf / MAX_EXACT) / math.log(MAX_DISTANCE / MAX_EXACT)
                                 * (N_BUCKETS - MAX_EXACT)).astype(I32)
            bucket = jnp.where(n < MAX_EXACT, n, jnp.minimum(large, N_BUCKETS - 1))
            for h in range(N_HEADS):
                t = jnp.zeros((QB, LANES), F32)
                for bk in range(N_BUCKETS):
                    t = jnp.where(bucket == bk, rb_ref[bk, h], t)
                btab[h, m] = (t - rb_ref[N_BUCKETS - 1, h]) * LOG2E

    n_ch = i // (KCH // QB) + 1
    n_ic = i // (ICH // QB) + 1
    nsub, isub = KCH // LANES, ICH // LANES

    kw_t = kw_ref[...].T
    w_t = [kw_t[IDX_DIM + h:IDX_DIM + h + 1, :] * (N_HEADS ** -0.5) * (IDX_DIM ** -0.5) for h in range(N_HEADS)]

    def score_chunk(c, carry):
        c0 = pl.multiple_of(c * ICH, ICH)
        tot = [jnp.zeros((LANES, QB), F32) for _ in range(isub)]
        for p in range(N_HEADS // 2):
            qp = qi_ref[:, p * LANES:(p + 1) * LANES]
            for e in range(2):
                z = _dot(kz_ref[pl.ds(c0, ICH), e * LANES:(e + 1) * LANES], qp, NT)
                for s in range(isub):
                    tot[s] = tot[s] + jnp.maximum(z[s * LANES:(s + 1) * LANES, :], 0.0) * w_t[2 * p + e]
        for s in range(isub):
            causal = c0 + s * LANES + rowi <= t0 + lane
            key_t[pl.ds(pl.multiple_of(c0 + s * LANES, LANES), LANES), :] = \
                jnp.where(causal, _sort_key(tot[s]), INT_MIN)
        return carry
    lax.fori_loop(0, n_ic, score_chunk, 0)

    def count_ge(cand):
        def body(c, cnt):
            kc = key_t[pl.ds(pl.multiple_of(c * ICH, ICH), ICH), :]
            ones = jnp.where(kc >= cand, 1, 0)
            return cnt + jnp.sum(ones.reshape(ICH // 8, 8, QB), axis=0)
        cnt = lax.fori_loop(0, n_ic, body, jnp.zeros((8, QB), I32))
        return jnp.sum(cnt, axis=0, keepdims=True)

    def search(bit, carry):
        u, cu = carry
        cand = u | (jnp.int32(1) << (31 - bit))
        cnt = count_ge(cand ^ INT_MIN)
        ok = cnt >= top_k
        return jnp.where(ok, cand, u), jnp.where(ok, cnt, cu)

    few = (t0 + lax.broadcasted_iota(I32, (1, QB), 1)) < top_k

    def unsettled(carry):
        g, _, cu = carry
        return (g < 8) & (jnp.min(jnp.where(few | (cu == top_k), 1, 0)) == 0)

    def four_bits(carry):
        g, u, cu = carry
        u, cu = lax.fori_loop(4 * g, 4 * g + 4, search, (u, cu))
        return g + 1, u, cu

    u, cu = lax.fori_loop(0, 20, search, (jnp.zeros((1, QB), I32), jnp.zeros((1, QB), I32)))
    _, u, cu = lax.while_loop(unsettled, four_bits, (jnp.int32(5), u, cu))
    thr = jnp.maximum(u ^ INT_MIN, INT_MIN + 1)

    @pl.when(jnp.max(cu) > top_k)
    def _fix_ties():
        budget = (top_k - count_ge(thr + 1)).astype(F32)
        lt = jnp.where(rowi >= lane, 1.0, 0.0).astype(BF16)

        def body(j, before):
            sl = pl.ds(pl.multiple_of(j * LANES, LANES), LANES)
            kc = key_t[sl, :]
            eq = kc == thr
            eqf = jnp.where(eq, 1.0, 0.0)
            rank = before + _dot(lt, eqf.astype(BF16)) - eqf
            key_t[sl, :] = jnp.where(eq & (rank >= budget), INT_MIN, kc)
            return before + jnp.sum(eqf, axis=0, keepdims=True)
        lax.fori_loop(0, i + 1, body, jnp.zeros((1, QB), F32))

    def mask_chunk(c, carry):
        c0 = pl.multiple_of(c * ICH, ICH)
        for s in range(isub):
            sl = pl.ds(pl.multiple_of(c0 + s * LANES, LANES), LANES)
            mbias[:, sl] = jnp.where(key_t[sl, :] >= thr, 0.0, NEG).T
        return carry
    lax.fori_loop(0, n_ic, mask_chunk, 0)

    if ICH < KCH:
        @pl.when(n_ic * ICH < n_ch * KCH)
        def _():
            mbias[:, pl.ds(pl.multiple_of(n_ic * ICH, ICH), ICH)] = jnp.full((QB, ICH), NEG, F32)

    q = q_ref[...]
    even = lane < HEAD_DIM
    for p in range(N_HEADS // 2):
        qp = q[:, p * LANES:(p + 1) * LANES]
        qm[2 * p] = jnp.where(even, qp, jnp.zeros_like(qp))
        qm[2 * p + 1] = jnp.where(even, jnp.zeros_like(qp), qp)
    for h in range(N_HEADS):
        m_scr[h][...] = jnp.full((QB, LANES), NEG, F32)
        acc[h][...] = jnp.zeros((QB, LANES), F32)

    def key_block(c, near):
        start = pl.multiple_of(c * KCH, KCH)
        rows = pl.ds(start, KCH)
        nsub = KCH // LANES
        for h in range(N_HEADS):
            cols = slice((h // 2) * LANES, (h // 2 + 1) * LANES)
            s_scr[h] = _dot(qm[h], k_ref[rows, cols], NT)
        for h in range(N_HEADS):
            sub = []
            for n in range(nsub):
                sc = s_scr[h, :, n * LANES:(n + 1) * LANES] + mbias[:, pl.ds(start + n * LANES, LANES)]
                if near:
                    sc = sc + btab[h, jnp.clip(i - (c * nsub + n), 0, 2)]
                sub.append(sc)
            mx = jnp.maximum(jnp.maximum(sub[0], sub[1]), jnp.maximum(sub[2], sub[3]))
            m_old = m_scr[h][...]
            m_new = jnp.maximum(m_old, jnp.max(mx, axis=1, keepdims=True))
            for n in range(nsub):
                p_scr[h, :, n * LANES:(n + 1) * LANES] = jnp.exp2(sub[n] - m_new).astype(BF16)
            a_scr[h][...] = jnp.exp2(m_old - m_new)
            m_scr[h][...] = m_new
        for h in range(N_HEADS):
            cols = slice((h // 2) * LANES, (h // 2 + 1) * LANES)
            v_ref = vo_ref if h % 2 else ve_ref
            acc[h][...] = a_scr[h][...] * acc[h][...] + _dot(p_scr[h], v_ref[rows, cols])

    n_far = jnp.maximum(i - 1, 0) // (KCH // QB)

    def far_body(c, carry):
        key_block(c, False)
        return carry
    lax.fori_loop(0, n_far, far_body, 0)

    def near_body(c, carry):
        key_block(c, True)
        return carry
    lax.fori_loop(n_far, n_ch, near_body, 0)

    for p in range(N_HEADS // 2):
        ae, ao = acc[2 * p][...], acc[2 * p + 1][...]
        oe = ae / pltpu.roll(ae, HEAD_DIM, 1)
        oo = ao / pltpu.roll(ao, HEAD_DIM, 1)
        o_ref[:, p * LANES:(p + 1) * LANES] = jnp.where(even, oe, oo).astype(BF16)


def _attention(q, k, ve, vo, qi, kz, kw, rel_bias, B, S):
    nq = S // QB
    top_k = min(MAX_TOPK, S // 4)
    qrow = lambda n: pl.BlockSpec((QB, n), lambda b, i: (b * nq + i, 0))
    seq = lambda n: pl.BlockSpec((S, n), lambda b, i: (b, 0))
    return pl.pallas_call(
        functools.partial(_attn_kernel, top_k),
        grid=(B, nq),
        in_specs=[pl.BlockSpec(memory_space=pltpu.SMEM), qrow(HDIM), seq(HDIM), seq(HDIM), seq(HDIM),
                  qrow(HDIM), seq(2 * LANES), qrow(LANES)],
        out_specs=qrow(HDIM),
        out_shape=jax.ShapeDtypeStruct((B * S, HDIM), BF16),
        scratch_shapes=[pltpu.VMEM((S, QB), I32),
                        pltpu.VMEM((QB, S), F32),
                        pltpu.VMEM((N_HEADS, 3, QB, LANES), F32),
                        pltpu.VMEM((N_HEADS, QB, LANES), BF16),
                        pltpu.VMEM((N_HEADS, QB, KCH), F32),
                        pltpu.VMEM((N_HEADS, QB, KCH), BF16)]
        + [pltpu.VMEM((QB, LANES), F32)] * (3 * N_HEADS),
        compiler_params=_params(("parallel", "arbitrary")),
        name="dsa_attention",
    )(rel_bias, q, k, ve, vo, qi, kz, kw)


def _layer_norm(x, g, b):
    mu = jnp.mean(x, axis=1, keepdims=True)
    d = x - mu
    var = jnp.mean(d * d, axis=1, keepdims=True)
    return d * lax.rsqrt(var + LN_EPS) * g + b


def _merge_kernel(x_ref, ya_ref, at_ref, wg_ref, wa_ref, wb_ref, wo_ref, g1_ref, b1_ref, wr_ref, br_ref,
                  h_ref, ri_ref, cnt_ref):
    @pl.when(pl.program_id(0) == 0)
    def _():
        cnt_ref[...] = jnp.zeros(cnt_ref.shape, F32)
    half = x_ref.shape[0] // 2
    halves = [slice(sb * half, (sb + 1) * half) for sb in range(2)]
    pre = []
    for rows in halves:
        g = _dot(x_ref[rows, :].astype(BF16), wg_ref[...])
        pre.append((g, _dot(ya_ref[rows, :], wa_ref[...]), _dot(at_ref[rows, :], wb_ref[...])))
    mixes = []
    for g, ya, yb in pre:
        mixin = _sigmoid(g[:, :D_MODEL]) * ya + _sigmoid(g[:, D_MODEL:]) * yb
        mixes.append(_dot(mixin.astype(BF16), wo_ref[...]))
    for rows, mix in zip(halves, mixes):
        h_ref[rows, :] = _layer_norm(ALPHA * x_ref[rows, :] + mix, g1_ref[...], b1_ref[...])
    for rows in halves:
        _route_rows(h_ref.at[rows], wr_ref, br_ref, ri_ref.at[rows], cnt_ref)


def _route_rows(h_ref, wr_ref, br_ref, ri_ref, cnt_ref):
    lg = _mm3(h_ref[...], wr_ref[...]) + br_ref[...]
    lane = lax.broadcasted_iota(I32, lg.shape, 1)
    gl = jnp.where(lane < N_GROUPS, lg, NEG)
    gmax = jnp.max(gl, axis=1, keepdims=True)
    p_g = 1.0 / jnp.sum(jnp.exp(gl - gmax), axis=1, keepdims=True)
    gsel = jnp.min(jnp.where(gl == gmax, lane, LANES), axis=1, keepdims=True)
    lo = N_GROUPS + EXPERTS_PER_GROUP * gsel
    el = jnp.where((lane >= lo) & (lane < lo + EXPERTS_PER_GROUP), lg, NEG)
    e1 = jnp.max(el, axis=1, keepdims=True)
    i1 = jnp.min(jnp.where(el == e1, lane, LANES), axis=1, keepdims=True)
    el2 = jnp.where(lane == i1, NEG, el)
    e2 = jnp.max(el2, axis=1, keepdims=True)
    i2 = jnp.min(jnp.where(el2 == e2, lane, LANES), axis=1, keepdims=True)
    w2 = jnp.exp(e2 - e1)
    gate1 = p_g / (1.0 + w2)
    gate2 = p_g * w2 / (1.0 + w2)
    tm = lg.shape[0]
    oh = jnp.concatenate([jnp.where(lane == i1 - N_GROUPS, 1.0, 0.0),
                          jnp.where(lane == i2 - N_GROUPS, 1.0, 0.0)], axis=0)
    rr = lax.broadcasted_iota(I32, (2 * tm, 2 * tm), 0)
    cc = lax.broadcasted_iota(I32, (2 * tm, 2 * tm), 1)
    before = _dot(jnp.where(rr > cc, 1.0, 0.0).astype(BF16), oh.astype(BF16))
    rank = jnp.sum((before + cnt_ref[0:1, :]) * oh, axis=1, keepdims=True)
    cnt_ref[...] = cnt_ref[...] + jnp.sum(oh, axis=0, keepdims=True)
    cols = [(i1 - N_GROUPS).astype(F32), (i2 - N_GROUPS).astype(F32), gate1, gate2, rank[:tm], rank[tm:]]
    ri = jnp.zeros(lg.shape, F32)
    for n, col in enumerate(cols):
        ri = jnp.where(lane == n, col, ri)
    ri_ref[...] = ri


def _merge(x2, ya, at, wg, wa, wb, wo, g1, b1, wr, br, tm):
    T = x2.shape[0]
    row = lambda n: pl.BlockSpec((tm, n), lambda i: (i, 0))
    ws = [wg, wa, wb, wo, g1, b1, wr, br]
    return pl.pallas_call(
        _merge_kernel,
        grid=(T // tm,),
        in_specs=[row(D_MODEL), row(HDIM), row(HDIM)] + [_full(w.shape) for w in ws],
        out_specs=[row(D_MODEL), row(LANES), _full((8, LANES))],
        out_shape=[jax.ShapeDtypeStruct((T, D_MODEL), F32), jax.ShapeDtypeStruct((T, LANES), F32),
                   jax.ShapeDtypeStruct((8, LANES), F32)],
        compiler_params=_params(("arbitrary",)),
        name="merge_router",
    )(x2, ya, at, *ws)


def _row_gather(idx_ref, n, src_hbm, dst, sem):
    def issue(r, carry):
        pltpu.make_async_copy(src_hbm.at[pl.ds(idx_ref[0, 0, r], 1)], dst.at[pl.ds(r, 1)], sem).start()
        return carry
    lax.fori_loop(0, n, issue, 0, unroll=8)
    pltpu.make_async_copy(src_hbm.at[pl.ds(0, n)], dst, sem).wait()


def _dispatch_kernel(p0_ref, p1_ref, h_ref, xs_in, xs_hbm, sem):
    del xs_in
    tm = h_ref.shape[0]

    def issue(r, carry):
        src = h_ref.at[pl.ds(r, 1)]
        pltpu.make_async_copy(src, xs_hbm.at[pl.ds(p0_ref[0, 0, r], 1)], sem).start()
        pltpu.make_async_copy(src, xs_hbm.at[pl.ds(p1_ref[0, 0, r], 1)], sem).start()
        return carry
    lax.fori_loop(0, tm, issue, 0, unroll=8)
    for _ in range(2):
        pltpu.make_async_copy(h_ref, xs_hbm.at[pl.ds(0, tm)], sem).wait()


def _dispatch(pos0, pos1, h, n_rows, tm):
    T = h.shape[0]
    idx = pl.BlockSpec((1, 1, tm), lambda i: (i, 0, 0), memory_space=pltpu.SMEM)
    return pl.pallas_call(
        _dispatch_kernel,
        grid=(T // tm,),
        in_specs=[idx, idx, pl.BlockSpec((tm, D_MODEL), lambda i: (i, 0)), pl.BlockSpec(memory_space=pl.ANY)],
        out_specs=pl.BlockSpec(memory_space=pl.ANY),
        out_shape=jax.ShapeDtypeStruct((n_rows, D_MODEL), F32),
        scratch_shapes=[pltpu.SemaphoreType.DMA(())],
        input_output_aliases={3: 0},
        compiler_params=_params(("arbitrary",)),
        name="moe_dispatch",
    )(pos0, pos1, h, jnp.zeros((n_rows, D_MODEL), F32))


def _moe_kernel(te_ref, nu_ref, xs_ref, wg_ref, wu_ref, wd_ref, o_ref):
    t = pl.program_id(0)

    @pl.when(t < nu_ref[0])
    def _():
        xb = xs_ref[...].astype(BF16)
        hg = _dot(xb, wg_ref[0])
        hu = _dot(xb, wu_ref[0])
        act = (hg * _sigmoid(hg)) * hu
        o_ref[...] = _dot(act.astype(BF16), wd_ref[0])

    @pl.when(t >= nu_ref[0])
    def _():
        o_ref[...] = jnp.zeros(o_ref.shape, F32)


def _moe(tile_e, n_used, xs, wg, wu, wd, tm):
    n_tiles = xs.shape[0] // tm
    grid_spec = pltpu.PrefetchScalarGridSpec(
        num_scalar_prefetch=2,
        grid=(n_tiles,),
        in_specs=[pl.BlockSpec((tm, D_MODEL), lambda t, te, nu: (t, 0)),
                  pl.BlockSpec((1, D_MODEL, D_EXPERT), lambda t, te, nu: (te[t], 0, 0)),
                  pl.BlockSpec((1, D_MODEL, D_EXPERT), lambda t, te, nu: (te[t], 0, 0)),
                  pl.BlockSpec((1, D_EXPERT, D_MODEL), lambda t, te, nu: (te[t], 0, 0))],
        out_specs=pl.BlockSpec((tm, D_MODEL), lambda t, te, nu: (t, 0)),
    )
    return pl.pallas_call(
        _moe_kernel,
        grid_spec=grid_spec,
        out_shape=jax.ShapeDtypeStruct((n_tiles * tm, D_MODEL), F32),
        compiler_params=_params(("arbitrary",)),
        name="moe_experts",
    )(tile_e, n_used, xs, wg, wu, wd)


def _final_kernel(p0_ref, p1_ref, n0_ref, n1_ref, h_ref, ri_ref, ys_hbm, g2_ref, b2_ref, o_ref, y0, y1, sems):
    i = pl.program_id(0)
    tm = h_ref.shape[0]
    slot = i % 2

    def issue(idx0, idx1, buf):
        def body(r, carry):
            pltpu.make_async_copy(ys_hbm.at[pl.ds(idx0[0, 0, r], 1)], y0.at[buf, pl.ds(r, 1)], sems.at[buf]).start()
            pltpu.make_async_copy(ys_hbm.at[pl.ds(idx1[0, 0, r], 1)], y1.at[buf, pl.ds(r, 1)], sems.at[buf]).start()
            return carry
        lax.fori_loop(0, tm, body, 0, unroll=8)

    @pl.when(i == 0)
    def _():
        issue(p0_ref, p1_ref, 0)

    @pl.when(i + 1 < pl.num_programs(0))
    def _():
        issue(n0_ref, n1_ref, 1 - slot)

    pltpu.make_async_copy(ys_hbm.at[pl.ds(0, tm)], y0.at[slot], sems.at[slot]).wait()
    pltpu.make_async_copy(ys_hbm.at[pl.ds(0, tm)], y1.at[slot], sems.at[slot]).wait()
    ri = ri_ref[...]
    moe = y0[slot] * ri[:, 2:3] + y1[slot] * ri[:, 3:4]
    o_ref[...] = _layer_norm(ALPHA * h_ref[...] + moe, g2_ref[...], b2_ref[...])


def _final(pos0, pos1, h, ri, ys, g2, b2, tm):
    T = h.shape[0]
    n = T // tm
    idx = pl.BlockSpec((1, 1, tm), lambda i: (i, 0, 0), memory_space=pltpu.SMEM)
    nxt = pl.BlockSpec((1, 1, tm), lambda i: (jnp.minimum(i + 1, n - 1), 0, 0), memory_space=pltpu.SMEM)
    row = lambda c: pl.BlockSpec((tm, c), lambda i: (i, 0))
    return pl.pallas_call(
        _final_kernel,
        grid=(n,),
        in_specs=[idx, idx, nxt, nxt, row(D_MODEL), row(LANES), pl.BlockSpec(memory_space=pl.ANY),
                  _full(g2.shape), _full(b2.shape)],
        out_specs=row(D_MODEL),
        out_shape=jax.ShapeDtypeStruct((T, D_MODEL), F32),
        scratch_shapes=[pltpu.VMEM((2, tm, D_MODEL), F32), pltpu.VMEM((2, tm, D_MODEL), F32),
                        pltpu.SemaphoreType.DMA((2,))],
        compiler_params=_params(("arbitrary",)),
        name="combine_ln",
    )(pos0, pos1, pos0, pos1, h, ri, ys, g2, b2)


def _routing_tables(eid, rank, sizes, tm):
    T = eid.shape[0]
    n_tiles = (2 * T) // tm + N_EXPERTS
    padded = ((sizes + tm - 1) // tm) * tm
    pad_end = jnp.cumsum(padded)
    pad_off = pad_end - padded
    pos = jnp.take(pad_off, eid) + rank
    tile_start = jnp.arange(n_tiles, dtype=I32) * tm
    tile_e = jnp.minimum(jnp.sum(tile_start[:, None] >= pad_end[None, :], axis=1), N_EXPERTS - 1).astype(I32)
    n_used = (pad_end[-1] // tm).astype(I32).reshape(1)
    return tile_e, n_used, pos, n_tiles * tm


def _block_diag_ones():
    hid = jnp.arange(HDIM, dtype=I32) // HEAD_DIM
    return (hid[:, None] == hid[None, :]).astype(BF16)


def _layer(x, w_in, mu_shift, w0, w_lora_up, a0, a_lora_up, g_lora_up, k_k, k_a, r_k, ln_x_g, ln_x_b,
           w_branch_a, idx_k_ln_g, idx_k_ln_b, rel_bias, w_branch_b, w_out, ln1_g, ln1_b,
           w_router_grp, b_router_grp, w_router_exp, b_router_exp, w_gate, w_up, w_down, ln2_g, ln2_b):
    B, S, _ = x.shape
    T = B * S
    x2 = x.reshape(T, D_MODEL)
    row = lambda t: t.reshape(1, -1)

    c_r, c_wd, c_k, c_v, c_ad, c_gd = 0, 512, 576, 1088, 1600, 1664
    perm = jnp.concatenate([jnp.arange(c_r, c_r + 512), jnp.arange(c_k, c_k + 512), jnp.arange(c_v, c_v + 512),
                            jnp.arange(c_wd, c_wd + 64), jnp.arange(c_ad, c_ad + 64),
                            jnp.arange(c_gd, c_gd + 128)])
    o_q = RWKV_COLS
    o_c = o_q + 4 * HDIM
    o_g = o_c + IDX_DIM + N_HEADS
    wr = w_in[:, :RWKV_COLS][:, perm].astype(BF16)
    wq = w_in[:, o_q:o_c].astype(BF16)
    wc = jnp.pad(w_in[:, o_c:o_g], ((0, 0), (0, LANES - IDX_DIM - N_HEADS))).astype(BF16)
    wgates = w_in[:, o_g:].astype(BF16)
    pad_idx = lambda t: jnp.pad(t, (0, LANES - IDX_DIM)).reshape(1, LANES)

    pr, q, k, ve, vo, qi, kz, kw = _inproj(x2, wr, wq, wc, pad_idx(idx_k_ln_g), pad_idx(idx_k_ln_b), tm=512)

    bd = _block_diag_ones()
    prm = {
        "mu": row(mu_shift[perm]), "w0": row(w0), "a0": row(a0), "k_k": row(k_k), "k_a": row(k_a),
        "r_k": row(r_k),
        "wup": jnp.pad(w_lora_up, ((0, AAA_LORA), (0, 0))),
        "aup": jnp.pad(a_lora_up, ((DECAY_LORA, 0), (0, 0))),
        "gup": g_lora_up, "bd": bd,
    }
    arrs = _rwkv_prep(pr, S, prm, tm=256)
    ya = _rwkv_scan(arrs, row(ln_x_g), row(ln_x_b), bd, B, S)

    at = _attention(q, k, ve, vo, qi, kz, kw, rel_bias, B, S)

    w_router = jnp.pad(jnp.concatenate([w_router_grp, w_router_exp], axis=1),
                       ((0, 0), (0, LANES - N_GROUPS - N_EXPERTS)))
    b_router = jnp.pad(jnp.concatenate([b_router_grp, b_router_exp]), (0, LANES - N_GROUPS - N_EXPERTS))
    h1, ri, cnt = _merge(x2, ya, at, wgates, w_branch_a.astype(BF16), w_branch_b.astype(BF16),
                         w_out.astype(BF16), row(ln1_g), row(ln1_b), w_router, row(b_router), tm=512)

    tm_e = 512
    tile_e, n_used, pos, n_rows = _routing_tables(ri[:, 0:2].astype(I32), ri[:, 4:6].astype(I32),
                                                  cnt[0, :N_EXPERTS].astype(I32), tm_e)
    tm_f = 256
    pos0 = pos[:, 0].reshape(T // tm_f, 1, tm_f)
    pos1 = pos[:, 1].reshape(T // tm_f, 1, tm_f)
    xs = _dispatch(pos0, pos1, h1, n_rows, tm_f)
    ys = _moe(tile_e, n_used, xs, w_gate.astype(BF16), w_up.astype(BF16), w_down.astype(BF16), tm_e)
    out = _final(pos0, pos1, h1, ri, ys, row(ln2_g), row(ln2_b), tm_f)
    return out.reshape(B, S, D_MODEL)


def kernel(x, w_in, mu_shift, w0, w_lora_up, a0, a_lora_up, g_lora_up, k_k, k_a, r_k, ln_x_g, ln_x_b, w_branch_a, idx_k_ln_g, idx_k_ln_b, rel_bias, w_branch_b, w_out, ln1_g, ln1_b, w_router_grp, b_router_grp, w_router_exp, b_router_exp, w_expert_gate, w_expert_up, w_expert_down, ln2_g, ln2_b):
    assert w_in.shape[0] == 1, "single-layer (DEPTH = 1) block"
    l = 0
    return _layer(x, w_in[l], mu_shift[l], w0[l], w_lora_up[l], a0[l], a_lora_up[l], g_lora_up[l], k_k[l],
                  k_a[l], r_k[l], ln_x_g[l], ln_x_b[l], w_branch_a[l], idx_k_ln_g[l], idx_k_ln_b[l], rel_bias,
                  w_branch_b[l], w_out[l], ln1_g[l], ln1_b[l], w_router_grp[l], b_router_grp[l],
                  w_router_exp[l], b_router_exp[l], w_expert_gate[l], w_expert_up[l], w_expert_down[l],
                  ln2_g[l], ln2_b[l])
```

```python
import functools
import math

import jax
import jax.numpy as jnp
from jax import lax
from jax.experimental import pallas as pl
from jax.experimental.pallas import tpu as pltpu

F32 = jnp.float32
BF16 = jnp.bfloat16
I32 = jnp.int32

D_MODEL = 1024
HEAD_DIM = 64
N_HEADS = 8
HDIM = N_HEADS * HEAD_DIM
DECAY_LORA = 64
AAA_LORA = 64
GATE_LORA = 128
RWKV_COLS = 3 * HDIM + DECAY_LORA + AAA_LORA + GATE_LORA
IDX_DIM = 64
MAX_TOPK = 256
N_BUCKETS = 32
MAX_EXACT = 16
MAX_DISTANCE = 128
N_GROUPS = 4
EXPERTS_PER_GROUP = 8
N_EXPERTS = 32
D_EXPERT = 512
GN_EPS = 64e-5
LN_EPS = 1e-5
ALPHA = 2.0 ** 0.25
LANES = 128
CHUNK = 64
SCAN_SUB = 4
QB = 128
KCH = 512
ICH = 512
assert ICH in (KCH, KCH // 2)
TM_INPROJ = 512
TM_PREP = 256
TM_MERGE = 512
TM_EXPERT = 512
TM_TOKEN = 256
INT_MIN = -(2 ** 31)
NEG = -1e30
LOG2E = 1.4426950408889634
VMEM_LIMIT = 56 * 1024 * 1024

NN = (((1,), (0,)), ((), ()))
NT = (((1,), (1,)), ((), ()))


def _dot(a, b, dims=NN):
    return lax.dot_general(a, b, dims, preferred_element_type=F32)


def _split2(x):
    hi = x.astype(BF16)
    lo = (x - hi.astype(F32)).astype(BF16)
    return hi, lo


def _split3(x):
    hi = x.astype(BF16)
    r1 = x - hi.astype(F32)
    mid = r1.astype(BF16)
    lo = (r1 - mid.astype(F32)).astype(BF16)
    return hi, mid, lo


def _mm1(a, b, dims=NN):
    return _dot(a.astype(BF16), b.astype(BF16), dims)


def _mm3(a, b, dims=NN):
    ah, al = _split2(a)
    bh, bl = _split2(b)
    return _dot(ah, bh, dims) + (_dot(ah, bl, dims) + _dot(al, bh, dims))


_mm_misc = _mm1
_mm_inv = _mm1
_mm_state = _mm3


def _mm_exact_lhs(a_bf, b):
    b0, b1, b2 = _split3(b)
    return _dot(a_bf, b0) + (_dot(a_bf, b1) + _dot(a_bf, b2))


def _mm_exact_rhs(a, b_bf):
    a0, a1 = _split2(a)
    return _dot(a0, b_bf) + _dot(a1, b_bf)


def _params(sem, vmem=VMEM_LIMIT):
    return pltpu.CompilerParams(dimension_semantics=sem, vmem_limit_bytes=vmem)


def _full(shape):
    nd = len(shape)
    return pl.BlockSpec(shape, lambda *_: (0,) * nd)


def _inproj_kernel(x_ref, wr_ref, wq_ref, wc_ref, lng_ref, lnb_ref,
                   pr_ref, q_ref, k_ref, ve_ref, vo_ref, qi_ref, kz_ref, kw_ref):
    xb = x_ref[...].astype(BF16)
    pr_ref[...] = _dot(xb, wr_ref[...])
    qkv = _dot(xb, wq_ref[...])
    q_ref[...] = (qkv[:, 0:HDIM] * (HEAD_DIM ** -0.5 * LOG2E)).astype(BF16)
    k_ref[...] = qkv[:, HDIM:2 * HDIM].astype(BF16)
    v = qkv[:, 2 * HDIM:3 * HDIM]
    even = (lax.broadcasted_iota(I32, v.shape, 1) % LANES) < HEAD_DIM
    ve_ref[...] = jnp.where(even, v, 1.0).astype(BF16)
    vo_ref[...] = jnp.where(even, 1.0, v).astype(BF16)
    qi_ref[...] = qkv[:, 3 * HDIM:4 * HDIM].astype(BF16)
    c = _dot(xb, wc_ref[...])
    kw_ref[...] = c
    lane = lax.broadcasted_iota(I32, c.shape, 1)
    isk = lane < IDX_DIM
    mu = jnp.sum(jnp.where(isk, c, 0.0), axis=1, keepdims=True) * (1.0 / IDX_DIM)
    d = jnp.where(isk, c - mu, 0.0)
    var = jnp.sum(d * d, axis=1, keepdims=True) * (1.0 / IDX_DIM)
    kn = d * lax.rsqrt(var + LN_EPS) * lng_ref[...] + lnb_ref[...]
    kz_ref[:, 0:LANES] = kn.astype(BF16)
    kz_ref[:, LANES:2 * LANES] = pltpu.roll(kn, IDX_DIM, 1).astype(BF16)


def _inproj(x2, wr, wq, wc, lng, lnb, tm):
    T = x2.shape[0]
    row = lambda n: pl.BlockSpec((tm, n), lambda i: (i, 0))
    return pl.pallas_call(
        _inproj_kernel,
        grid=(T // tm,),
        in_specs=[row(D_MODEL), _full(wr.shape), _full(wq.shape), _full(wc.shape),
                  _full(lng.shape), _full(lnb.shape)],
        out_specs=[row(RWKV_COLS), row(HDIM), row(HDIM), row(HDIM), row(HDIM), row(HDIM),
                   row(2 * LANES), row(LANES)],
        out_shape=[jax.ShapeDtypeStruct((T, RWKV_COLS), F32)]
        + [jax.ShapeDtypeStruct((T, HDIM), BF16)] * 5
        + [jax.ShapeDtypeStruct((T, 2 * LANES), BF16), jax.ShapeDtypeStruct((T, LANES), F32)],
        compiler_params=_params(("parallel",)),
        name="inproj",
    )(x2, wr, wq, wc, lng, lnb)


def _softplus(x):
    return jnp.maximum(x, 0.0) + jnp.log(1.0 + jnp.exp(-jnp.abs(x)))


def _sigmoid(x):
    return 1.0 / (1.0 + jnp.exp(-x))


def _prep_kernel(tiles_per_seq, p_ref, pp_ref, mu_ref, w0_ref, a0_ref, kk_ref, ka_ref, rk_ref,
                 wup_ref, aup_ref, gup_ref, bd_ref,
                 r_ref, lw_ref, k_ref, v_ref, a_ref, b_ref, g_ref, bon_ref):
    i = pl.program_id(0)
    p = p_ref[...]
    tm = p.shape[0]
    first = (i % tiles_per_seq) == 0
    prow = jnp.where(first, 0.0, pp_ref[7:8, :])
    rowid = lax.broadcasted_iota(I32, p.shape, 0)
    prev = jnp.where(rowid == 0, prow, pltpu.roll(p, 1, 0))
    ps = p + (prev - p) * mu_ref[...]
    r = ps[:, 0:HDIM]
    k = ps[:, HDIM:2 * HDIM]
    v = ps[:, 2 * HDIM:3 * HDIM]
    da = ps[:, 3 * HDIM:3 * HDIM + LANES]
    gd = ps[:, 3 * HDIM + LANES:3 * HDIM + 2 * LANES]
    w = -_softplus(-(w0_ref[...] + _mm3(jnp.tanh(da), wup_ref[...]))) - 0.5
    lw_ref[...] = -jnp.exp(w)
    a = _sigmoid(a0_ref[...] + _mm3(da, aup_ref[...]))
    g_ref[...] = _mm3(_sigmoid(gd), gup_ref[...])
    bd = bd_ref[...]
    kk = k * kk_ref[...]
    ss = _mm_exact_rhs(kk * kk, bd)
    kk = kk / jnp.maximum(jnp.sqrt(ss), 1e-12)
    k2 = k * (1.0 + (a - 1.0) * ka_ref[...])
    r_ref[...] = r
    k_ref[...] = k2
    v_ref[...] = v
    a_ref[...] = -kk
    b_ref[...] = kk * a
    bon_ref[...] = _mm_exact_rhs(r * k2 * rk_ref[...], bd) * v


def _rwkv_prep(pr, S, prm, tm):
    T = pr.shape[0]
    row = lambda n: pl.BlockSpec((tm, n), lambda i: (i, 0))
    prev = pl.BlockSpec((8, RWKV_COLS), lambda i: (jnp.maximum(i * (tm // 8) - 1, 0), 0))
    names = ["mu", "w0", "a0", "k_k", "k_a", "r_k", "wup", "aup", "gup", "bd"]
    return pl.pallas_call(
        functools.partial(_prep_kernel, S // tm),
        grid=(T // tm,),
        in_specs=[row(RWKV_COLS), prev] + [_full(prm[n].shape) for n in names],
        out_specs=[row(HDIM)] * 8,
        out_shape=[jax.ShapeDtypeStruct((T, HDIM), F32)] * 8,
        compiler_params=_params(("parallel",)),
        name="rwkv_prep",
    )(pr, pr, *[prm[n] for n in names])


def _scan_kernel(r_ref, lw_ref, k_ref, v_ref, a_ref, b_ref, g_ref, bon_ref, lng_ref, lnb_ref, bd_ref,
                 o_ref, st_ref, y_ref):
    C, N, H = CHUNK, HEAD_DIM, N_HEADS

    @pl.when(pl.program_id(1) == 0)
    def _():
        st_ref[...] = jnp.zeros(st_ref.shape, F32)

    ri = lax.broadcasted_iota(I32, (C, C), 0)
    ci = lax.broadcasted_iota(I32, (C, C), 1)
    incl = ri >= ci
    strict = ri > ci
    eye = ri == ci
    eye_f = jnp.where(eye, 1.0, 0.0)
    lmat = jnp.where(incl, 1.0, 0.0).astype(BF16)
    sls = [slice(h * N, (h + 1) * N) for h in range(H)]
    units = [(s, h) for s in range(SCAN_SUB) for h in range(H)]
    ah, rh, vh, bT, kT, bhT, khT, gam = {}, {}, {}, {}, {}, {}, {}, {}
    for s in range(SCAN_SUB):
        rows = slice(s * C, (s + 1) * C)
        lw = lw_ref[rows, :]
        cum = _mm_exact_lhs(lmat, lw)
        last = cum[C - 1:C, :]
        e_i = jnp.exp(-cum)
        e_end = jnp.exp(last - cum)
        g_s = jnp.exp(last)
        r_t = r_ref[rows, :] * jnp.exp(cum)
        a_t = a_ref[rows, :] * jnp.exp(cum - lw)
        v_s = v_ref[rows, :]
        b_s, k_s = b_ref[rows, :], k_ref[rows, :]
        bT_s, kT_s = (b_s * e_i).T, (k_s * e_i).T
        bhT_s, khT_s = (b_s * e_end).T, (k_s * e_end).T
        for h in range(H):
            un = (s, h)
            ah[un], rh[un], vh[un] = a_t[:, sls[h]], r_t[:, sls[h]], v_s[:, sls[h]]
            bT[un], kT[un] = bT_s[sls[h], :], kT_s[sls[h], :]
            bhT[un], khT[un] = bhT_s[sls[h], :], khT_s[sls[h], :]
            gam[un] = g_s[:, sls[h]]

    ar = {un: jnp.concatenate([ah[un], rh[un]], axis=0) for un in units}
    sb = {un: _mm_misc(ar[un], bT[un]) for un in units}
    sk = {un: _mm_misc(ar[un], kT[un]) for un in units}
    a_ab = {un: jnp.where(strict, sb[un][:C], 0.0) for un in units}
    a_rb = {un: jnp.where(incl, sb[un][C:], 0.0) for un in units}
    a_ak = {un: jnp.where(strict, sk[un][:C], 0.0) for un in units}
    a_rk = {un: jnp.where(incl, sk[un][C:], 0.0) for un in units}
    u = {un: _mm_misc(a_ak[un], vh[un]) for un in units}
    tinv = {un: eye_f + a_ab[un] for un in units}
    xp = a_ab
    for _ in range(5):
        xp = {un: _mm_inv(xp[un], xp[un]) for un in units}
        tinv = {un: tinv[un] + _mm_inv(tinv[un], xp[un]) for un in units}
    pm = {un: _mm_misc(tinv[un], ah[un]) for un in units}
    qm = {un: _mm_misc(tinv[un], u[un]) for un in units}
    r2 = {un: rh[un] + _mm_misc(a_rb[un], pm[un]) for un in units}
    mmat = {un: jnp.where(eye, gam[un], 0.0) + _mm_misc(bhT[un], pm[un]) for un in units}
    y0 = {un: _mm_misc(a_rb[un], qm[un]) + _mm_misc(a_rk[un], vh[un]) for un in units}
    gmat = {un: _mm_misc(bhT[un], qm[un]) + _mm_misc(khT[un], vh[un]) for un in units}
    st = [st_ref[h] for h in range(H)]
    for s in range(SCAN_SUB):
        for h in range(H):
            y_ref[s * C:(s + 1) * C, sls[h]] = _mm_state(r2[(s, h)], st[h]) + y0[(s, h)]
        st = [_mm_state(mmat[(s, h)], st[h]) + gmat[(s, h)] for h in range(H)]
    for h in range(H):
        st_ref[h] = st[h]

    y = y_ref[...]
    bd = bd_ref[...]
    mu = _mm_exact_rhs(y, bd) * (1.0 / N)
    d = y - mu
    var = _mm_exact_rhs(d * d, bd) * (1.0 / N)
    yn = d * lax.rsqrt(var + GN_EPS) * lng_ref[...] + lnb_ref[...]
    o_ref[...] = ((yn + bon_ref[...]) * g_ref[...]).astype(BF16)


def _rwkv_scan(arrs, lng, lnb, bd, B, S):
    rows = SCAN_SUB * CHUNK
    nc = S // rows
    row = pl.BlockSpec((rows, HDIM), lambda b, c: (b * nc + c, 0))
    return pl.pallas_call(
        _scan_kernel,
        grid=(B, nc),
        in_specs=[row] * 8 + [_full(lng.shape), _full(lnb.shape), _full(bd.shape)],
        out_specs=row,
        out_shape=jax.ShapeDtypeStruct((B * S, HDIM), BF16),
        scratch_shapes=[pltpu.VMEM((N_HEADS, HEAD_DIM, HEAD_DIM), F32), pltpu.VMEM((rows, HDIM), F32)],
        compiler_params=_params(("parallel", "arbitrary")),
        name="rwkv_scan",
    )(*arrs, lng, lnb, bd)


def _sort_key(x):
    bits = pltpu.bitcast(x, I32)
    return bits ^ ((bits >> 31) & 0x7FFFFFFF)


def _attn_kernel(top_k, rb_ref, q_ref, k_ref, ve_ref, vo_ref, qi_ref, kz_ref, kw_ref, o_ref,
                 key_t, mbias, btab, qm, s_scr, p_scr, *state):
    m_scr, acc, a_scr = (state[n * N_HEADS:(n + 1) * N_HEADS] for n in range(3))
    i = pl.program_id(1)
    t0 = i * QB
    lane = lax.broadcasted_iota(I32, (QB, LANES), 1)
    rowi = lax.broadcasted_iota(I32, (QB, LANES), 0)

    @pl.when(i == 0)
    def _build_bias():
        for h in range(N_HEADS):
            btab[h, 2] = jnp.zeros((QB, LANES), F32)
        for m in range(2):
            n = jnp.maximum(m * QB + rowi - lane, 0)
            nf = jnp.maximum(n, 1).astype(F32)
            large = MAX_EXACT + (jnp.log(nf / MAX_EXACT) / math.log(MAX_DISTANCE / MAX_EXACT)
                                 * (N_BUCKETS - MAX_EXACT)).astype(I32)
            bucket = jnp.where(n < MAX_EXACT, n, jnp.minimum(large, N_BUCKETS - 1))
            for h in range(N_HEADS):
                t = jnp.zeros((QB, LANES), F32)
                for bk in range(N_BUCKETS):
                    t = jnp.where(bucket == bk, rb_ref[bk, h], t)
                btab[h, m] = (t - rb_ref[N_BUCKETS - 1, h]) * LOG2E

    n_ch = i // (KCH // QB) + 1
    n_ic = i // (ICH // QB) + 1
    nsub, isub = KCH // LANES, ICH // LANES

    kw_t = kw_ref[...].T
    w_t = [kw_t[IDX_DIM + h:IDX_DIM + h + 1, :] * (N_HEADS ** -0.5) * (IDX_DIM ** -0.5) for h in range(N_HEADS)]

    def score_chunk(c, carry):
        c0 = pl.multiple_of(c * ICH, ICH)
        tot = [jnp.zeros((LANES, QB), F32) for _ in range(isub)]
        for p in range(N_HEADS // 2):
            qp = qi_ref[:, p * LANES:(p + 1) * LANES]
            for e in range(2):
                z = _dot(kz_ref[pl.ds(c0, ICH), e * LANES:(e + 1) * LANES], qp, NT)
                for s in range(isub):
                    tot[s] = tot[s] + jnp.maximum(z[s * LANES:(s + 1) * LANES, :], 0.0) * w_t[2 * p + e]
        for s in range(isub):
            causal = c0 + s * LANES + rowi <= t0 + lane
            key_t[pl.ds(pl.multiple_of(c0 + s * LANES, LANES), LANES), :] = \
                jnp.where(causal, _sort_key(tot[s]), INT_MIN)
        return carry
    lax.fori_loop(0, n_ic, score_chunk, 0)

    def count_ge(cand):
        def body(c, cnt):
            kc = key_t[pl.ds(pl.multiple_of(c * ICH, ICH), ICH), :]
            ones = jnp.where(kc >= cand, 1, 0)
            return cnt + jnp.sum(ones.reshape(ICH // 8, 8, QB), axis=0)
        cnt = lax.fori_loop(0, n_ic, body, jnp.zeros((8, QB), I32))
        return jnp.sum(cnt, axis=0, keepdims=True)

    def search(bit, carry):
        u, cu = carry
        cand = u | (jnp.int32(1) << (31 - bit))
        cnt = count_ge(cand ^ INT_MIN)
        ok = cnt >= top_k
        return jnp.where(ok, cand, u), jnp.where(ok, cnt, cu)

    few = (t0 + lax.broadcasted_iota(I32, (1, QB), 1)) < top_k

    def unsettled(carry):
        g, _, cu = carry
        return (g < 8) & (jnp.min(jnp.where(few | (cu == top_k), 1, 0)) == 0)

    def four_bits(carry):
        g, u, cu = carry
        u, cu = lax.fori_loop(4 * g, 4 * g + 4, search, (u, cu))
        return g + 1, u, cu

    u, cu = lax.fori_loop(0, 20, search, (jnp.zeros((1, QB), I32), jnp.zeros((1, QB), I32)))
    _, u, cu = lax.while_loop(unsettled, four_bits, (jnp.int32(5), u, cu))
    thr = jnp.maximum(u ^ INT_MIN, INT_MIN + 1)

    @pl.when(jnp.max(cu) > top_k)
    def _fix_ties():
        budget = (top_k - count_ge(thr + 1)).astype(F32)
        kr = lax.broadcasted_iota(I32, (ICH, ICH), 0)
        kc_ = lax.broadcasted_iota(I32, (ICH, ICH), 1)
        lt = jnp.where(kr >= kc_, 1.0, 0.0).astype(BF16)

        def body(c, before):
            sl = pl.ds(pl.multiple_of(c * ICH, ICH), ICH)
            kc = key_t[sl, :]
            eq = kc == thr
            eqf = jnp.where(eq, 1.0, 0.0)
            upto = before + _dot(lt, eqf.astype(BF16))
            key_t[sl, :] = jnp.where(eq & (upto - eqf >= budget), INT_MIN, kc)
            return upto[ICH - 1:ICH, :]
        lax.fori_loop(0, n_ic, body, jnp.zeros((1, QB), F32))

    def mask_chunk(c, carry):
        c0 = pl.multiple_of(c * ICH, ICH)
        for s in range(isub):
            sl = pl.ds(pl.multiple_of(c0 + s * LANES, LANES), LANES)
            mbias[:, sl] = jnp.where(key_t[sl, :] >= thr, 0.0, NEG).T
        return carry
    lax.fori_loop(0, n_ic, mask_chunk, 0)

    if ICH < KCH:
        @pl.when(n_ic * ICH < n_ch * KCH)
        def _():
            mbias[:, pl.ds(pl.multiple_of(n_ic * ICH, ICH), ICH)] = jnp.full((QB, ICH), NEG, F32)

    q = q_ref[...]
    even = lane < HEAD_DIM
    for p in range(N_HEADS // 2):
        qp = q[:, p * LANES:(p + 1) * LANES]
        qm[2 * p] = jnp.where(even, qp, jnp.zeros_like(qp))
        qm[2 * p + 1] = jnp.where(even, jnp.zeros_like(qp), qp)
    for h in range(N_HEADS):
        m_scr[h][...] = jnp.full((QB, LANES), NEG, F32)
        acc[h][...] = jnp.zeros((QB, LANES), F32)

    def key_block(c, near):
        start = pl.multiple_of(c * KCH, KCH)
        rows = pl.ds(start, KCH)
        nsub = KCH // LANES
        for h in range(N_HEADS):
            cols = slice((h // 2) * LANES, (h // 2 + 1) * LANES)
            s_scr[h] = _dot(qm[h], k_ref[rows, cols], NT)
        for h in range(N_HEADS):
            sub = []
            for n in range(nsub):
                sc = s_scr[h, :, n * LANES:(n + 1) * LANES] + mbias[:, pl.ds(start + n * LANES, LANES)]
                if near:
                    sc = sc + btab[h, jnp.clip(i - (c * nsub + n), 0, 2)]
                sub.append(sc)
            mx = jnp.maximum(jnp.maximum(sub[0], sub[1]), jnp.maximum(sub[2], sub[3]))
            m_old = m_scr[h][...]
            m_new = jnp.maximum(m_old, jnp.max(mx, axis=1, keepdims=True))
            for n in range(nsub):
                p_scr[h, :, n * LANES:(n + 1) * LANES] = jnp.exp2(sub[n] - m_new).astype(BF16)
            a_scr[h][...] = jnp.exp2(m_old - m_new)
            m_scr[h][...] = m_new
        for h in range(N_HEADS):
            cols = slice((h // 2) * LANES, (h // 2 + 1) * LANES)
            v_ref = vo_ref if h % 2 else ve_ref
            acc[h][...] = a_scr[h][...] * acc[h][...] + _dot(p_scr[h], v_ref[rows, cols])

    n_far = jnp.maximum(i - 1, 0) // (KCH // QB)

    def far_body(c, carry):
        key_block(c, False)
        return carry
    lax.fori_loop(0, n_far, far_body, 0)

    def near_body(c, carry):
        key_block(c, True)
        return carry
    lax.fori_loop(n_far, n_ch, near_body, 0)

    for p in range(N_HEADS // 2):
        ae, ao = acc[2 * p][...], acc[2 * p + 1][...]
        oe = ae / pltpu.roll(ae, HEAD_DIM, 1)
        oo = ao / pltpu.roll(ao, HEAD_DIM, 1)
        o_ref[:, p * LANES:(p + 1) * LANES] = jnp.where(even, oe, oo).astype(BF16)


def _attention(q, k, ve, vo, qi, kz, kw, rel_bias, B, S):
    nq = S // QB
    top_k = min(MAX_TOPK, S // 4)
    qrow = lambda n: pl.BlockSpec((QB, n), lambda b, i: (b * nq + i, 0))
    seq = lambda n: pl.BlockSpec((S, n), lambda b, i: (b, 0))
    return pl.pallas_call(
        functools.partial(_attn_kernel, top_k),
        grid=(B, nq),
        in_specs=[pl.BlockSpec(memory_space=pltpu.SMEM), qrow(HDIM), seq(HDIM), seq(HDIM), seq(HDIM),
                  qrow(HDIM), seq(2 * LANES), qrow(LANES)],
        out_specs=qrow(HDIM),
        out_shape=jax.ShapeDtypeStruct((B * S, HDIM), BF16),
        scratch_shapes=[pltpu.VMEM((S, QB), I32),
                        pltpu.VMEM((QB, S), F32),
                        pltpu.VMEM((N_HEADS, 3, QB, LANES), F32),
                        pltpu.VMEM((N_HEADS, QB, LANES), BF16),
                        pltpu.VMEM((N_HEADS, QB, KCH), F32),
                        pltpu.VMEM((N_HEADS, QB, KCH), BF16)]
        + [pltpu.VMEM((QB, LANES), F32)] * (3 * N_HEADS),
        compiler_params=_params(("parallel", "arbitrary")),
        name="dsa_attention",
    )(rel_bias, q, k, ve, vo, qi, kz, kw)


def _layer_norm(x, g, b):
    mu = jnp.mean(x, axis=1, keepdims=True)
    d = x - mu
    var = jnp.mean(d * d, axis=1, keepdims=True)
    return d * lax.rsqrt(var + LN_EPS) * g + b


def _merge_kernel(x_ref, ya_ref, at_ref, wg_ref, wa_ref, wb_ref, wo_ref, g1_ref, b1_ref, wr_ref, br_ref,
                  h_ref, ri_ref, cnt_ref):
    @pl.when(pl.program_id(0) == 0)
    def _():
        cnt_ref[...] = jnp.zeros(cnt_ref.shape, F32)
    half = x_ref.shape[0] // 2
    halves = [slice(sb * half, (sb + 1) * half) for sb in range(2)]
    pre = []
    for rows in halves:
        g = _dot(x_ref[rows, :].astype(BF16), wg_ref[...])
        pre.append((g, _dot(ya_ref[rows, :], wa_ref[...]), _dot(at_ref[rows, :], wb_ref[...])))
    mixes = []
    for g, ya, yb in pre:
        mixin = _sigmoid(g[:, :D_MODEL]) * ya + _sigmoid(g[:, D_MODEL:]) * yb
        mixes.append(_dot(mixin.astype(BF16), wo_ref[...]))
    for rows, mix in zip(halves, mixes):
        h_ref[rows, :] = _layer_norm(ALPHA * x_ref[rows, :] + mix, g1_ref[...], b1_ref[...])
    for rows in halves:
        _route_rows(h_ref.at[rows], wr_ref, br_ref, ri_ref.at[rows], cnt_ref)


def _route_rows(h_ref, wr_ref, br_ref, ri_ref, cnt_ref):
    lg = _mm3(h_ref[...], wr_ref[...]) + br_ref[...]
    lane = lax.broadcasted_iota(I32, lg.shape, 1)
    gl = jnp.where(lane < N_GROUPS, lg, NEG)
    gmax = jnp.max(gl, axis=1, keepdims=True)
    p_g = 1.0 / jnp.sum(jnp.exp(gl - gmax), axis=1, keepdims=True)
    gsel = jnp.min(jnp.where(gl == gmax, lane, LANES), axis=1, keepdims=True)
    lo = N_GROUPS + EXPERTS_PER_GROUP * gsel
    el = jnp.where((lane >= lo) & (lane < lo + EXPERTS_PER_GROUP), lg, NEG)
    e1 = jnp.max(el, axis=1, keepdims=True)
    i1 = jnp.min(jnp.where(el == e1, lane, LANES), axis=1, keepdims=True)
    el2 = jnp.where(lane == i1, NEG, el)
    e2 = jnp.max(el2, axis=1, keepdims=True)
    i2 = jnp.min(jnp.where(el2 == e2, lane, LANES), axis=1, keepdims=True)
    w2 = jnp.exp(e2 - e1)
    gate1 = p_g / (1.0 + w2)
    gate2 = p_g * w2 / (1.0 + w2)
    tm = lg.shape[0]
    oh = jnp.concatenate([jnp.where(lane == i1 - N_GROUPS, 1.0, 0.0),
                          jnp.where(lane == i2 - N_GROUPS, 1.0, 0.0)], axis=0)
    rr = lax.broadcasted_iota(I32, (2 * tm, 2 * tm), 0)
    cc = lax.broadcasted_iota(I32, (2 * tm, 2 * tm), 1)
    before = _dot(jnp.where(rr > cc, 1.0, 0.0).astype(BF16), oh.astype(BF16))
    rank = jnp.sum((before + cnt_ref[0:1, :]) * oh, axis=1, keepdims=True)
    cnt_ref[...] = cnt_ref[...] + jnp.sum(oh, axis=0, keepdims=True)
    cols = [(i1 - N_GROUPS).astype(F32), (i2 - N_GROUPS).astype(F32), gate1, gate2, rank[:tm], rank[tm:]]
    ri = jnp.zeros(lg.shape, F32)
    for n, col in enumerate(cols):
        ri = jnp.where(lane == n, col, ri)
    ri_ref[...] = ri


def _merge(x2, ya, at, wg, wa, wb, wo, g1, b1, wr, br, tm):
    T = x2.shape[0]
    row = lambda n: pl.BlockSpec((tm, n), lambda i: (i, 0))
    ws = [wg, wa, wb, wo, g1, b1, wr, br]
    return pl.pallas_call(
        _merge_kernel,
        grid=(T // tm,),
        in_specs=[row(D_MODEL), row(HDIM), row(HDIM)] + [_full(w.shape) for w in ws],
        out_specs=[row(D_MODEL), row(LANES), _full((8, LANES))],
        out_shape=[jax.ShapeDtypeStruct((T, D_MODEL), F32), jax.ShapeDtypeStruct((T, LANES), F32),
                   jax.ShapeDtypeStruct((8, LANES), F32)],
        compiler_params=_params(("arbitrary",)),
        name="merge_router",
    )(x2, ya, at, *ws)


def _dispatch_kernel(p0_ref, p1_ref, h_ref, xs_in, xs_hbm, sem):
    del xs_in
    tm = h_ref.shape[0]

    def issue(r, carry):
        src = h_ref.at[pl.ds(r, 1)]
        pltpu.make_async_copy(src, xs_hbm.at[pl.ds(p0_ref[0, 0, r], 1)], sem).start()
        pltpu.make_async_copy(src, xs_hbm.at[pl.ds(p1_ref[0, 0, r], 1)], sem).start()
        return carry
    lax.fori_loop(0, tm, issue, 0, unroll=8)
    for _ in range(2):
        pltpu.make_async_copy(h_ref, xs_hbm.at[pl.ds(0, tm)], sem).wait()


def _dispatch(pos0, pos1, h, n_rows, tm):
    T = h.shape[0]
    idx = pl.BlockSpec((1, 1, tm), lambda i: (i, 0, 0), memory_space=pltpu.SMEM)
    return pl.pallas_call(
        _dispatch_kernel,
        grid=(T // tm,),
        in_specs=[idx, idx, pl.BlockSpec((tm, D_MODEL), lambda i: (i, 0)), pl.BlockSpec(memory_space=pl.ANY)],
        out_specs=pl.BlockSpec(memory_space=pl.ANY),
        out_shape=jax.ShapeDtypeStruct((n_rows, D_MODEL), F32),
        scratch_shapes=[pltpu.SemaphoreType.DMA(())],
        input_output_aliases={3: 0},
        compiler_params=_params(("arbitrary",)),
        name="moe_dispatch",
    )(pos0, pos1, h, jnp.zeros((n_rows, D_MODEL), F32))


def _moe_kernel(te_ref, nu_ref, xs_ref, wg_ref, wu_ref, wd_ref, o_ref):
    t = pl.program_id(0)

    @pl.when(t < nu_ref[0])
    def _():
        xb = xs_ref[...].astype(BF16)
        hg = _dot(xb, wg_ref[0])
        hu = _dot(xb, wu_ref[0])
        act = (hg * _sigmoid(hg)) * hu
        o_ref[...] = _dot(act.astype(BF16), wd_ref[0])

    @pl.when(t >= nu_ref[0])
    def _():
        o_ref[...] = jnp.zeros(o_ref.shape, F32)


def _moe(tile_e, n_used, xs, wg, wu, wd, tm):
    n_tiles = xs.shape[0] // tm
    grid_spec = pltpu.PrefetchScalarGridSpec(
        num_scalar_prefetch=2,
        grid=(n_tiles,),
        in_specs=[pl.BlockSpec((tm, D_MODEL), lambda t, te, nu: (t, 0)),
                  pl.BlockSpec((1, D_MODEL, D_EXPERT), lambda t, te, nu: (te[t], 0, 0)),
                  pl.BlockSpec((1, D_MODEL, D_EXPERT), lambda t, te, nu: (te[t], 0, 0)),
                  pl.BlockSpec((1, D_EXPERT, D_MODEL), lambda t, te, nu: (te[t], 0, 0))],
        out_specs=pl.BlockSpec((tm, D_MODEL), lambda t, te, nu: (t, 0)),
    )
    return pl.pallas_call(
        _moe_kernel,
        grid_spec=grid_spec,
        out_shape=jax.ShapeDtypeStruct((n_tiles * tm, D_MODEL), F32),
        compiler_params=_params(("arbitrary",)),
        name="moe_experts",
    )(tile_e, n_used, xs, wg, wu, wd)


def _final_kernel(p0_ref, p1_ref, n0_ref, n1_ref, h_ref, ri_ref, ys_hbm, g2_ref, b2_ref, o_ref, y0, y1, sems):
    i = pl.program_id(0)
    tm = h_ref.shape[0]
    slot = i % 2

    def issue(idx0, idx1, buf):
        def body(r, carry):
            pltpu.make_async_copy(ys_hbm.at[pl.ds(idx0[0, 0, r], 1)], y0.at[buf, pl.ds(r, 1)], sems.at[buf]).start()
            pltpu.make_async_copy(ys_hbm.at[pl.ds(idx1[0, 0, r], 1)], y1.at[buf, pl.ds(r, 1)], sems.at[buf]).start()
            return carry
        lax.fori_loop(0, tm, body, 0, unroll=8)

    @pl.when(i == 0)
    def _():
        issue(p0_ref, p1_ref, 0)

    @pl.when(i + 1 < pl.num_programs(0))
    def _():
        issue(n0_ref, n1_ref, 1 - slot)

    pltpu.make_async_copy(ys_hbm.at[pl.ds(0, tm)], y0.at[slot], sems.at[slot]).wait()
    pltpu.make_async_copy(ys_hbm.at[pl.ds(0, tm)], y1.at[slot], sems.at[slot]).wait()
    ri = ri_ref[...]
    moe = y0[slot] * ri[:, 2:3] + y1[slot] * ri[:, 3:4]
    o_ref[...] = _layer_norm(ALPHA * h_ref[...] + moe, g2_ref[...], b2_ref[...])


def _final(pos0, pos1, h, ri, ys, g2, b2, tm):
    T = h.shape[0]
    n = T // tm
    idx = pl.BlockSpec((1, 1, tm), lambda i: (i, 0, 0), memory_space=pltpu.SMEM)
    nxt = pl.BlockSpec((1, 1, tm), lambda i: (jnp.minimum(i + 1, n - 1), 0, 0), memory_space=pltpu.SMEM)
    row = lambda c: pl.BlockSpec((tm, c), lambda i: (i, 0))
    return pl.pallas_call(
        _final_kernel,
        grid=(n,),
        in_specs=[idx, idx, nxt, nxt, row(D_MODEL), row(LANES), pl.BlockSpec(memory_space=pl.ANY),
                  _full(g2.shape), _full(b2.shape)],
        out_specs=row(D_MODEL),
        out_shape=jax.ShapeDtypeStruct((T, D_MODEL), F32),
        scratch_shapes=[pltpu.VMEM((2, tm, D_MODEL), F32), pltpu.VMEM((2, tm, D_MODEL), F32),
                        pltpu.SemaphoreType.DMA((2,))],
        compiler_params=_params(("arbitrary",)),
        name="combine_ln",
    )(pos0, pos1, pos0, pos1, h, ri, ys, g2, b2)


def _routing_tables(eid, rank, sizes, tm):
    T = eid.shape[0]
    n_tiles = (2 * T) // tm + N_EXPERTS
    padded = ((sizes + tm - 1) // tm) * tm
    pad_end = jnp.cumsum(padded)
    pad_off = pad_end - padded
    pos = jnp.take(pad_off, eid) + rank
    tile_start = jnp.arange(n_tiles, dtype=I32) * tm
    tile_e = jnp.minimum(jnp.sum(tile_start[:, None] >= pad_end[None, :], axis=1), N_EXPERTS - 1).astype(I32)
    n_used = (pad_end[-1] // tm).astype(I32).reshape(1)
    return tile_e, n_used, pos, n_tiles * tm


def _block_diag_ones():
    hid = jnp.arange(HDIM, dtype=I32) // HEAD_DIM
    return (hid[:, None] == hid[None, :]).astype(BF16)


def _layer(x, w_in, mu_shift, w0, w_lora_up, a0, a_lora_up, g_lora_up, k_k, k_a, r_k, ln_x_g, ln_x_b,
           w_branch_a, idx_k_ln_g, idx_k_ln_b, rel_bias, w_branch_b, w_out, ln1_g, ln1_b,
           w_router_grp, b_router_grp, w_router_exp, b_router_exp, w_gate, w_up, w_down, ln2_g, ln2_b):
    B, S, _ = x.shape
    T = B * S
    assert x.shape[2] == D_MODEL and S % KCH == 0 and S % (SCAN_SUB * CHUNK) == 0 and S % TM_PREP == 0
    assert T % max(TM_INPROJ, TM_MERGE, TM_EXPERT, TM_TOKEN) == 0
    x2 = x.reshape(T, D_MODEL)
    row = lambda t: t.reshape(1, -1)

    c_r, c_wd, c_k, c_v, c_ad, c_gd = 0, 512, 576, 1088, 1600, 1664
    perm = jnp.concatenate([jnp.arange(c_r, c_r + 512), jnp.arange(c_k, c_k + 512), jnp.arange(c_v, c_v + 512),
                            jnp.arange(c_wd, c_wd + 64), jnp.arange(c_ad, c_ad + 64),
                            jnp.arange(c_gd, c_gd + 128)])
    o_q = RWKV_COLS
    o_c = o_q + 4 * HDIM
    o_g = o_c + IDX_DIM + N_HEADS
    wr = w_in[:, :RWKV_COLS][:, perm].astype(BF16)
    wq = w_in[:, o_q:o_c].astype(BF16)
    wc = jnp.pad(w_in[:, o_c:o_g], ((0, 0), (0, LANES - IDX_DIM - N_HEADS))).astype(BF16)
    wgates = w_in[:, o_g:].astype(BF16)
    pad_idx = lambda t: jnp.pad(t, (0, LANES - IDX_DIM)).reshape(1, LANES)

    pr, q, k, ve, vo, qi, kz, kw = _inproj(x2, wr, wq, wc, pad_idx(idx_k_ln_g), pad_idx(idx_k_ln_b), TM_INPROJ)

    bd = _block_diag_ones()
    prm = {
        "mu": row(mu_shift[perm]), "w0": row(w0), "a0": row(a0), "k_k": row(k_k), "k_a": row(k_a),
        "r_k": row(r_k),
        "wup": jnp.pad(w_lora_up, ((0, AAA_LORA), (0, 0))),
        "aup": jnp.pad(a_lora_up, ((DECAY_LORA, 0), (0, 0))),
        "gup": g_lora_up, "bd": bd,
    }
    arrs = _rwkv_prep(pr, S, prm, TM_PREP)
    ya = _rwkv_scan(arrs, row(ln_x_g), row(ln_x_b), bd, B, S)

    at = _attention(q, k, ve, vo, qi, kz, kw, rel_bias, B, S)

    w_router = jnp.pad(jnp.concatenate([w_router_grp, w_router_exp], axis=1),
                       ((0, 0), (0, LANES - N_GROUPS - N_EXPERTS)))
    b_router = jnp.pad(jnp.concatenate([b_router_grp, b_router_exp]), (0, LANES - N_GROUPS - N_EXPERTS))
    h1, ri, cnt = _merge(x2, ya, at, wgates, w_branch_a.astype(BF16), w_branch_b.astype(BF16),
                         w_out.astype(BF16), row(ln1_g), row(ln1_b), w_router, row(b_router), TM_MERGE)

    tile_e, n_used, pos, n_rows = _routing_tables(ri[:, 0:2].astype(I32), ri[:, 4:6].astype(I32),
                                                  cnt[0, :N_EXPERTS].astype(I32), TM_EXPERT)
    pos0 = pos[:, 0].reshape(T // TM_TOKEN, 1, TM_TOKEN)
    pos1 = pos[:, 1].reshape(T // TM_TOKEN, 1, TM_TOKEN)
    xs = _dispatch(pos0, pos1, h1, n_rows, TM_TOKEN)
    ys = _moe(tile_e, n_used, xs, w_gate.astype(BF16), w_up.astype(BF16), w_down.astype(BF16), TM_EXPERT)
    out = _final(pos0, pos1, h1, ri, ys, row(ln2_g), row(ln2_b), TM_TOKEN)
    return out.reshape(B, S, D_MODEL)


def kernel(x, w_in, mu_shift, w0, w_lora_up, a0, a_lora_up, g_lora_up, k_k, k_a, r_k, ln_x_g, ln_x_b, w_branch_a, idx_k_ln_g, idx_k_ln_b, rel_bias, w_branch_b, w_out, ln1_g, ln1_b, w_router_grp, b_router_grp, w_router_exp, b_router_exp, w_expert_gate, w_expert_up, w_expert_down, ln2_g, ln2_b):
    assert w_in.shape[0] == 1, "single-layer (DEPTH = 1) block"
    l = 0
    return _layer(x, w_in[l], mu_shift[l], w0[l], w_lora_up[l], a0[l], a_lora_up[l], g_lora_up[l], k_k[l],
                  k_a[l], r_k[l], ln_x_g[l], ln_x_b[l], w_branch_a[l], idx_k_ln_g[l], idx_k_ln_b[l], rel_bias,
                  w_branch_b[l], w_out[l], ln1_g[l], ln1_b[l], w_router_grp[l], b_router_grp[l],
                  w_router_exp[l], b_router_exp[l], w_expert_gate[l], w_expert_up[l], w_expert_down[l],
                  ln2_g[l], ln2_b[l])
```

```python
import functools
import math

import jax
import jax.numpy as jnp
from jax import lax
from jax.experimental import pallas as pl
from jax.experimental.pallas import tpu as pltpu

F32 = jnp.float32
BF16 = jnp.bfloat16
I32 = jnp.int32

D_MODEL = 1024
HEAD_DIM = 64
N_HEADS = 8
HDIM = N_HEADS * HEAD_DIM
DECAY_LORA = 64
AAA_LORA = 64
GATE_LORA = 128
RWKV_COLS = 3 * HDIM + DECAY_LORA + AAA_LORA + GATE_LORA
IDX_DIM = 64
MAX_TOPK = 256
N_BUCKETS = 32
MAX_EXACT = 16
MAX_DISTANCE = 128
N_GROUPS = 4
EXPERTS_PER_GROUP = 8
N_EXPERTS = 32
D_EXPERT = 512
GN_EPS = 64e-5
LN_EPS = 1e-5
ALPHA = 2.0 ** 0.25
LANES = 128
CHUNK = 64
SCAN_SUB = 4
QB = 128
KCH = 512
ICH = 512
assert ICH in (KCH, KCH // 2)
TM_INPROJ = 512
TM_PREP = 256
TM_MERGE = 512
TM_EXPERT = 512
TM_TOKEN = 256
INT_MIN = -(2 ** 31)
NEG = -1e30
LOG2E = 1.4426950408889634
VMEM_LIMIT = 56 * 1024 * 1024

NN = (((1,), (0,)), ((), ()))
NT = (((1,), (1,)), ((), ()))


def _dot(a, b, dims=NN):
    return lax.dot_general(a, b, dims, preferred_element_type=F32)


def _split2(x):
    hi = x.astype(BF16)
    lo = (x - hi.astype(F32)).astype(BF16)
    return hi, lo


def _split3(x):
    hi = x.astype(BF16)
    r1 = x - hi.astype(F32)
    mid = r1.astype(BF16)
    lo = (r1 - mid.astype(F32)).astype(BF16)
    return hi, mid, lo


def _mm1(a, b, dims=NN):
    return _dot(a.astype(BF16), b.astype(BF16), dims)


def _mm3(a, b, dims=NN):
    ah, al = _split2(a)
    bh, bl = _split2(b)
    return _dot(ah, bh, dims) + (_dot(ah, bl, dims) + _dot(al, bh, dims))


_mm_misc = _mm1
_mm_inv = _mm1
_mm_state = _mm3


def _mm_exact_lhs(a_bf, b):
    b0, b1, b2 = _split3(b)
    return _dot(a_bf, b0) + (_dot(a_bf, b1) + _dot(a_bf, b2))


def _mm_exact_rhs(a, b_bf):
    a0, a1 = _split2(a)
    return _dot(a0, b_bf) + _dot(a1, b_bf)


def _params(sem, vmem=VMEM_LIMIT):
    return pltpu.CompilerParams(dimension_semantics=sem, vmem_limit_bytes=vmem)


def _full(shape):
    nd = len(shape)
    return pl.BlockSpec(shape, lambda *_: (0,) * nd)


def _inproj_kernel(x_ref, wr_ref, wq_ref, wc_ref, lng_ref, lnb_ref,
                   pr_ref, q_ref, k_ref, ve_ref, vo_ref, qi_ref, kz_ref, kw_ref):
    xb = x_ref[...].astype(BF16)
    pr_ref[...] = _dot(xb, wr_ref[...])
    qkv = _dot(xb, wq_ref[...])
    q_ref[...] = (qkv[:, 0:HDIM] * (HEAD_DIM ** -0.5 * LOG2E)).astype(BF16)
    k_ref[...] = qkv[:, HDIM:2 * HDIM].astype(BF16)
    v = qkv[:, 2 * HDIM:3 * HDIM]
    even = (lax.broadcasted_iota(I32, v.shape, 1) % LANES) < HEAD_DIM
    ve_ref[...] = jnp.where(even, v, 1.0).astype(BF16)
    vo_ref[...] = jnp.where(even, 1.0, v).astype(BF16)
    qi_ref[...] = qkv[:, 3 * HDIM:4 * HDIM].astype(BF16)
    c = _dot(xb, wc_ref[...])
    kw_ref[...] = c
    lane = lax.broadcasted_iota(I32, c.shape, 1)
    isk = lane < IDX_DIM
    mu = jnp.sum(jnp.where(isk, c, 0.0), axis=1, keepdims=True) * (1.0 / IDX_DIM)
    d = jnp.where(isk, c - mu, 0.0)
    var = jnp.sum(d * d, axis=1, keepdims=True) * (1.0 / IDX_DIM)
    kn = d * lax.rsqrt(var + LN_EPS) * lng_ref[...] + lnb_ref[...]
    kz_ref[:, 0:LANES] = kn.astype(BF16)
    kz_ref[:, LANES:2 * LANES] = pltpu.roll(kn, IDX_DIM, 1).astype(BF16)


def _inproj(x2, wr, wq, wc, lng, lnb, tm):
    T = x2.shape[0]
    row = lambda n: pl.BlockSpec((tm, n), lambda i: (i, 0))
    return pl.pallas_call(
        _inproj_kernel,
        grid=(T // tm,),
        in_specs=[row(D_MODEL), _full(wr.shape), _full(wq.shape), _full(wc.shape),
                  _full(lng.shape), _full(lnb.shape)],
        out_specs=[row(RWKV_COLS), row(HDIM), row(HDIM), row(HDIM), row(HDIM), row(HDIM),
                   row(2 * LANES), row(LANES)],
        out_shape=[jax.ShapeDtypeStruct((T, RWKV_COLS), F32)]
        + [jax.ShapeDtypeStruct((T, HDIM), BF16)] * 5
        + [jax.ShapeDtypeStruct((T, 2 * LANES), BF16), jax.ShapeDtypeStruct((T, LANES), F32)],
        compiler_params=_params(("parallel",)),
        name="inproj",
    )(x2, wr, wq, wc, lng, lnb)


def _softplus(x):
    return jnp.maximum(x, 0.0) + jnp.log(1.0 + jnp.exp(-jnp.abs(x)))


def _sigmoid(x):
    return 1.0 / (1.0 + jnp.exp(-x))


def _prep_kernel(tiles_per_seq, p_ref, pp_ref, mu_ref, w0_ref, a0_ref, kk_ref, ka_ref, rk_ref,
                 wup_ref, aup_ref, gup_ref, bd_ref,
                 r_ref, lw_ref, k_ref, v_ref, a_ref, b_ref, g_ref, bon_ref):
    i = pl.program_id(0)
    p = p_ref[...]
    tm = p.shape[0]
    first = (i % tiles_per_seq) == 0
    prow = jnp.where(first, 0.0, pp_ref[7:8, :])
    rowid = lax.broadcasted_iota(I32, p.shape, 0)
    prev = jnp.where(rowid == 0, prow, pltpu.roll(p, 1, 0))
    ps = p + (prev - p) * mu_ref[...]
    r = ps[:, 0:HDIM]
    k = ps[:, HDIM:2 * HDIM]
    v = ps[:, 2 * HDIM:3 * HDIM]
    da = ps[:, 3 * HDIM:3 * HDIM + LANES]
    gd = ps[:, 3 * HDIM + LANES:3 * HDIM + 2 * LANES]
    w = -_softplus(-(w0_ref[...] + _mm3(jnp.tanh(da), wup_ref[...]))) - 0.5
    lw_ref[...] = -jnp.exp(w)
    a = _sigmoid(a0_ref[...] + _mm3(da, aup_ref[...]))
    g_ref[...] = _mm3(_sigmoid(gd), gup_ref[...])
    bd = bd_ref[...]
    kk = k * kk_ref[...]
    ss = _mm_exact_rhs(kk * kk, bd)
    kk = kk / jnp.maximum(jnp.sqrt(ss), 1e-12)
    k2 = k * (1.0 + (a - 1.0) * ka_ref[...])
    r_ref[...] = r
    k_ref[...] = k2
    v_ref[...] = v
    a_ref[...] = -kk
    b_ref[...] = kk * a
    bon_ref[...] = _mm_exact_rhs(r * k2 * rk_ref[...], bd) * v


def _rwkv_prep(pr, S, prm, tm):
    T = pr.shape[0]
    row = lambda n: pl.BlockSpec((tm, n), lambda i: (i, 0))
    prev = pl.BlockSpec((8, RWKV_COLS), lambda i: (jnp.maximum(i * (tm // 8) - 1, 0), 0))
    names = ["mu", "w0", "a0", "k_k", "k_a", "r_k", "wup", "aup", "gup", "bd"]
    return pl.pallas_call(
        functools.partial(_prep_kernel, S // tm),
        grid=(T // tm,),
        in_specs=[row(RWKV_COLS), prev] + [_full(prm[n].shape) for n in names],
        out_specs=[row(HDIM)] * 8,
        out_shape=[jax.ShapeDtypeStruct((T, HDIM), F32)] * 8,
        compiler_params=_params(("parallel",)),
        name="rwkv_prep",
    )(pr, pr, *[prm[n] for n in names])


def _scan_kernel(r_ref, lw_ref, k_ref, v_ref, a_ref, b_ref, g_ref, bon_ref, lng_ref, lnb_ref, bd_ref,
                 o_ref, st_ref, y_ref):
    C, N, H = CHUNK, HEAD_DIM, N_HEADS

    @pl.when(pl.program_id(1) == 0)
    def _():
        st_ref[...] = jnp.zeros(st_ref.shape, F32)

    ri = lax.broadcasted_iota(I32, (C, C), 0)
    ci = lax.broadcasted_iota(I32, (C, C), 1)
    incl = ri >= ci
    strict = ri > ci
    eye = ri == ci
    eye_f = jnp.where(eye, 1.0, 0.0)
    lmat = jnp.where(incl, 1.0, 0.0).astype(BF16)
    sls = [slice(h * N, (h + 1) * N) for h in range(H)]
    units = [(s, h) for s in range(SCAN_SUB) for h in range(H)]
    ah, rh, vh, bT, kT, bhT, khT, gam = {}, {}, {}, {}, {}, {}, {}, {}
    for s in range(SCAN_SUB):
        rows = slice(s * C, (s + 1) * C)
        lw = lw_ref[rows, :]
        cum = _mm_exact_lhs(lmat, lw)
        last = cum[C - 1:C, :]
        e_i = jnp.exp(-cum)
        e_end = jnp.exp(last - cum)
        g_s = jnp.exp(last)
        r_t = r_ref[rows, :] * jnp.exp(cum)
        a_t = a_ref[rows, :] * jnp.exp(cum - lw)
        v_s = v_ref[rows, :]
        b_s, k_s = b_ref[rows, :], k_ref[rows, :]
        bT_s, kT_s = (b_s * e_i).T, (k_s * e_i).T
        bhT_s, khT_s = (b_s * e_end).T, (k_s * e_end).T
        for h in range(H):
            un = (s, h)
            ah[un], rh[un], vh[un] = a_t[:, sls[h]], r_t[:, sls[h]], v_s[:, sls[h]]
            bT[un], kT[un] = bT_s[sls[h], :], kT_s[sls[h], :]
            bhT[un], khT[un] = bhT_s[sls[h], :], khT_s[sls[h], :]
            gam[un] = g_s[:, sls[h]]

    ar = {un: jnp.concatenate([ah[un], rh[un]], axis=0) for un in units}
    sb = {un: _mm_misc(ar[un], bT[un]) for un in units}
    sk = {un: _mm_misc(ar[un], kT[un]) for un in units}
    a_ab = {un: jnp.where(strict, sb[un][:C], 0.0) for un in units}
    a_rb = {un: jnp.where(incl, sb[un][C:], 0.0) for un in units}
    a_ak = {un: jnp.where(strict, sk[un][:C], 0.0) for un in units}
    a_rk = {un: jnp.where(incl, sk[un][C:], 0.0) for un in units}
    u = {un: _mm_misc(a_ak[un], vh[un]) for un in units}
    tinv = {un: eye_f + a_ab[un] for un in units}
    xp = a_ab
    for _ in range(5):
        xp = {un: _mm_inv(xp[un], xp[un]) for un in units}
        tinv = {un: tinv[un] + _mm_inv(tinv[un], xp[un]) for un in units}
    pm = {un: _mm_misc(tinv[un], ah[un]) for un in units}
    qm = {un: _mm_misc(tinv[un], u[un]) for un in units}
    r2 = {un: rh[un] + _mm_misc(a_rb[un], pm[un]) for un in units}
    mmat = {un: jnp.where(eye, gam[un], 0.0) + _mm_misc(bhT[un], pm[un]) for un in units}
    y0 = {un: _mm_misc(a_rb[un], qm[un]) + _mm_misc(a_rk[un], vh[un]) for un in units}
    gmat = {un: _mm_misc(bhT[un], qm[un]) + _mm_misc(khT[un], vh[un]) for un in units}
    st = [st_ref[h] for h in range(H)]
    for s in range(SCAN_SUB):
        for h in range(H):
            y_ref[s * C:(s + 1) * C, sls[h]] = _mm_misc(r2[(s, h)], st[h]) + y0[(s, h)]
        st = [_mm_state(mmat[(s, h)], st[h]) + gmat[(s, h)] for h in range(H)]
    for h in range(H):
        st_ref[h] = st[h]

    y = y_ref[...]
    bd = bd_ref[...]
    mu = _mm_exact_rhs(y, bd) * (1.0 / N)
    d = y - mu
    var = _mm_exact_rhs(d * d, bd) * (1.0 / N)
    yn = d * lax.rsqrt(var + GN_EPS) * lng_ref[...] + lnb_ref[...]
    o_ref[...] = ((yn + bon_ref[...]) * g_ref[...]).astype(BF16)


def _rwkv_scan(arrs, lng, lnb, bd, B, S):
    rows = SCAN_SUB * CHUNK
    nc = S // rows
    row = pl.BlockSpec((rows, HDIM), lambda b, c: (b * nc + c, 0))
    return pl.pallas_call(
        _scan_kernel,
        grid=(B, nc),
        in_specs=[row] * 8 + [_full(lng.shape), _full(lnb.shape), _full(bd.shape)],
        out_specs=row,
        out_shape=jax.ShapeDtypeStruct((B * S, HDIM), BF16),
        scratch_shapes=[pltpu.VMEM((N_HEADS, HEAD_DIM, HEAD_DIM), F32), pltpu.VMEM((rows, HDIM), F32)],
        compiler_params=_params(("parallel", "arbitrary")),
        name="rwkv_scan",
    )(*arrs, lng, lnb, bd)


def _sort_key(x):
    bits = pltpu.bitcast(x, I32)
    return bits ^ ((bits >> 31) & 0x7FFFFFFF)


def _attn_kernel(top_k, rb_ref, q_ref, k_ref, ve_ref, vo_ref, qi_ref, kz_ref, kw_ref, o_ref,
                 key_t, mbias, btab, qm, s_scr, p_scr, *state):
    m_scr, acc, a_scr = (state[n * N_HEADS:(n + 1) * N_HEADS] for n in range(3))
    i = pl.program_id(1)
    t0 = i * QB
    lane = lax.broadcasted_iota(I32, (QB, LANES), 1)
    rowi = lax.broadcasted_iota(I32, (QB, LANES), 0)

    @pl.when(i == 0)
    def _build_bias():
        for h in range(N_HEADS):
            btab[h, 2] = jnp.zeros((QB, LANES), F32)
        for m in range(2):
            n = jnp.maximum(m * QB + rowi - lane, 0)
            nf = jnp.maximum(n, 1).astype(F32)
            large = MAX_EXACT + (jnp.log(nf / MAX_EXACT) / math.log(MAX_DISTANCE / MAX_EXACT)
                                 * (N_BUCKETS - MAX_EXACT)).astype(I32)
            bucket = jnp.where(n < MAX_EXACT, n, jnp.minimum(large, N_BUCKETS - 1))
            for h in range(N_HEADS):
                t = jnp.zeros((QB, LANES), F32)
                for bk in range(N_BUCKETS):
                    t = jnp.where(bucket == bk, rb_ref[bk, h], t)
                btab[h, m] = (t - rb_ref[N_BUCKETS - 1, h]) * LOG2E

    n_ch = i // (KCH // QB) + 1
    n_ic = i // (ICH // QB) + 1
    nsub, isub = KCH // LANES, ICH // LANES

    kw_t = kw_ref[...].T
    w_t = [kw_t[IDX_DIM + h:IDX_DIM + h + 1, :] * (N_HEADS ** -0.5) * (IDX_DIM ** -0.5) for h in range(N_HEADS)]

    def score_chunk(c, carry):
        c0 = pl.multiple_of(c * ICH, ICH)
        tot = [jnp.zeros((LANES, QB), F32) for _ in range(isub)]
        for p in range(N_HEADS // 2):
            qp = qi_ref[:, p * LANES:(p + 1) * LANES]
            for e in range(2):
                z = _dot(kz_ref[pl.ds(c0, ICH), e * LANES:(e + 1) * LANES], qp, NT)
                for s in range(isub):
                    tot[s] = tot[s] + jnp.maximum(z[s * LANES:(s + 1) * LANES, :], 0.0) * w_t[2 * p + e]
        for s in range(isub):
            causal = c0 + s * LANES + rowi <= t0 + lane
            key_t[pl.ds(pl.multiple_of(c0 + s * LANES, LANES), LANES), :] = \
                jnp.where(causal, _sort_key(tot[s]), INT_MIN)
        return carry

    def score_pair(j, carry):
        score_chunk(2 * j, carry)
        return score_chunk(jnp.minimum(2 * j + 1, n_ic - 1), carry)
    lax.fori_loop(0, (n_ic + 1) // 2, score_pair, 0)

    def count_ge(cand):
        def body(c, cnt):
            kc = key_t[pl.ds(pl.multiple_of(c * ICH, ICH), ICH), :]
            ones = jnp.where(kc >= cand, 1, 0)
            return cnt + jnp.sum(ones.reshape(ICH // 8, 8, QB), axis=0)
        cnt = lax.fori_loop(0, n_ic, body, jnp.zeros((8, QB), I32))
        return jnp.sum(cnt, axis=0, keepdims=True)

    def search(bit, carry):
        u, cu = carry
        cand = u | (jnp.int32(1) << (31 - bit))
        cnt = count_ge(cand ^ INT_MIN)
        ok = cnt >= top_k
        return jnp.where(ok, cand, u), jnp.where(ok, cnt, cu)

    few = (t0 + lax.broadcasted_iota(I32, (1, QB), 1)) < top_k

    u, cu = search(0, (jnp.zeros((1, QB), I32), jnp.zeros((1, QB), I32)))
    at_zero = (cu >= top_k) & (count_ge(jnp.ones((1, QB), I32)) < top_k)

    def unsettled(carry):
        g, _, cu = carry
        return (g < 8) & (jnp.min(jnp.where(few | at_zero | (cu == top_k), 1, 0)) == 0)

    def four_bits(carry):
        g, u, cu = carry
        u, cu = lax.fori_loop(4 * g, 4 * g + 4, search, (u, cu))
        return g + 1, u, cu

    u, cu = lax.fori_loop(1, 20, search, (u, cu))
    _, u, cu = lax.while_loop(unsettled, four_bits, (jnp.int32(5), u, cu))
    thr = jnp.maximum(u ^ INT_MIN, INT_MIN + 1)

    @pl.when(jnp.max(cu) > top_k)
    def _fix_ties():
        budget = (top_k - count_ge(thr + 1)).astype(F32)
        kr = lax.broadcasted_iota(I32, (ICH, ICH), 0)
        kc_ = lax.broadcasted_iota(I32, (ICH, ICH), 1)
        lt = jnp.where(kr >= kc_, 1.0, 0.0).astype(BF16)

        def body(c, before):
            sl = pl.ds(pl.multiple_of(c * ICH, ICH), ICH)
            kc = key_t[sl, :]
            eq = kc == thr
            eqf = jnp.where(eq, 1.0, 0.0)
            upto = before + _dot(lt, eqf.astype(BF16))
            key_t[sl, :] = jnp.where(eq & (upto - eqf >= budget), INT_MIN, kc)
            return upto[ICH - 1:ICH, :]
        lax.fori_loop(0, n_ic, body, jnp.zeros((1, QB), F32))

    def mask_chunk(c, carry):
        c0 = pl.multiple_of(c * ICH, ICH)
        for s in range(isub):
            sl = pl.ds(pl.multiple_of(c0 + s * LANES, LANES), LANES)
            mbias[:, sl] = jnp.where(key_t[sl, :] >= thr, 0.0, NEG).T
        return carry
    lax.fori_loop(0, n_ic, mask_chunk, 0)

    if ICH < KCH:
        @pl.when(n_ic * ICH < n_ch * KCH)
        def _():
            mbias[:, pl.ds(pl.multiple_of(n_ic * ICH, ICH), ICH)] = jnp.full((QB, ICH), NEG, F32)

    q = q_ref[...]
    even = lane < HEAD_DIM
    for p in range(N_HEADS // 2):
        qp = q[:, p * LANES:(p + 1) * LANES]
        qm[2 * p] = jnp.where(even, qp, jnp.zeros_like(qp))
        qm[2 * p + 1] = jnp.where(even, jnp.zeros_like(qp), qp)
    for h in range(N_HEADS):
        m_scr[h][...] = jnp.full((QB, LANES), NEG, F32)
        acc[h][...] = jnp.zeros((QB, LANES), F32)

    def key_block(c, near):
        start = pl.multiple_of(c * KCH, KCH)
        rows = pl.ds(start, KCH)
        nsub = KCH // LANES
        for h in range(N_HEADS):
            cols = slice((h // 2) * LANES, (h // 2 + 1) * LANES)
            s_scr[h] = _dot(qm[h], k_ref[rows, cols], NT)
        for h in range(N_HEADS):
            sub = []
            for n in range(nsub):
                sc = s_scr[h, :, n * LANES:(n + 1) * LANES] + mbias[:, pl.ds(start + n * LANES, LANES)]
                if near:
                    sc = sc + btab[h, jnp.clip(i - (c * nsub + n), 0, 2)]
                sub.append(sc)
            mx = jnp.maximum(jnp.maximum(sub[0], sub[1]), jnp.maximum(sub[2], sub[3]))
            m_old = m_scr[h][...]
            m_new = jnp.maximum(m_old, jnp.max(mx, axis=1, keepdims=True))
            for n in range(nsub):
                p_scr[h, :, n * LANES:(n + 1) * LANES] = jnp.exp2(sub[n] - m_new).astype(BF16)
            a_scr[h][...] = jnp.exp2(m_old - m_new)
            m_scr[h][...] = m_new
        for h in range(N_HEADS):
            cols = slice((h // 2) * LANES, (h // 2 + 1) * LANES)
            v_ref = vo_ref if h % 2 else ve_ref
            acc[h][...] = a_scr[h][...] * acc[h][...] + _dot(p_scr[h], v_ref[rows, cols])

    n_far = jnp.maximum(i - 1, 0) // (KCH // QB)

    def far_body(c, carry):
        key_block(c, False)
        return carry
    lax.fori_loop(0, n_far, far_body, 0)

    def near_body(c, carry):
        key_block(c, True)
        return carry
    lax.fori_loop(n_far, n_ch, near_body, 0)

    for p in range(N_HEADS // 2):
        ae, ao = acc[2 * p][...], acc[2 * p + 1][...]
        oe = ae / pltpu.roll(ae, HEAD_DIM, 1)
        oo = ao / pltpu.roll(ao, HEAD_DIM, 1)
        o_ref[:, p * LANES:(p + 1) * LANES] = jnp.where(even, oe, oo).astype(BF16)


def _attention(q, k, ve, vo, qi, kz, kw, rel_bias, B, S):
    nq = S // QB
    top_k = min(MAX_TOPK, S // 4)
    qrow = lambda n: pl.BlockSpec((QB, n), lambda b, i: (b * nq + i, 0))
    seq = lambda n: pl.BlockSpec((S, n), lambda b, i: (b, 0))
    return pl.pallas_call(
        functools.partial(_attn_kernel, top_k),
        grid=(B, nq),
        in_specs=[pl.BlockSpec(memory_space=pltpu.SMEM), qrow(HDIM), seq(HDIM), seq(HDIM), seq(HDIM),
                  qrow(HDIM), seq(2 * LANES), qrow(LANES)],
        out_specs=qrow(HDIM),
        out_shape=jax.ShapeDtypeStruct((B * S, HDIM), BF16),
        scratch_shapes=[pltpu.VMEM((S, QB), I32),
                        pltpu.VMEM((QB, S), F32),
                        pltpu.VMEM((N_HEADS, 3, QB, LANES), F32),
                        pltpu.VMEM((N_HEADS, QB, LANES), BF16),
                        pltpu.VMEM((N_HEADS, QB, KCH), F32),
                        pltpu.VMEM((N_HEADS, QB, KCH), BF16)]
        + [pltpu.VMEM((QB, LANES), F32)] * (3 * N_HEADS),
        compiler_params=_params(("parallel", "arbitrary")),
        name="dsa_attention",
    )(rel_bias, q, k, ve, vo, qi, kz, kw)


def _layer_norm(x, g, b):
    mu = jnp.mean(x, axis=1, keepdims=True)
    d = x - mu
    var = jnp.mean(d * d, axis=1, keepdims=True)
    return d * lax.rsqrt(var + LN_EPS) * g + b


def _merge_kernel(x_ref, ya_ref, at_ref, wg_ref, wa_ref, wb_ref, wo_ref, g1_ref, b1_ref, wr_ref, br_ref,
                  h_ref, ri_ref, cnt_ref):
    @pl.when(pl.program_id(0) == 0)
    def _():
        cnt_ref[...] = jnp.zeros(cnt_ref.shape, F32)
    half = x_ref.shape[0] // 2
    halves = [slice(sb * half, (sb + 1) * half) for sb in range(2)]
    pre = []
    for rows in halves:
        g = _dot(x_ref[rows, :].astype(BF16), wg_ref[...])
        pre.append((g, _dot(ya_ref[rows, :], wa_ref[...]), _dot(at_ref[rows, :], wb_ref[...])))
    mixes = []
    for g, ya, yb in pre:
        mixin = _sigmoid(g[:, :D_MODEL]) * ya + _sigmoid(g[:, D_MODEL:]) * yb
        mixes.append(_dot(mixin.astype(BF16), wo_ref[...]))
    for rows, mix in zip(halves, mixes):
        h_ref[rows, :] = _layer_norm(ALPHA * x_ref[rows, :] + mix, g1_ref[...], b1_ref[...])
    for rows in halves:
        _route_rows(h_ref.at[rows], wr_ref, br_ref, ri_ref.at[rows], cnt_ref)


def _route_rows(h_ref, wr_ref, br_ref, ri_ref, cnt_ref):
    lg = _mm3(h_ref[...], wr_ref[...]) + br_ref[...]
    lane = lax.broadcasted_iota(I32, lg.shape, 1)
    gl = jnp.where(lane < N_GROUPS, lg, NEG)
    gmax = jnp.max(gl, axis=1, keepdims=True)
    p_g = 1.0 / jnp.sum(jnp.exp(gl - gmax), axis=1, keepdims=True)
    gsel = jnp.min(jnp.where(gl == gmax, lane, LANES), axis=1, keepdims=True)
    lo = N_GROUPS + EXPERTS_PER_GROUP * gsel
    el = jnp.where((lane >= lo) & (lane < lo + EXPERTS_PER_GROUP), lg, NEG)
    e1 = jnp.max(el, axis=1, keepdims=True)
    i1 = jnp.min(jnp.where(el == e1, lane, LANES), axis=1, keepdims=True)
    el2 = jnp.where(lane == i1, NEG, el)
    e2 = jnp.max(el2, axis=1, keepdims=True)
    i2 = jnp.min(jnp.where(el2 == e2, lane, LANES), axis=1, keepdims=True)
    w2 = jnp.exp(e2 - e1)
    gate1 = p_g / (1.0 + w2)
    gate2 = p_g * w2 / (1.0 + w2)
    tm = lg.shape[0]
    oh = jnp.concatenate([jnp.where(lane == i1 - N_GROUPS, 1.0, 0.0),
                          jnp.where(lane == i2 - N_GROUPS, 1.0, 0.0)], axis=0)
    rr = lax.broadcasted_iota(I32, (2 * tm, 2 * tm), 0)
    cc = lax.broadcasted_iota(I32, (2 * tm, 2 * tm), 1)
    before = _dot(jnp.where(rr > cc, 1.0, 0.0).astype(BF16), oh.astype(BF16))
    rank = jnp.sum((before + cnt_ref[0:1, :]) * oh, axis=1, keepdims=True)
    cnt_ref[...] = cnt_ref[...] + jnp.sum(oh, axis=0, keepdims=True)
    cols = [(i1 - N_GROUPS).astype(F32), (i2 - N_GROUPS).astype(F32), gate1, gate2, rank[:tm], rank[tm:]]
    ri = jnp.zeros(lg.shape, F32)
    for n, col in enumerate(cols):
        ri = jnp.where(lane == n, col, ri)
    ri_ref[...] = ri


def _merge(x2, ya, at, wg, wa, wb, wo, g1, b1, wr, br, tm):
    T = x2.shape[0]
    row = lambda n: pl.BlockSpec((tm, n), lambda i: (i, 0))
    ws = [wg, wa, wb, wo, g1, b1, wr, br]
    return pl.pallas_call(
        _merge_kernel,
        grid=(T // tm,),
        in_specs=[row(D_MODEL), row(HDIM), row(HDIM)] + [_full(w.shape) for w in ws],
        out_specs=[row(D_MODEL), row(LANES), _full((8, LANES))],
        out_shape=[jax.ShapeDtypeStruct((T, D_MODEL), F32), jax.ShapeDtypeStruct((T, LANES), F32),
                   jax.ShapeDtypeStruct((8, LANES), F32)],
        compiler_params=_params(("arbitrary",)),
        name="merge_router",
    )(x2, ya, at, *ws)


def _dispatch_kernel(p0_ref, p1_ref, h_ref, xs_in, xs_hbm, sem):
    del xs_in
    tm = h_ref.shape[0]

    def issue(r, carry):
        src = h_ref.at[pl.ds(r, 1)]
        pltpu.make_async_copy(src, xs_hbm.at[pl.ds(p0_ref[0, 0, r], 1)], sem).start()
        pltpu.make_async_copy(src, xs_hbm.at[pl.ds(p1_ref[0, 0, r], 1)], sem).start()
        return carry
    lax.fori_loop(0, tm, issue, 0, unroll=8)
    for _ in range(2):
        pltpu.make_async_copy(h_ref, xs_hbm.at[pl.ds(0, tm)], sem).wait()


def _dispatch(pos0, pos1, h, n_rows, tm):
    T = h.shape[0]
    idx = pl.BlockSpec((1, 1, tm), lambda i: (i, 0, 0), memory_space=pltpu.SMEM)
    return pl.pallas_call(
        _dispatch_kernel,
        grid=(T // tm,),
        in_specs=[idx, idx, pl.BlockSpec((tm, D_MODEL), lambda i: (i, 0)), pl.BlockSpec(memory_space=pl.ANY)],
        out_specs=pl.BlockSpec(memory_space=pl.ANY),
        out_shape=jax.ShapeDtypeStruct((n_rows, D_MODEL), F32),
        scratch_shapes=[pltpu.SemaphoreType.DMA(())],
        input_output_aliases={3: 0},
        compiler_params=_params(("arbitrary",)),
        name="moe_dispatch",
    )(pos0, pos1, h, jnp.zeros((n_rows, D_MODEL), F32))


def _moe_kernel(te_ref, nu_ref, xs_ref, wg_ref, wu_ref, wd_ref, o_ref):
    t = pl.program_id(0)

    @pl.when(t < nu_ref[0])
    def _():
        xb = xs_ref[...].astype(BF16)
        hg = _dot(xb, wg_ref[0])
        hu = _dot(xb, wu_ref[0])
        act = (hg * _sigmoid(hg)) * hu
        o_ref[...] = _dot(act.astype(BF16), wd_ref[0])

    @pl.when(t >= nu_ref[0])
    def _():
        o_ref[...] = jnp.zeros(o_ref.shape, F32)


def _moe(tile_e, n_used, xs, wg, wu, wd, tm):
    n_tiles = xs.shape[0] // tm
    grid_spec = pltpu.PrefetchScalarGridSpec(
        num_scalar_prefetch=2,
        grid=(n_tiles,),
        in_specs=[pl.BlockSpec((tm, D_MODEL), lambda t, te, nu: (t, 0)),
                  pl.BlockSpec((1, D_MODEL, D_EXPERT), lambda t, te, nu: (te[t], 0, 0)),
                  pl.BlockSpec((1, D_MODEL, D_EXPERT), lambda t, te, nu: (te[t], 0, 0)),
                  pl.BlockSpec((1, D_EXPERT, D_MODEL), lambda t, te, nu: (te[t], 0, 0))],
        out_specs=pl.BlockSpec((tm, D_MODEL), lambda t, te, nu: (t, 0)),
    )
    return pl.pallas_call(
        _moe_kernel,
        grid_spec=grid_spec,
        out_shape=jax.ShapeDtypeStruct((n_tiles * tm, D_MODEL), F32),
        compiler_params=_params(("arbitrary",)),
        name="moe_experts",
    )(tile_e, n_used, xs, wg, wu, wd)


def _final_kernel(p0_ref, p1_ref, n0_ref, n1_ref, h_ref, ri_ref, ys_hbm, g2_ref, b2_ref, o_ref, y0, y1, sems):
    i = pl.program_id(0)
    tm = h_ref.shape[0]
    slot = i % 2

    def issue(idx0, idx1, buf):
        def body(r, carry):
            pltpu.make_async_copy(ys_hbm.at[pl.ds(idx0[0, 0, r], 1)], y0.at[buf, pl.ds(r, 1)], sems.at[buf]).start()
            pltpu.make_async_copy(ys_hbm.at[pl.ds(idx1[0, 0, r], 1)], y1.at[buf, pl.ds(r, 1)], sems.at[buf]).start()
            return carry
        lax.fori_loop(0, tm, body, 0, unroll=8)

    @pl.when(i == 0)
    def _():
        issue(p0_ref, p1_ref, 0)

    @pl.when(i + 1 < pl.num_programs(0))
    def _():
        issue(n0_ref, n1_ref, 1 - slot)

    pltpu.make_async_copy(ys_hbm.at[pl.ds(0, tm)], y0.at[slot], sems.at[slot]).wait()
    pltpu.make_async_copy(ys_hbm.at[pl.ds(0, tm)], y1.at[slot], sems.at[slot]).wait()
    ri = ri_ref[...]
    moe = y0[slot] * ri[:, 2:3] + y1[slot] * ri[:, 3:4]
    o_ref[...] = _layer_norm(ALPHA * h_ref[...] + moe, g2_ref[...], b2_ref[...])


def _final(pos0, pos1, h, ri, ys, g2, b2, tm):
    T = h.shape[0]
    n = T // tm
    idx = pl.BlockSpec((1, 1, tm), lambda i: (i, 0, 0), memory_space=pltpu.SMEM)
    nxt = pl.BlockSpec((1, 1, tm), lambda i: (jnp.minimum(i + 1, n - 1), 0, 0), memory_space=pltpu.SMEM)
    row = lambda c: pl.BlockSpec((tm, c), lambda i: (i, 0))
    return pl.pallas_call(
        _final_kernel,
        grid=(n,),
        in_specs=[idx, idx, nxt, nxt, row(D_MODEL), row(LANES), pl.BlockSpec(memory_space=pl.ANY),
                  _full(g2.shape), _full(b2.shape)],
        out_specs=row(D_MODEL),
        out_shape=jax.ShapeDtypeStruct((T, D_MODEL), F32),
        scratch_shapes=[pltpu.VMEM((2, tm, D_MODEL), F32), pltpu.VMEM((2, tm, D_MODEL), F32),
                        pltpu.SemaphoreType.DMA((2,))],
        compiler_params=_params(("arbitrary",)),
        name="combine_ln",
    )(pos0, pos1, pos0, pos1, h, ri, ys, g2, b2)


def _routing_tables(eid, rank, sizes, tm):
    T = eid.shape[0]
    n_tiles = (2 * T) // tm + N_EXPERTS
    padded = ((sizes + tm - 1) // tm) * tm
    pad_end = jnp.cumsum(padded)
    pad_off = pad_end - padded
    pos = jnp.take(pad_off, eid) + rank
    tile_start = jnp.arange(n_tiles, dtype=I32) * tm
    tile_e = jnp.minimum(jnp.sum(tile_start[:, None] >= pad_end[None, :], axis=1), N_EXPERTS - 1).astype(I32)
    n_used = (pad_end[-1] // tm).astype(I32).reshape(1)
    return tile_e, n_used, pos, n_tiles * tm


def _block_diag_ones():
    hid = jnp.arange(HDIM, dtype=I32) // HEAD_DIM
    return (hid[:, None] == hid[None, :]).astype(BF16)


def _layer(x, w_in, mu_shift, w0, w_lora_up, a0, a_lora_up, g_lora_up, k_k, k_a, r_k, ln_x_g, ln_x_b,
           w_branch_a, idx_k_ln_g, idx_k_ln_b, rel_bias, w_branch_b, w_out, ln1_g, ln1_b,
           w_router_grp, b_router_grp, w_router_exp, b_router_exp, w_gate, w_up, w_down, ln2_g, ln2_b):
    B, S, _ = x.shape
    T = B * S
    assert x.shape[2] == D_MODEL and S % KCH == 0 and S % (SCAN_SUB * CHUNK) == 0 and S % TM_PREP == 0
    assert T % max(TM_INPROJ, TM_MERGE, TM_EXPERT, TM_TOKEN) == 0
    x2 = x.reshape(T, D_MODEL)
    row = lambda t: t.reshape(1, -1)

    c_r, c_wd, c_k, c_v, c_ad, c_gd = 0, 512, 576, 1088, 1600, 1664
    perm = jnp.concatenate([jnp.arange(c_r, c_r + 512), jnp.arange(c_k, c_k + 512), jnp.arange(c_v, c_v + 512),
                            jnp.arange(c_wd, c_wd + 64), jnp.arange(c_ad, c_ad + 64),
                            jnp.arange(c_gd, c_gd + 128)])
    o_q = RWKV_COLS
    o_c = o_q + 4 * HDIM
    o_g = o_c + IDX_DIM + N_HEADS
    wr = w_in[:, :RWKV_COLS][:, perm].astype(BF16)
    wq = w_in[:, o_q:o_c].astype(BF16)
    wc = jnp.pad(w_in[:, o_c:o_g], ((0, 0), (0, LANES - IDX_DIM - N_HEADS))).astype(BF16)
    wgates = w_in[:, o_g:].astype(BF16)
    pad_idx = lambda t: jnp.pad(t, (0, LANES - IDX_DIM)).reshape(1, LANES)

    pr, q, k, ve, vo, qi, kz, kw = _inproj(x2, wr, wq, wc, pad_idx(idx_k_ln_g), pad_idx(idx_k_ln_b), TM_INPROJ)

    bd = _block_diag_ones()
    prm = {
        "mu": row(mu_shift[perm]), "w0": row(w0), "a0": row(a0), "k_k": row(k_k), "k_a": row(k_a),
        "r_k": row(r_k),
        "wup": jnp.pad(w_lora_up, ((0, AAA_LORA), (0, 0))),
        "aup": jnp.pad(a_lora_up, ((DECAY_LORA, 0), (0, 0))),
        "gup": g_lora_up, "bd": bd,
    }
    arrs = _rwkv_prep(pr, S, prm, TM_PREP)
    ya = _rwkv_scan(arrs, row(ln_x_g), row(ln_x_b), bd, B, S)

    at = _attention(q, k, ve, vo, qi, kz, kw, rel_bias, B, S)

    w_router = jnp.pad(jnp.concatenate([w_router_grp, w_router_exp], axis=1),
                       ((0, 0), (0, LANES - N_GROUPS - N_EXPERTS)))
    b_router = jnp.pad(jnp.concatenate([b_router_grp, b_router_exp]), (0, LANES - N_GROUPS - N_EXPERTS))
    h1, ri, cnt = _merge(x2, ya, at, wgates, w_branch_a.astype(BF16), w_branch_b.astype(BF16),
                         w_out.astype(BF16), row(ln1_g), row(ln1_b), w_router, row(b_router), TM_MERGE)

    tile_e, n_used, pos, n_rows = _routing_tables(ri[:, 0:2].astype(I32), ri[:, 4:6].astype(I32),
                                                  cnt[0, :N_EXPERTS].astype(I32), TM_EXPERT)
    pos0 = pos[:, 0].reshape(T // TM_TOKEN, 1, TM_TOKEN)
    pos1 = pos[:, 1].reshape(T // TM_TOKEN, 1, TM_TOKEN)
    xs = _dispatch(pos0, pos1, h1, n_rows, TM_TOKEN)
    ys = _moe(tile_e, n_used, xs, w_gate.astype(BF16), w_up.astype(BF16), w_down.astype(BF16), TM_EXPERT)
    out = _final(pos0, pos1, h1, ri, ys, row(ln2_g), row(ln2_b), TM_TOKEN)
    return out.reshape(B, S, D_MODEL)


def kernel(x, w_in, mu_shift, w0, w_lora_up, a0, a_lora_up, g_lora_up, k_k, k_a, r_k, ln_x_g, ln_x_b, w_branch_a, idx_k_ln_g, idx_k_ln_b, rel_bias, w_branch_b, w_out, ln1_g, ln1_b, w_router_grp, b_router_grp, w_router_exp, b_router_exp, w_expert_gate, w_expert_up, w_expert_down, ln2_g, ln2_b):
    assert w_in.shape[0] == 1, "single-layer (DEPTH = 1) block"
    l = 0
    return _layer(x, w_in[l], mu_shift[l], w0[l], w_lora_up[l], a0[l], a_lora_up[l], g_lora_up[l], k_k[l],
                  k_a[l], r_k[l], ln_x_g[l], ln_x_b[l], w_branch_a[l], idx_k_ln_g[l], idx_k_ln_b[l], rel_bias,
                  w_branch_b[l], w_out[l], ln1_g[l], ln1_b[l], w_router_grp[l], b_router_grp[l],
                  w_router_exp[l], b_router_exp[l], w_expert_gate[l], w_expert_up[l], w_expert_down[l],
                  ln2_g[l], ln2_b[l])
```

```python
import functools
import math

import jax
import jax.numpy as jnp
from jax import lax
from jax.experimental import pallas as pl
from jax.experimental.pallas import tpu as pltpu

F32 = jnp.float32
BF16 = jnp.bfloat16
I32 = jnp.int32

D_MODEL = 1024
HEAD_DIM = 64
N_HEADS = 8
HDIM = N_HEADS * HEAD_DIM
DECAY_LORA = 64
AAA_LORA = 64
GATE_LORA = 128
RWKV_COLS = 3 * HDIM + DECAY_LORA + AAA_LORA + GATE_LORA
IDX_DIM = 64
MAX_TOPK = 256
N_BUCKETS = 32
MAX_EXACT = 16
MAX_DISTANCE = 128
N_GROUPS = 4
EXPERTS_PER_GROUP = 8
N_EXPERTS = 32
D_EXPERT = 512
GN_EPS = 64e-5
LN_EPS = 1e-5
ALPHA = 2.0 ** 0.25
LANES = 128
CHUNK = 64
SCAN_SUB = 4
QB = 128
KCH = 512
ICH = 512
assert ICH in (KCH, KCH // 2)
TM_INPROJ = 512
TM_PREP = 256
TM_MERGE = 512
TM_EXPERT = 512
TM_TOKEN = 256
INT_MIN = -(2 ** 31)
NEG = -1e30
LOG2E = 1.4426950408889634
VMEM_LIMIT = 56 * 1024 * 1024

NN = (((1,), (0,)), ((), ()))
NT = (((1,), (1,)), ((), ()))


def _dot(a, b, dims=NN):
    return lax.dot_general(a, b, dims, preferred_element_type=F32)


def _split2(x):
    hi = x.astype(BF16)
    lo = (x - hi.astype(F32)).astype(BF16)
    return hi, lo


def _split3(x):
    hi = x.astype(BF16)
    r1 = x - hi.astype(F32)
    mid = r1.astype(BF16)
    lo = (r1 - mid.astype(F32)).astype(BF16)
    return hi, mid, lo


def _mm1(a, b, dims=NN):
    return _dot(a.astype(BF16), b.astype(BF16), dims)


def _mm3(a, b, dims=NN):
    ah, al = _split2(a)
    bh, bl = _split2(b)
    return _dot(ah, bh, dims) + (_dot(ah, bl, dims) + _dot(al, bh, dims))


_mm_misc = _mm1
_mm_inv = _mm1
_mm_state = _mm3


def _mm_exact_lhs(a_bf, b):
    b0, b1, b2 = _split3(b)
    return _dot(a_bf, b0) + (_dot(a_bf, b1) + _dot(a_bf, b2))


def _mm_exact_rhs(a, b_bf):
    a0, a1 = _split2(a)
    return _dot(a0, b_bf) + _dot(a1, b_bf)


def _params(sem, vmem=VMEM_LIMIT):
    return pltpu.CompilerParams(dimension_semantics=sem, vmem_limit_bytes=vmem)


def _full(shape):
    nd = len(shape)
    return pl.BlockSpec(shape, lambda *_: (0,) * nd)


def _inproj_kernel(x_ref, wr_ref, wq_ref, wc_ref, lng_ref, lnb_ref,
                   pr_ref, q_ref, k_ref, ve_ref, vo_ref, qi_ref, kz_ref, kw_ref):
    xb = x_ref[...].astype(BF16)
    pr_ref[...] = _dot(xb, wr_ref[...])
    qkv = _dot(xb, wq_ref[...])
    q_ref[...] = (qkv[:, 0:HDIM] * (HEAD_DIM ** -0.5 * LOG2E)).astype(BF16)
    k_ref[...] = qkv[:, HDIM:2 * HDIM].astype(BF16)
    v = qkv[:, 2 * HDIM:3 * HDIM]
    even = (lax.broadcasted_iota(I32, v.shape, 1) % LANES) < HEAD_DIM
    ve_ref[...] = jnp.where(even, v, 1.0).astype(BF16)
    vo_ref[...] = jnp.where(even, 1.0, v).astype(BF16)
    qi_ref[...] = qkv[:, 3 * HDIM:4 * HDIM].astype(BF16)
    c = _dot(xb, wc_ref[...])
    kw_ref[...] = c
    lane = lax.broadcasted_iota(I32, c.shape, 1)
    isk = lane < IDX_DIM
    mu = jnp.sum(jnp.where(isk, c, 0.0), axis=1, keepdims=True) * (1.0 / IDX_DIM)
    d = jnp.where(isk, c - mu, 0.0)
    var = jnp.sum(d * d, axis=1, keepdims=True) * (1.0 / IDX_DIM)
    kn = d * lax.rsqrt(var + LN_EPS) * lng_ref[...] + lnb_ref[...]
    kz_ref[:, 0:LANES] = kn.astype(BF16)
    kz_ref[:, LANES:2 * LANES] = pltpu.roll(kn, IDX_DIM, 1).astype(BF16)


def _inproj(x2, wr, wq, wc, lng, lnb, tm):
    T = x2.shape[0]
    row = lambda n: pl.BlockSpec((tm, n), lambda i: (i, 0))
    return pl.pallas_call(
        _inproj_kernel,
        grid=(T // tm,),
        in_specs=[row(D_MODEL), _full(wr.shape), _full(wq.shape), _full(wc.shape),
                  _full(lng.shape), _full(lnb.shape)],
        out_specs=[row(RWKV_COLS), row(HDIM), row(HDIM), row(HDIM), row(HDIM), row(HDIM),
                   row(2 * LANES), row(LANES)],
        out_shape=[jax.ShapeDtypeStruct((T, RWKV_COLS), F32)]
        + [jax.ShapeDtypeStruct((T, HDIM), BF16)] * 5
        + [jax.ShapeDtypeStruct((T, 2 * LANES), BF16), jax.ShapeDtypeStruct((T, LANES), F32)],
        compiler_params=_params(("parallel",)),
        name="inproj",
    )(x2, wr, wq, wc, lng, lnb)


def _softplus(x):
    return jnp.maximum(x, 0.0) + jnp.log(1.0 + jnp.exp(-jnp.abs(x)))


def _sigmoid(x):
    return 1.0 / (1.0 + jnp.exp(-x))


def _prep_kernel(tiles_per_seq, p_ref, pp_ref, mu_ref, w0_ref, a0_ref, kk_ref, ka_ref, rk_ref,
                 wup_ref, aup_ref, gup_ref, bd_ref,
                 r_ref, lw_ref, k_ref, v_ref, a_ref, b_ref, g_ref, bon_ref):
    i = pl.program_id(0)
    p = p_ref[...]
    tm = p.shape[0]
    first = (i % tiles_per_seq) == 0
    prow = jnp.where(first, 0.0, pp_ref[7:8, :])
    rowid = lax.broadcasted_iota(I32, p.shape, 0)
    prev = jnp.where(rowid == 0, prow, pltpu.roll(p, 1, 0))
    ps = p + (prev - p) * mu_ref[...]
    r = ps[:, 0:HDIM]
    k = ps[:, HDIM:2 * HDIM]
    v = ps[:, 2 * HDIM:3 * HDIM]
    da = ps[:, 3 * HDIM:3 * HDIM + LANES]
    gd = ps[:, 3 * HDIM + LANES:3 * HDIM + 2 * LANES]
    w = -_softplus(-(w0_ref[...] + _mm3(jnp.tanh(da), wup_ref[...]))) - 0.5
    lw_ref[...] = -jnp.exp(w)
    a = _sigmoid(a0_ref[...] + _mm3(da, aup_ref[...]))
    g_ref[...] = _mm3(_sigmoid(gd), gup_ref[...])
    bd = bd_ref[...]
    kk = k * kk_ref[...]
    ss = _mm_exact_rhs(kk * kk, bd)
    kk = kk / jnp.maximum(jnp.sqrt(ss), 1e-12)
    k2 = k * (1.0 + (a - 1.0) * ka_ref[...])
    r_ref[...] = r
    k_ref[...] = k2
    v_ref[...] = v
    a_ref[...] = -kk
    b_ref[...] = kk * a
    bon_ref[...] = _mm_exact_rhs(r * k2 * rk_ref[...], bd) * v


def _rwkv_prep(pr, S, prm, tm):
    T = pr.shape[0]
    row = lambda n: pl.BlockSpec((tm, n), lambda i: (i, 0))
    prev = pl.BlockSpec((8, RWKV_COLS), lambda i: (jnp.maximum(i * (tm // 8) - 1, 0), 0))
    names = ["mu", "w0", "a0", "k_k", "k_a", "r_k", "wup", "aup", "gup", "bd"]
    return pl.pallas_call(
        functools.partial(_prep_kernel, S // tm),
        grid=(T // tm,),
        in_specs=[row(RWKV_COLS), prev] + [_full(prm[n].shape) for n in names],
        out_specs=[row(HDIM)] * 8,
        out_shape=[jax.ShapeDtypeStruct((T, HDIM), F32)] * 8,
        compiler_params=_params(("parallel",)),
        name="rwkv_prep",
    )(pr, pr, *[prm[n] for n in names])


def _scan_kernel(r_ref, lw_ref, k_ref, v_ref, a_ref, b_ref, g_ref, bon_ref, lng_ref, lnb_ref, bd_ref,
                 o_ref, st_ref, y_ref):
    C, N, H = CHUNK, HEAD_DIM, N_HEADS

    @pl.when(pl.program_id(1) == 0)
    def _():
        st_ref[...] = jnp.zeros(st_ref.shape, F32)

    ri = lax.broadcasted_iota(I32, (C, C), 0)
    ci = lax.broadcasted_iota(I32, (C, C), 1)
    incl = ri >= ci
    strict = ri > ci
    eye = ri == ci
    eye_f = jnp.where(eye, 1.0, 0.0)
    lmat = jnp.where(incl, 1.0, 0.0).astype(BF16)
    sls = [slice(h * N, (h + 1) * N) for h in range(H)]
    units = [(s, h) for s in range(SCAN_SUB) for h in range(H)]
    ah, rh, vh, bT, kT, bhT, khT, gam = {}, {}, {}, {}, {}, {}, {}, {}
    for s in range(SCAN_SUB):
        rows = slice(s * C, (s + 1) * C)
        lw = lw_ref[rows, :]
        cum = _mm_exact_lhs(lmat, lw)
        last = cum[C - 1:C, :]
        e_i = jnp.exp(-cum)
        e_end = jnp.exp(last - cum)
        g_s = jnp.exp(last)
        r_t = r_ref[rows, :] * jnp.exp(cum)
        a_t = a_ref[rows, :] * jnp.exp(cum - lw)
        v_s = v_ref[rows, :]
        b_s, k_s = b_ref[rows, :], k_ref[rows, :]
        bT_s, kT_s = (b_s * e_i).T, (k_s * e_i).T
        bhT_s, khT_s = (b_s * e_end).T, (k_s * e_end).T
        for h in range(H):
            un = (s, h)
            ah[un], rh[un], vh[un] = a_t[:, sls[h]], r_t[:, sls[h]], v_s[:, sls[h]]
            bT[un], kT[un] = bT_s[sls[h], :], kT_s[sls[h], :]
            bhT[un], khT[un] = bhT_s[sls[h], :], khT_s[sls[h], :]
            gam[un] = g_s[:, sls[h]]

    ar = {un: jnp.concatenate([ah[un], rh[un]], axis=0) for un in units}
    sb = {un: _mm_misc(ar[un], bT[un]) for un in units}
    sk = {un: _mm_misc(ar[un], kT[un]) for un in units}
    a_ab = {un: jnp.where(strict, sb[un][:C], 0.0) for un in units}
    a_rb = {un: jnp.where(incl, sb[un][C:], 0.0) for un in units}
    a_ak = {un: jnp.where(strict, sk[un][:C], 0.0) for un in units}
    a_rk = {un: jnp.where(incl, sk[un][C:], 0.0) for un in units}
    u = {un: _mm_misc(a_ak[un], vh[un]) for un in units}
    tinv = {un: eye_f + a_ab[un] for un in units}
    xp = a_ab
    for _ in range(5):
        xp = {un: _mm_inv(xp[un], xp[un]) for un in units}
        tinv = {un: tinv[un] + _mm_inv(tinv[un], xp[un]) for un in units}
    pm = {un: _mm_misc(tinv[un], ah[un]) for un in units}
    qm = {un: _mm_misc(tinv[un], u[un]) for un in units}
    r2 = {un: rh[un] + _mm_misc(a_rb[un], pm[un]) for un in units}
    mmat = {un: jnp.where(eye, gam[un], 0.0) + _mm_misc(bhT[un], pm[un]) for un in units}
    y0 = {un: _mm_misc(a_rb[un], qm[un]) + _mm_misc(a_rk[un], vh[un]) for un in units}
    gmat = {un: _mm_misc(bhT[un], qm[un]) + _mm_misc(khT[un], vh[un]) for un in units}
    st = [st_ref[h] for h in range(H)]
    for s in range(SCAN_SUB):
        for h in range(H):
            y_ref[s * C:(s + 1) * C, sls[h]] = _mm_misc(r2[(s, h)], st[h]) + y0[(s, h)]
        st = [_mm_state(mmat[(s, h)], st[h]) + gmat[(s, h)] for h in range(H)]
    for h in range(H):
        st_ref[h] = st[h]

    y = y_ref[...]
    bd = bd_ref[...]
    mu = _mm_exact_rhs(y, bd) * (1.0 / N)
    d = y - mu
    var = _mm_exact_rhs(d * d, bd) * (1.0 / N)
    yn = d * lax.rsqrt(var + GN_EPS) * lng_ref[...] + lnb_ref[...]
    o_ref[...] = ((yn + bon_ref[...]) * g_ref[...]).astype(BF16)


def _rwkv_scan(arrs, lng, lnb, bd, B, S):
    rows = SCAN_SUB * CHUNK
    nc = S // rows
    row = pl.BlockSpec((rows, HDIM), lambda b, c: (b * nc + c, 0))
    return pl.pallas_call(
        _scan_kernel,
        grid=(B, nc),
        in_specs=[row] * 8 + [_full(lng.shape), _full(lnb.shape), _full(bd.shape)],
        out_specs=row,
        out_shape=jax.ShapeDtypeStruct((B * S, HDIM), BF16),
        scratch_shapes=[pltpu.VMEM((N_HEADS, HEAD_DIM, HEAD_DIM), F32), pltpu.VMEM((rows, HDIM), F32)],
        compiler_params=_params(("parallel", "arbitrary")),
        name="rwkv_scan",
    )(*arrs, lng, lnb, bd)


def _sort_key(x):
    bits = pltpu.bitcast(x, I32)
    return bits ^ ((bits >> 31) & 0x7FFFFFFF)


def _attn_kernel(top_k, rb_ref, q_ref, k_ref, ve_ref, vo_ref, qi_ref, kz_ref, kw_ref, o_ref,
                 key_t, mbias, btab, qm, s_scr, p_scr, *state):
    m_scr, acc, a_scr = (state[n * N_HEADS:(n + 1) * N_HEADS] for n in range(3))
    i = pl.program_id(1)
    t0 = i * QB
    lane = lax.broadcasted_iota(I32, (QB, LANES), 1)
    rowi = lax.broadcasted_iota(I32, (QB, LANES), 0)

    @pl.when(i == 0)
    def _build_bias():
        for h in range(N_HEADS):
            btab[h, 2] = jnp.zeros((QB, LANES), F32)
        for m in range(2):
            n = jnp.maximum(m * QB + rowi - lane, 0)
            nf = jnp.maximum(n, 1).astype(F32)
            large = MAX_EXACT + (jnp.log(nf / MAX_EXACT) / math.log(MAX_DISTANCE / MAX_EXACT)
                                 * (N_BUCKETS - MAX_EXACT)).astype(I32)
            bucket = jnp.where(n < MAX_EXACT, n, jnp.minimum(large, N_BUCKETS - 1))
            for h in range(N_HEADS):
                t = jnp.zeros((QB, LANES), F32)
                for bk in range(N_BUCKETS):
                    t = jnp.where(bucket == bk, rb_ref[bk, h], t)
                btab[h, m] = (t - rb_ref[N_BUCKETS - 1, h]) * LOG2E

    n_ch = i // (KCH // QB) + 1
    n_ic = i // (ICH // QB) + 1
    nsub, isub = KCH // LANES, ICH // LANES

    kw_t = kw_ref[...].T
    w_t = [kw_t[IDX_DIM + h:IDX_DIM + h + 1, :] * (N_HEADS ** -0.5) * (IDX_DIM ** -0.5) for h in range(N_HEADS)]

    def score_chunk(c, carry):
        c0 = pl.multiple_of(c * ICH, ICH)
        tot = [jnp.zeros((LANES, QB), F32) for _ in range(isub)]
        for p in range(N_HEADS // 2):
            qp = qi_ref[:, p * LANES:(p + 1) * LANES]
            for e in range(2):
                z = _dot(kz_ref[pl.ds(c0, ICH), e * LANES:(e + 1) * LANES], qp, NT)
                for s in range(isub):
                    tot[s] = tot[s] + jnp.maximum(z[s * LANES:(s + 1) * LANES, :], 0.0) * w_t[2 * p + e]
        for s in range(isub):
            causal = c0 + s * LANES + rowi <= t0 + lane
            key_t[pl.ds(pl.multiple_of(c0 + s * LANES, LANES), LANES), :] = \
                jnp.where(causal, _sort_key(tot[s]), INT_MIN)
        return carry

    def score_pair(j, carry):
        score_chunk(2 * j, carry)
        return score_chunk(jnp.minimum(2 * j + 1, n_ic - 1), carry)
    lax.fori_loop(0, (n_ic + 1) // 2, score_pair, 0)

    def count_ge(cand):
        def body(c, cnt):
            kc = key_t[pl.ds(pl.multiple_of(c * ICH, ICH), ICH), :]
            ones = jnp.where(kc >= cand, 1, 0)
            return cnt + jnp.sum(ones.reshape(ICH // 8, 8, QB), axis=0)
        cnt = lax.fori_loop(0, n_ic, body, jnp.zeros((8, QB), I32))
        return jnp.sum(cnt, axis=0, keepdims=True)

    def search(bit, carry):
        u, cu = carry
        cand = u | (jnp.int32(1) << (31 - bit))
        cnt = count_ge(cand ^ INT_MIN)
        ok = cnt >= top_k
        return jnp.where(ok, cand, u), jnp.where(ok, cnt, cu)

    few = (t0 + lax.broadcasted_iota(I32, (1, QB), 1)) < top_k

    u, cu = search(0, (jnp.zeros((1, QB), I32), jnp.zeros((1, QB), I32)))
    at_zero = (cu >= top_k) & (count_ge(jnp.ones((1, QB), I32)) < top_k)

    def unsettled(carry):
        g, _, cu = carry
        return (g < 8) & (jnp.min(jnp.where(few | at_zero | (cu == top_k), 1, 0)) == 0)

    def four_bits(carry):
        g, u, cu = carry
        u, cu = lax.fori_loop(4 * g, 4 * g + 4, search, (u, cu))
        return g + 1, u, cu

    u, cu = lax.fori_loop(1, 24, search, (u, cu))
    _, u, cu = lax.while_loop(unsettled, four_bits, (jnp.int32(6), u, cu))
    thr = jnp.maximum(u ^ INT_MIN, INT_MIN + 1)

    @pl.when(jnp.max(cu) > top_k)
    def _fix_ties():
        budget = (top_k - count_ge(thr + 1)).astype(F32)
        kr = lax.broadcasted_iota(I32, (ICH, ICH), 0)
        kc_ = lax.broadcasted_iota(I32, (ICH, ICH), 1)
        lt = jnp.where(kr >= kc_, 1.0, 0.0).astype(BF16)

        def body(c, before):
            sl = pl.ds(pl.multiple_of(c * ICH, ICH), ICH)
            kc = key_t[sl, :]
            eq = kc == thr
            eqf = jnp.where(eq, 1.0, 0.0)
            upto = before + _dot(lt, eqf.astype(BF16))
            key_t[sl, :] = jnp.where(eq & (upto - eqf >= budget), INT_MIN, kc)
            return upto[ICH - 1:ICH, :]
        lax.fori_loop(0, n_ic, body, jnp.zeros((1, QB), F32))

    def mask_chunk(c, carry):
        c0 = pl.multiple_of(c * ICH, ICH)
        for s in range(isub):
            sl = pl.ds(pl.multiple_of(c0 + s * LANES, LANES), LANES)
            mbias[:, sl] = jnp.where(key_t[sl, :] >= thr, 0.0, NEG).T
        return carry
    lax.fori_loop(0, n_ic, mask_chunk, 0)

    if ICH < KCH:
        @pl.when(n_ic * ICH < n_ch * KCH)
        def _():
            mbias[:, pl.ds(pl.multiple_of(n_ic * ICH, ICH), ICH)] = jnp.full((QB, ICH), NEG, F32)

    q = q_ref[...]
    even = lane < HEAD_DIM
    for p in range(N_HEADS // 2):
        qp = q[:, p * LANES:(p + 1) * LANES]
        qm[2 * p] = jnp.where(even, qp, jnp.zeros_like(qp))
        qm[2 * p + 1] = jnp.where(even, jnp.zeros_like(qp), qp)
    for h in range(N_HEADS):
        m_scr[h][...] = jnp.full((QB, LANES), NEG, F32)
        acc[h][...] = jnp.zeros((QB, LANES), F32)

    def key_block(c, near):
        start = pl.multiple_of(c * KCH, KCH)
        rows = pl.ds(start, KCH)
        nsub = KCH // LANES
        for h in range(N_HEADS):
            cols = slice((h // 2) * LANES, (h // 2 + 1) * LANES)
            s_scr[h] = _dot(qm[h], k_ref[rows, cols], NT)
        for h in range(N_HEADS):
            sub = []
            for n in range(nsub):
                sc = s_scr[h, :, n * LANES:(n + 1) * LANES] + mbias[:, pl.ds(start + n * LANES, LANES)]
                if near:
                    sc = sc + btab[h, jnp.clip(i - (c * nsub + n), 0, 2)]
                sub.append(sc)
            mx = jnp.maximum(jnp.maximum(sub[0], sub[1]), jnp.maximum(sub[2], sub[3]))
            m_old = m_scr[h][...]
            m_new = jnp.maximum(m_old, jnp.max(mx, axis=1, keepdims=True))
            for n in range(nsub):
                p_scr[h, :, n * LANES:(n + 1) * LANES] = jnp.exp2(sub[n] - m_new).astype(BF16)
            a_scr[h][...] = jnp.exp2(m_old - m_new)
            m_scr[h][...] = m_new
        for h in range(N_HEADS):
            cols = slice((h // 2) * LANES, (h // 2 + 1) * LANES)
            v_ref = vo_ref if h % 2 else ve_ref
            acc[h][...] = a_scr[h][...] * acc[h][...] + _dot(p_scr[h], v_ref[rows, cols])

    n_far = jnp.maximum(i - 1, 0) // (KCH // QB)

    def far_body(c, carry):
        key_block(c, False)
        return carry
    lax.fori_loop(0, n_far, far_body, 0)

    def near_body(c, carry):
        key_block(c, True)
        return carry
    lax.fori_loop(n_far, n_ch, near_body, 0)

    for p in range(N_HEADS // 2):
        ae, ao = acc[2 * p][...], acc[2 * p + 1][...]
        oe = ae / pltpu.roll(ae, HEAD_DIM, 1)
        oo = ao / pltpu.roll(ao, HEAD_DIM, 1)
        o_ref[:, p * LANES:(p + 1) * LANES] = jnp.where(even, oe, oo).astype(BF16)


def _attention(q, k, ve, vo, qi, kz, kw, rel_bias, B, S):
    nq = S // QB
    top_k = min(MAX_TOPK, S // 4)
    qrow = lambda n: pl.BlockSpec((QB, n), lambda b, i: (b * nq + i, 0))
    seq = lambda n: pl.BlockSpec((S, n), lambda b, i: (b, 0))
    return pl.pallas_call(
        functools.partial(_attn_kernel, top_k),
        grid=(B, nq),
        in_specs=[pl.BlockSpec(memory_space=pltpu.SMEM), qrow(HDIM), seq(HDIM), seq(HDIM), seq(HDIM),
                  qrow(HDIM), seq(2 * LANES), qrow(LANES)],
        out_specs=qrow(HDIM),
        out_shape=jax.ShapeDtypeStruct((B * S, HDIM), BF16),
        scratch_shapes=[pltpu.VMEM((S, QB), I32),
                        pltpu.VMEM((QB, S), F32),
                        pltpu.VMEM((N_HEADS, 3, QB, LANES), F32),
                        pltpu.VMEM((N_HEADS, QB, LANES), BF16),
                        pltpu.VMEM((N_HEADS, QB, KCH), F32),
                        pltpu.VMEM((N_HEADS, QB, KCH), BF16)]
        + [pltpu.VMEM((QB, LANES), F32)] * (3 * N_HEADS),
        compiler_params=_params(("parallel", "arbitrary")),
        name="dsa_attention",
    )(rel_bias, q, k, ve, vo, qi, kz, kw)


def _layer_norm(x, g, b):
    mu = jnp.mean(x, axis=1, keepdims=True)
    d = x - mu
    var = jnp.mean(d * d, axis=1, keepdims=True)
    return d * lax.rsqrt(var + LN_EPS) * g + b


def _merge_kernel(x_ref, ya_ref, at_ref, wg_ref, wa_ref, wb_ref, wo_ref, g1_ref, b1_ref, wr_ref, br_ref,
                  h_ref, ri_ref, cnt_ref):
    @pl.when(pl.program_id(0) == 0)
    def _():
        cnt_ref[...] = jnp.zeros(cnt_ref.shape, F32)
    half = x_ref.shape[0] // 2
    halves = [slice(sb * half, (sb + 1) * half) for sb in range(2)]
    pre = []
    for rows in halves:
        g = _dot(x_ref[rows, :].astype(BF16), wg_ref[...])
        pre.append((g, _dot(ya_ref[rows, :], wa_ref[...]), _dot(at_ref[rows, :], wb_ref[...])))
    mixes = []
    for g, ya, yb in pre:
        mixin = _sigmoid(g[:, :D_MODEL]) * ya + _sigmoid(g[:, D_MODEL:]) * yb
        mixes.append(_dot(mixin.astype(BF16), wo_ref[...]))
    for rows, mix in zip(halves, mixes):
        h_ref[rows, :] = _layer_norm(ALPHA * x_ref[rows, :] + mix, g1_ref[...], b1_ref[...])
    for rows in halves:
        _route_rows(h_ref.at[rows], wr_ref, br_ref, ri_ref.at[rows], cnt_ref)


def _route_rows(h_ref, wr_ref, br_ref, ri_ref, cnt_ref):
    lg = _mm3(h_ref[...], wr_ref[...]) + br_ref[...]
    lane = lax.broadcasted_iota(I32, lg.shape, 1)
    gl = jnp.where(lane < N_GROUPS, lg, NEG)
    gmax = jnp.max(gl, axis=1, keepdims=True)
    p_g = 1.0 / jnp.sum(jnp.exp(gl - gmax), axis=1, keepdims=True)
    gsel = jnp.min(jnp.where(gl == gmax, lane, LANES), axis=1, keepdims=True)
    lo = N_GROUPS + EXPERTS_PER_GROUP * gsel
    el = jnp.where((lane >= lo) & (lane < lo + EXPERTS_PER_GROUP), lg, NEG)
    e1 = jnp.max(el, axis=1, keepdims=True)
    i1 = jnp.min(jnp.where(el == e1, lane, LANES), axis=1, keepdims=True)
    el2 = jnp.where(lane == i1, NEG, el)
    e2 = jnp.max(el2, axis=1, keepdims=True)
    i2 = jnp.min(jnp.where(el2 == e2, lane, LANES), axis=1, keepdims=True)
    w2 = jnp.exp(e2 - e1)
    gate1 = p_g / (1.0 + w2)
    gate2 = p_g * w2 / (1.0 + w2)
    tm = lg.shape[0]
    oh = jnp.concatenate([jnp.where(lane == i1 - N_GROUPS, 1.0, 0.0),
                          jnp.where(lane == i2 - N_GROUPS, 1.0, 0.0)], axis=0)
    rr = lax.broadcasted_iota(I32, (2 * tm, 2 * tm), 0)
    cc = lax.broadcasted_iota(I32, (2 * tm, 2 * tm), 1)
    before = _dot(jnp.where(rr > cc, 1.0, 0.0).astype(BF16), oh.astype(BF16))
    rank = jnp.sum((before + cnt_ref[0:1, :]) * oh, axis=1, keepdims=True)
    cnt_ref[...] = cnt_ref[...] + jnp.sum(oh, axis=0, keepdims=True)
    cols = [(i1 - N_GROUPS).astype(F32), (i2 - N_GROUPS).astype(F32), gate1, gate2, rank[:tm], rank[tm:]]
    ri = jnp.zeros(lg.shape, F32)
    for n, col in enumerate(cols):
        ri = jnp.where(lane == n, col, ri)
    ri_ref[...] = ri


def _merge(x2, ya, at, wg, wa, wb, wo, g1, b1, wr, br, tm):
    T = x2.shape[0]
    row = lambda n: pl.BlockSpec((tm, n), lambda i: (i, 0))
    ws = [wg, wa, wb, wo, g1, b1, wr, br]
    return pl.pallas_call(
        _merge_kernel,
        grid=(T // tm,),
        in_specs=[row(D_MODEL), row(HDIM), row(HDIM)] + [_full(w.shape) for w in ws],
        out_specs=[row(D_MODEL), row(LANES), _full((8, LANES))],
        out_shape=[jax.ShapeDtypeStruct((T, D_MODEL), F32), jax.ShapeDtypeStruct((T, LANES), F32),
                   jax.ShapeDtypeStruct((8, LANES), F32)],
        compiler_params=_params(("arbitrary",)),
        name="merge_router",
    )(x2, ya, at, *ws)


def _dispatch_kernel(p0_ref, p1_ref, h_ref, xs_in, xs_hbm, sem):
    del xs_in
    tm = h_ref.shape[0]

    def issue(r, carry):
        src = h_ref.at[pl.ds(r, 1)]
        pltpu.make_async_copy(src, xs_hbm.at[pl.ds(p0_ref[0, 0, r], 1)], sem).start()
        pltpu.make_async_copy(src, xs_hbm.at[pl.ds(p1_ref[0, 0, r], 1)], sem).start()
        return carry
    lax.fori_loop(0, tm, issue, 0, unroll=8)
    for _ in range(2):
        pltpu.make_async_copy(h_ref, xs_hbm.at[pl.ds(0, tm)], sem).wait()


def _dispatch(pos0, pos1, h, n_rows, tm):
    T = h.shape[0]
    idx = pl.BlockSpec((1, 1, tm), lambda i: (i, 0, 0), memory_space=pltpu.SMEM)
    return pl.pallas_call(
        _dispatch_kernel,
        grid=(T // tm,),
        in_specs=[idx, idx, pl.BlockSpec((tm, D_MODEL), lambda i: (i, 0)), pl.BlockSpec(memory_space=pl.ANY)],
        out_specs=pl.BlockSpec(memory_space=pl.ANY),
        out_shape=jax.ShapeDtypeStruct((n_rows, D_MODEL), F32),
        scratch_shapes=[pltpu.SemaphoreType.DMA(())],
        input_output_aliases={3: 0},
        compiler_params=_params(("arbitrary",)),
        name="moe_dispatch",
    )(pos0, pos1, h, jnp.zeros((n_rows, D_MODEL), F32))


def _moe_kernel(te_ref, nu_ref, xs_ref, wg_ref, wu_ref, wd_ref, o_ref):
    t = pl.program_id(0)

    @pl.when(t < nu_ref[0])
    def _():
        xb = xs_ref[...].astype(BF16)
        hg = _dot(xb, wg_ref[0])
        hu = _dot(xb, wu_ref[0])
        act = (hg * _sigmoid(hg)) * hu
        o_ref[...] = _dot(act.astype(BF16), wd_ref[0])

    @pl.when(t >= nu_ref[0])
    def _():
        o_ref[...] = jnp.zeros(o_ref.shape, F32)


def _moe(tile_e, n_used, xs, wg, wu, wd, tm):
    n_tiles = xs.shape[0] // tm
    grid_spec = pltpu.PrefetchScalarGridSpec(
        num_scalar_prefetch=2,
        grid=(n_tiles,),
        in_specs=[pl.BlockSpec((tm, D_MODEL), lambda t, te, nu: (t, 0)),
                  pl.BlockSpec((1, D_MODEL, D_EXPERT), lambda t, te, nu: (te[t], 0, 0)),
                  pl.BlockSpec((1, D_MODEL, D_EXPERT), lambda t, te, nu: (te[t], 0, 0)),
                  pl.BlockSpec((1, D_EXPERT, D_MODEL), lambda t, te, nu: (te[t], 0, 0))],
        out_specs=pl.BlockSpec((tm, D_MODEL), lambda t, te, nu: (t, 0)),
    )
    return pl.pallas_call(
        _moe_kernel,
        grid_spec=grid_spec,
        out_shape=jax.ShapeDtypeStruct((n_tiles * tm, D_MODEL), F32),
        compiler_params=_params(("arbitrary",)),
        name="moe_experts",
    )(tile_e, n_used, xs, wg, wu, wd)


def _final_kernel(p0_ref, p1_ref, n0_ref, n1_ref, h_ref, ri_ref, ys_hbm, g2_ref, b2_ref, o_ref, y0, y1, sems):
    i = pl.program_id(0)
    tm = h_ref.shape[0]
    slot = i % 2

    def issue(idx0, idx1, buf):
        def body(r, carry):
            pltpu.make_async_copy(ys_hbm.at[pl.ds(idx0[0, 0, r], 1)], y0.at[buf, pl.ds(r, 1)], sems.at[buf]).start()
            pltpu.make_async_copy(ys_hbm.at[pl.ds(idx1[0, 0, r], 1)], y1.at[buf, pl.ds(r, 1)], sems.at[buf]).start()
            return carry
        lax.fori_loop(0, tm, body, 0, unroll=8)

    @pl.when(i == 0)
    def _():
        issue(p0_ref, p1_ref, 0)

    @pl.when(i + 1 < pl.num_programs(0))
    def _():
        issue(n0_ref, n1_ref, 1 - slot)

    pltpu.make_async_copy(ys_hbm.at[pl.ds(0, tm)], y0.at[slot], sems.at[slot]).wait()
    pltpu.make_async_copy(ys_hbm.at[pl.ds(0, tm)], y1.at[slot], sems.at[slot]).wait()
    ri = ri_ref[...]
    moe = y0[slot] * ri[:, 2:3] + y1[slot] * ri[:, 3:4]
    o_ref[...] = _layer_norm(ALPHA * h_ref[...] + moe, g2_ref[...], b2_ref[...])


def _final(pos0, pos1, h, ri, ys, g2, b2, tm):
    T = h.shape[0]
    n = T // tm
    idx = pl.BlockSpec((1, 1, tm), lambda i: (i, 0, 0), memory_space=pltpu.SMEM)
    nxt = pl.BlockSpec((1, 1, tm), lambda i: (jnp.minimum(i + 1, n - 1), 0, 0), memory_space=pltpu.SMEM)
    row = lambda c: pl.BlockSpec((tm, c), lambda i: (i, 0))
    return pl.pallas_call(
        _final_kernel,
        grid=(n,),
        in_specs=[idx, idx, nxt, nxt, row(D_MODEL), row(LANES), pl.BlockSpec(memory_space=pl.ANY),
                  _full(g2.shape), _full(b2.shape)],
        out_specs=row(D_MODEL),
        out_shape=jax.ShapeDtypeStruct((T, D_MODEL), F32),
        scratch_shapes=[pltpu.VMEM((2, tm, D_MODEL), F32), pltpu.VMEM((2, tm, D_MODEL), F32),
                        pltpu.SemaphoreType.DMA((2,))],
        compiler_params=_params(("arbitrary",)),
        name="combine_ln",
    )(pos0, pos1, pos0, pos1, h, ri, ys, g2, b2)


def _routing_tables(eid, rank, sizes, tm):
    T = eid.shape[0]
    n_tiles = (2 * T) // tm + N_EXPERTS
    padded = ((sizes + tm - 1) // tm) * tm
    pad_end = jnp.cumsum(padded)
    pad_off = pad_end - padded
    pos = jnp.take(pad_off, eid) + rank
    tile_start = jnp.arange(n_tiles, dtype=I32) * tm
    tile_e = jnp.minimum(jnp.sum(tile_start[:, None] >= pad_end[None, :], axis=1), N_EXPERTS - 1).astype(I32)
    n_used = (pad_end[-1] // tm).astype(I32).reshape(1)
    return tile_e, n_used, pos, n_tiles * tm


def _block_diag_ones():
    hid = jnp.arange(HDIM, dtype=I32) // HEAD_DIM
    return (hid[:, None] == hid[None, :]).astype(BF16)


def _layer(x, w_in, mu_shift, w0, w_lora_up, a0, a_lora_up, g_lora_up, k_k, k_a, r_k, ln_x_g, ln_x_b,
           w_branch_a, idx_k_ln_g, idx_k_ln_b, rel_bias, w_branch_b, w_out, ln1_g, ln1_b,
           w_router_grp, b_router_grp, w_router_exp, b_router_exp, w_gate, w_up, w_down, ln2_g, ln2_b):
    B, S, _ = x.shape
    T = B * S
    assert x.shape[2] == D_MODEL and S % KCH == 0 and S % (SCAN_SUB * CHUNK) == 0 and S % TM_PREP == 0
    assert T % max(TM_INPROJ, TM_MERGE, TM_EXPERT, TM_TOKEN) == 0
    x2 = x.reshape(T, D_MODEL)
    row = lambda t: t.reshape(1, -1)

    c_r, c_wd, c_k, c_v, c_ad, c_gd = 0, 512, 576, 1088, 1600, 1664
    perm = jnp.concatenate([jnp.arange(c_r, c_r + 512), jnp.arange(c_k, c_k + 512), jnp.arange(c_v, c_v + 512),
                            jnp.arange(c_wd, c_wd + 64), jnp.arange(c_ad, c_ad + 64),
                            jnp.arange(c_gd, c_gd + 128)])
    o_q = RWKV_COLS
    o_c = o_q + 4 * HDIM
    o_g = o_c + IDX_DIM + N_HEADS
    wr = w_in[:, :RWKV_COLS][:, perm].astype(BF16)
    wq = w_in[:, o_q:o_c].astype(BF16)
    wc = jnp.pad(w_in[:, o_c:o_g], ((0, 0), (0, LANES - IDX_DIM - N_HEADS))).astype(BF16)
    wgates = w_in[:, o_g:].astype(BF16)
    pad_idx = lambda t: jnp.pad(t, (0, LANES - IDX_DIM)).reshape(1, LANES)

    pr, q, k, ve, vo, qi, kz, kw = _inproj(x2, wr, wq, wc, pad_idx(idx_k_ln_g), pad_idx(idx_k_ln_b), TM_INPROJ)

    bd = _block_diag_ones()
    prm = {
        "mu": row(mu_shift[perm]), "w0": row(w0), "a0": row(a0), "k_k": row(k_k), "k_a": row(k_a),
        "r_k": row(r_k),
        "wup": jnp.pad(w_lora_up, ((0, AAA_LORA), (0, 0))),
        "aup": jnp.pad(a_lora_up, ((DECAY_LORA, 0), (0, 0))),
        "gup": g_lora_up, "bd": bd,
    }
    arrs = _rwkv_prep(pr, S, prm, TM_PREP)
    ya = _rwkv_scan(arrs, row(ln_x_g), row(ln_x_b), bd, B, S)

    at = _attention(q, k, ve, vo, qi, kz, kw, rel_bias, B, S)

    w_router = jnp.pad(jnp.concatenate([w_router_grp, w_router_exp], axis=1),
                       ((0, 0), (0, LANES - N_GROUPS - N_EXPERTS)))
    b_router = jnp.pad(jnp.concatenate([b_router_grp, b_router_exp]), (0, LANES - N_GROUPS - N_EXPERTS))
    h1, ri, cnt = _merge(x2, ya, at, wgates, w_branch_a.astype(BF16), w_branch_b.astype(BF16),
                         w_out.astype(BF16), row(ln1_g), row(ln1_b), w_router, row(b_router), TM_MERGE)

    tile_e, n_used, pos, n_rows = _routing_tables(ri[:, 0:2].astype(I32), ri[:, 4:6].astype(I32),
                                                  cnt[0, :N_EXPERTS].astype(I32), TM_EXPERT)
    pos0 = pos[:, 0].reshape(T // TM_TOKEN, 1, TM_TOKEN)
    pos1 = pos[:, 1].reshape(T // TM_TOKEN, 1, TM_TOKEN)
    xs = _dispatch(pos0, pos1, h1, n_rows, TM_TOKEN)
    ys = _moe(tile_e, n_used, xs, w_gate.astype(BF16), w_up.astype(BF16), w_down.astype(BF16), TM_EXPERT)
    out = _final(pos0, pos1, h1, ri, ys, row(ln2_g), row(ln2_b), TM_TOKEN)
    return out.reshape(B, S, D_MODEL)


def kernel(x, w_in, mu_shift, w0, w_lora_up, a0, a_lora_up, g_lora_up, k_k, k_a, r_k, ln_x_g, ln_x_b, w_branch_a, idx_k_ln_g, idx_k_ln_b, rel_bias, w_branch_b, w_out, ln1_g, ln1_b, w_router_grp, b_router_grp, w_router_exp, b_router_exp, w_expert_gate, w_expert_up, w_expert_down, ln2_g, ln2_b):
    assert w_in.shape[0] == 1, "single-layer (DEPTH = 1) block"
    l = 0
    return _layer(x, w_in[l], mu_shift[l], w0[l], w_lora_up[l], a0[l], a_lora_up[l], g_lora_up[l], k_k[l],
                  k_a[l], r_k[l], ln_x_g[l], ln_x_b[l], w_branch_a[l], idx_k_ln_g[l], idx_k_ln_b[l], rel_bias,
                  w_branch_b[l], w_out[l], ln1_g[l], ln1_b[l], w_router_grp[l], b_router_grp[l],
                  w_router_exp[l], b_router_exp[l], w_expert_gate[l], w_expert_up[l], w_expert_down[l],
                  ln2_g[l], ln2_b[l])
```
